```python
import math
import jax
import jax.numpy as jnp
from jax import lax
import numpy as np

D_MODEL = 2048
BATCH = 4
SEQ = 2048
DEPTH = 2
DEC_BATCH = 32
DEC_SEQ = 8
PAST_LEN = 8192
PAGE_SIZE = 128

F32 = jnp.float32
RMS_EPS = 1e-6
NEG_INF = -1e30

GDN_K_HEADS = 16
GDN_V_HEADS = 32
GDN_DK = 128
GDN_DV = 128
GDN_KDIM = GDN_K_HEADS * GDN_DK
GDN_VDIM = GDN_V_HEADS * GDN_DV
GDN_CONV_DIM = 2 * GDN_KDIM + GDN_VDIM
GDN_CONV_W = 4
GDN_CHUNK = 64
GDN_PROJ = GDN_CONV_DIM + GDN_VDIM + 2 * GDN_V_HEADS

NSA_HEADS = 16
NSA_KV_HEADS = 4
NSA_HD = D_MODEL // NSA_HEADS
NSA_GROUP = NSA_HEADS // NSA_KV_HEADS
NSA_QDIM = NSA_HEADS * NSA_HD
NSA_KVDIM = NSA_KV_HEADS * NSA_HD
NSA_PROJ = NSA_QDIM + 6 * NSA_KVDIM + 3 * NSA_HEADS
CMP_BLOCK = 32
CMP_STRIDE = 16
CMP_HIDDEN = 2 * NSA_HD
SEL_BLOCK = 64
N_SEL = 16
WINDOW = 512
FORCE_BONUS = 1e3
SEL_QBLOCK = 64
WIN_QBLOCK = 128

N_BUCKETS = 32
REL_MAX_DIST = 1024

D_FF = ((8 * D_MODEL // 3 + 127) // 128) * 128
FFN_CONV_W = 3

N_GDN = (DEPTH + 1) // 2
N_NSA = DEPTH // 2

kernel_name = "hybrid_gdn_nsa_convffn_step"


def rmsnorm(x, w):
    xf = x.astype(F32)
    y = xf * lax.rsqrt(jnp.mean(xf * xf, axis=-1, keepdims=True) + RMS_EPS)
    return (y * w.astype(F32)).astype(x.dtype)


def l2norm(x):
    xf = x.astype(F32)
    return xf * lax.rsqrt(jnp.sum(xf * xf, axis=-1, keepdims=True) + RMS_EPS)


def causal_dwconv(x, prefix, w, b=None):
    width = w.shape[-1]
    t = x.shape[1]
    xp = jnp.concatenate([prefix.astype(x.dtype), x], axis=1)
    y = xp[:, 0:t] * w[:, 0]
    for i in range(1, width):
        y = y + xp[:, i:i + t] * w[:, i]
    if b is not None:
        y = y + b
    return y, xp[:, t:]


def t5_bucket(dist):
    d = jnp.maximum(dist, 0)
    max_exact = N_BUCKETS // 2
    scale = (N_BUCKETS - max_exact) / math.log(REL_MAX_DIST / max_exact)
    large = max_exact + (jnp.log(jnp.maximum(d, 1).astype(F32) / max_exact) * scale).astype(jnp.int32)
    return jnp.where(d < max_exact, d, jnp.minimum(large, N_BUCKETS - 1))


def gated_delta_rule(q, k, v, g, beta, s0):
    b, t, h, _ = q.shape
    dv = v.shape[-1]
    L = math.gcd(t, GDN_CHUNK)
    nc = t // L

    def chunks(a):
        a = a.reshape((b, nc, L, h) + a.shape[3:])
        return jnp.moveaxis(a, (1, 3), (0, 2))

    qc, kc, vc, gc, bc = chunks(q), chunks(k), chunks(v), chunks(g), chunks(beta)
    G = jnp.cumsum(gc, axis=-1)
    diff = G[..., :, None] - G[..., None, :]
    incl = jnp.tril(jnp.ones((L, L), bool))
    strict = jnp.tril(jnp.ones((L, L), bool), -1)
    dec_incl = jnp.exp(jnp.where(incl, diff, -jnp.inf))
    dec_strict = jnp.exp(jnp.where(strict, diff, -jnp.inf))
    kk = jnp.einsum('cbhid,cbhjd->cbhij', kc, kc)
    a_mat = jnp.eye(L, dtype=F32) + bc[..., :, None] * kk * dec_strict
    u_eff = lax.linalg.triangular_solve(a_mat, bc[..., None] * vc, left_side=True, lower=True, unit_diagonal=True)
    w_k = lax.linalg.triangular_solve(a_mat, (bc * jnp.exp(G))[..., None] * kc, left_side=True, lower=True, unit_diagonal=True)
    a_qk = jnp.einsum('cbhid,cbhjd->cbhij', qc, kc) * dec_incl
    q_dec = qc * jnp.exp(G)[..., None]
    g_last = G[..., -1:]
    k_dec = kc * jnp.exp(g_last - G)[..., None]

    def step(s, xs):
        u_c, w_c, aqk_c, q_c, k_c, gl_c = xs
        u = u_c - jnp.einsum('bhik,bhkv->bhiv', w_c, s)
        o = jnp.einsum('bhik,bhkv->bhiv', q_c, s) + jnp.einsum('bhij,bhjv->bhiv', aqk_c, u)
        s = jnp.exp(gl_c)[..., None] * s + jnp.einsum('bhik,bhiv->bhkv', k_c, u)
        return s, o

    s_fin, o = lax.scan(step, s0, (u_eff, w_k, a_qk, q_dec, k_dec, g_last))
    o = jnp.moveaxis(o, (0, 2), (1, 3)).reshape(b, t, h, dv)
    return o, s_fin


def gdn_mixer(h, s0, conv_prefix, w_in, conv_w, a_log, dt_bias, norm_w, w_out):
    b, t, _ = h.shape
    proj = h @ w_in
    qkv, z, bl, al = jnp.split(proj, [GDN_CONV_DIM, GDN_CONV_DIM + GDN_VDIM, GDN_CONV_DIM + GDN_VDIM + GDN_V_HEADS], axis=-1)
    qkv, new_prefix = causal_dwconv(qkv, conv_prefix, conv_w)
    qkv = jax.nn.silu(qkv)
    q = qkv[..., :GDN_KDIM].reshape(b, t, GDN_K_HEADS, GDN_DK)
    k = qkv[..., GDN_KDIM:2 * GDN_KDIM].reshape(b, t, GDN_K_HEADS, GDN_DK)
    v = qkv[..., 2 * GDN_KDIM:].reshape(b, t, GDN_V_HEADS, GDN_DV).astype(F32)
    rep = GDN_V_HEADS // GDN_K_HEADS
    q = jnp.repeat(l2norm(q), rep, axis=2) * (GDN_DK ** -0.5)
    k = jnp.repeat(l2norm(k), rep, axis=2)
    beta = jax.nn.sigmoid(bl.astype(F32))
    g = -jnp.exp(a_log.astype(F32)) * jax.nn.softplus(al.astype(F32) + dt_bias.astype(F32))
    o, s_new = gated_delta_rule(q, k, v, g, beta, s0.astype(F32))
    o = rmsnorm(o, norm_w) * jax.nn.silu(z.reshape(b, t, GDN_V_HEADS, GDN_DV).astype(F32))
    y = o.reshape(b, t, GDN_VDIM).astype(h.dtype) @ w_out
    return y, s_new.astype(s0.dtype), new_prefix


def nsa_compress(rows, pe, w1, w2):
    b, n = rows.shape[:2]
    ns = n // CMP_STRIDE
    nsub = CMP_BLOCK // CMP_STRIDE
    nc = ns - nsub + 1
    r = rows[:, :ns * CMP_STRIDE].reshape(b, ns, CMP_STRIDE, NSA_KV_HEADS, NSA_HD)
    w1r = w1.reshape(nsub, CMP_STRIDE, NSA_HD, CMP_HIDDEN)
    hid = pe.reshape(-1) @ w1
    for i in range(nsub):
        hid = hid + jnp.einsum('bnsgd,sde->bnge', r[:, i:i + nc], w1r[i])
    return jax.nn.gelu(hid) @ w2


def nsa_mixer(h, past, win_prefix, win_keep, w_in, q_norm, k_norm, cmp_pe, cmp_w1, cmp_w2, rel_bias, w_out):
    b, t, _ = h.shape
    p_len = past.shape[1]
    n_all = p_len + t
    proj = h @ w_in
    q = proj[..., :NSA_QDIM].reshape(b, t, NSA_HEADS, NSA_HD)
    kv = proj[..., NSA_QDIM:NSA_QDIM + 6 * NSA_KVDIM].reshape(b, t, 6, NSA_KV_HEADS, NSA_HD)
    gates = jax.nn.sigmoid(proj[..., NSA_QDIM + 6 * NSA_KVDIM:].astype(F32)).reshape(b, t, 3, NSA_HEADS)
    q = rmsnorm(q, q_norm) * (NSA_HD ** -0.5)
    qg = q.reshape(b, t, NSA_KV_HEADS, NSA_GROUP, NSA_HD)
    new_rows = jnp.stack([kv[:, :, 0], kv[:, :, 1], rmsnorm(kv[:, :, 2], k_norm[1]), kv[:, :, 3]], axis=2)
    new_win = jnp.stack([rmsnorm(kv[:, :, 4], k_norm[2]), kv[:, :, 5]], axis=2)
    rows = jnp.concatenate([past.astype(h.dtype), new_rows], axis=1)
    qpos = p_len + jnp.arange(t)
    rel_h = rel_bias.astype(F32)

    kc = rmsnorm(nsa_compress(rows[:, :, 0], cmp_pe[0], cmp_w1[0], cmp_w2[0]), k_norm[0])
    vc = nsa_compress(rows[:, :, 1], cmp_pe[1], cmp_w1[1], cmp_w2[1])
    nc = kc.shape[1]
    c_end = jnp.arange(nc) * CMP_STRIDE + (CMP_BLOCK - 1)
    c_dist = qpos[:, None] - c_end[None, :]
    c_ok = (c_dist >= 0)[:, None, None, :]
    c_bias = rel_h[t5_bucket(c_dist)].reshape(t, nc, NSA_KV_HEADS, NSA_GROUP).transpose(0, 2, 3, 1)
    s = jnp.einsum('btgjd,bngd->btgjn', qg, kc, preferred_element_type=F32) + c_bias
    p_cmp = jax.nn.softmax(jnp.where(c_ok, s, NEG_INF), axis=-1) * c_ok
    o_cmp = jnp.einsum('btgjn,bngd->btgjd', p_cmp.astype(vc.dtype), vc)

    n_sb = -(-n_all // SEL_BLOCK)
    sb_start = jnp.arange(n_sb) * SEL_BLOCK
    c_start = c_end - (CMP_BLOCK - 1)
    ovl = jnp.maximum(jnp.minimum(c_end[:, None], sb_start[None, :] + SEL_BLOCK - 1)
                      - jnp.maximum(c_start[:, None], sb_start[None, :]) + 1, 0).astype(F32) / CMP_BLOCK
    imp = jnp.einsum('btgjn,nm->btgm', p_cmp, ovl)
    cur = qpos // SEL_BLOCK
    blk = jnp.arange(n_sb)
    sb_ok = sb_start[None, :] <= qpos[:, None]
    forced = (blk[None, :] == 0) | (blk[None, :] == cur[:, None]) | (blk[None, :] == cur[:, None] - 1)
    score = jnp.where(sb_ok[:, None, :], imp + jnp.where(forced, FORCE_BONUS, 0.0)[:, None, :], NEG_INF)
    n_pick = min(N_SEL, n_sb)
    top_v, top_i = lax.top_k(score, n_pick)
    top_ok = top_v > 0.5 * NEG_INF

    pad = n_sb * SEL_BLOCK - n_all
    ks = jnp.pad(rows[:, :, 2], ((0, 0), (0, pad), (0, 0), (0, 0))).reshape(
        b, n_sb, SEL_BLOCK, NSA_KV_HEADS, NSA_HD).transpose(0, 3, 1, 2, 4)
    vs = jnp.pad(rows[:, :, 3], ((0, 0), (0, pad), (0, 0), (0, 0))).reshape(
        b, n_sb, SEL_BLOCK, NSA_KV_HEADS, NSA_HD).transpose(0, 3, 1, 2, 4)
    rel_g = rel_h.reshape(N_BUCKETS, NSA_KV_HEADS, NSA_GROUP).transpose(1, 0, 2)
    b_ix = jnp.arange(b)[:, None, None, None]
    g_ix = jnp.arange(NSA_KV_HEADS)[None, None, :, None]
    qb_len = math.gcd(t, SEL_QBLOCK)
    nqb = t // qb_len

    def sel_attend(args):
        q_b, pos_b, idx_b, ok_b = args
        kg = ks[b_ix, g_ix, idx_b]
        vg = vs[b_ix, g_ix, idx_b]
        kpos = idx_b[..., None] * SEL_BLOCK + jnp.arange(SEL_BLOCK)
        dist = pos_b[None, :, None, None, None] - kpos
        mask = (ok_b[..., None] & (dist >= 0))[:, :, :, None]
        bias = jnp.moveaxis(rel_g[g_ix[..., None], t5_bucket(dist)], -1, 3)
        sc = jnp.einsum('bqgjd,bqgnkd->bqgjnk', q_b, kg, preferred_element_type=F32) + bias
        sc = jnp.where(mask, sc, NEG_INF)
        p = jax.nn.softmax(sc.reshape(sc.shape[:4] + (-1,)), axis=-1).reshape(sc.shape)
        return jnp.einsum('bqgjnk,bqgnkd->bqgjd', p.astype(vg.dtype), vg)

    def to_blocks(a):
        return jnp.moveaxis(a.reshape((a.shape[0], nqb, qb_len) + a.shape[2:]), 1, 0)

    o_sel = lax.map(sel_attend, (to_blocks(qg), qpos.reshape(nqb, qb_len), to_blocks(top_i), to_blocks(top_ok)))
    o_sel = jnp.moveaxis(o_sel, 0, 1).reshape(b, t, NSA_HEADS, NSA_HD)

    rows_w = jnp.concatenate([win_prefix.astype(h.dtype), new_win], axis=1)
    wpos = p_len - WINDOW + jnp.arange(WINDOW + t)
    wb = math.gcd(t, WIN_QBLOCK)
    nwb = t // wb
    w_idx = jnp.arange(nwb)[:, None] * wb + jnp.arange(WINDOW + wb)[None, :]
    slab = rows_w[:, w_idx]
    kpos = wpos[w_idx]
    w_dist = qpos.reshape(nwb, wb)[:, :, None] - kpos[:, None, :]
    w_ok = (w_dist >= 0) & (w_dist < WINDOW) & (kpos[:, None, :] >= 0)
    w_bias = rel_h[t5_bucket(w_dist)].reshape(nwb, wb, WINDOW + wb, NSA_KV_HEADS, NSA_GROUP).transpose(0, 3, 4, 1, 2)
    qw = qg.reshape(b, nwb, wb, NSA_KV_HEADS, NSA_GROUP, NSA_HD)
    sw = jnp.einsum('bnqgjd,bnkgd->bngjqk', qw, slab[:, :, :, 0], preferred_element_type=F32) + w_bias
    pw = jax.nn.softmax(jnp.where(w_ok[:, None, None], sw, NEG_INF), axis=-1)
    o_win = jnp.einsum('bngjqk,bnkgd->bnqgjd', pw.astype(slab.dtype), slab[:, :, :, 1]).reshape(b, t, NSA_HEADS, NSA_HD)

    o = (gates[:, :, 0, :, None] * o_cmp.reshape(b, t, NSA_HEADS, NSA_HD)
         + gates[:, :, 1, :, None] * o_sel
         + gates[:, :, 2, :, None] * o_win)
    y = o.reshape(b, t, NSA_QDIM).astype(h.dtype) @ w_out
    return y, new_rows, rows_w[:, rows_w.shape[1] - win_keep:]


def conv_ffn(h, prefix, w_up, conv_w, conv_b, w_down):
    up = h @ w_up
    gate, val = up[..., :D_FF], up[..., D_FF:]
    gate, new_prefix = causal_dwconv(gate, prefix, conv_w, conv_b)
    return (jax.nn.silu(gate) * val) @ w_down, new_prefix


def setup_inputs(seed: int = 0) -> dict:
    key = jax.random.key(seed)
    ks = jax.random.split(key, 32)
    n_pages = PAST_LEN // PAGE_SIZE
    n_used = DEC_BATCH * n_pages
    n_pool = (n_used * 5) // 4
    win_buf = min(WINDOW, PAST_LEN)

    def nrm(k, shape, scale=1.0):
        return jax.random.normal(k, shape, F32) * scale

    def gain(k, shape):
        return 1.0 + nrm(k, shape, 0.02)

    page_table = jax.random.permutation(ks[7], n_pool)[:n_used].reshape(DEC_BATCH, n_pages).astype(jnp.int32)
    return {
        "x_prompt": nrm(ks[0], (BATCH, SEQ, D_MODEL)),
        "x_sample": nrm(ks[1], (DEC_BATCH, DEC_SEQ, D_MODEL)),
        "state_gdn": nrm(ks[2], (N_GDN, DEC_BATCH, GDN_V_HEADS, GDN_DK, GDN_DV), GDN_DK ** -0.5),
        "state_gdn_conv": nrm(ks[3], (N_GDN, DEC_BATCH, GDN_CONV_W - 1, GDN_CONV_DIM)),
        "cache_nsa_kv": nrm(ks[4], (N_NSA, n_pool, PAGE_SIZE, 4, NSA_KV_HEADS, NSA_HD)),
        "state_nsa_win": nrm(ks[5], (N_NSA, DEC_BATCH, win_buf, 2, NSA_KV_HEADS, NSA_HD)),
        "state_ffn_conv": nrm(ks[6], (DEPTH, DEC_BATCH, FFN_CONV_W - 1, D_FF)),
        "page_table": page_table,
        "norm_mix": gain(ks[8], (DEPTH, D_MODEL)),
        "norm_ffn": gain(ks[9], (DEPTH, D_MODEL)),
        "gdn_w_in": nrm(ks[10], (N_GDN, D_MODEL, GDN_PROJ), D_MODEL ** -0.5),
        "gdn_conv_w": nrm(ks[11], (N_GDN, GDN_CONV_DIM, GDN_CONV_W), GDN_CONV_W ** -0.5),
        "gdn_A_log": jnp.log(jax.random.uniform(ks[12], (N_GDN, GDN_V_HEADS), F32, 1.0, 16.0)),
        "gdn_dt_bias": nrm(ks[13], (N_GDN, GDN_V_HEADS), 0.1),
        "gdn_norm": gain(ks[14], (N_GDN, GDN_DV)),
        "gdn_w_out": nrm(ks[15], (N_GDN, GDN_VDIM, D_MODEL), GDN_VDIM ** -0.5),
        "nsa_w_in": nrm(ks[16], (N_NSA, D_MODEL, NSA_PROJ), D_MODEL ** -0.5),
        "nsa_q_norm": gain(ks[17], (N_NSA, NSA_HD)),
        "nsa_k_norm": gain(ks[18], (N_NSA, 3, NSA_HD)),
        "nsa_cmp_pe": nrm(ks[19], (N_NSA, 2, CMP_BLOCK, NSA_HD), 0.1),
        "nsa_cmp_w1": nrm(ks[20], (N_NSA, 2, CMP_BLOCK * NSA_HD, CMP_HIDDEN), (CMP_BLOCK * NSA_HD) ** -0.5),
        "nsa_cmp_w2": nrm(ks[21], (N_NSA, 2, CMP_HIDDEN, NSA_HD), CMP_HIDDEN ** -0.5),
        "rel_bias": nrm(ks[22], (N_BUCKETS, NSA_HEADS), 0.5),
        "nsa_w_out": nrm(ks[23], (N_NSA, NSA_QDIM, D_MODEL), NSA_QDIM ** -0.5),
        "ffn_w_up": nrm(ks[24], (DEPTH, D_MODEL, 2 * D_FF), D_MODEL ** -0.5),
        "ffn_conv_w": nrm(ks[25], (DEPTH, D_FF, FFN_CONV_W), FFN_CONV_W ** -0.5),
        "ffn_conv_b": nrm(ks[26], (DEPTH, D_FF), 0.02),
        "ffn_w_down": nrm(ks[27], (DEPTH, D_FF, D_MODEL), D_FF ** -0.5),
    }


def reference(x_prompt, x_sample, state_gdn, state_gdn_conv, cache_nsa_kv, state_nsa_win, state_ffn_conv,
              page_table, norm_mix, norm_ffn, gdn_w_in, gdn_conv_w, gdn_A_log, gdn_dt_bias, gdn_norm, gdn_w_out,
              nsa_w_in, nsa_q_norm, nsa_k_norm, nsa_cmp_pe, nsa_cmp_w1, nsa_cmp_w2, rel_bias, nsa_w_out,
              ffn_w_up, ffn_conv_w, ffn_conv_b, ffn_w_down):
    n_pages = PAST_LEN // PAGE_SIZE
    win_buf = state_nsa_win.shape[2]
    hp, hs = x_prompt, x_sample
    gdn_p, gdnc_p, kv_p, win_p, ffn_p = [], [], [], [], []
    gdn_s, gdnc_s, kv_s, win_s, ffn_s = [], [], [], [], []
    for i in range(DEPTH):
        j = i // 2
        hn_p = rmsnorm(hp, norm_mix[i])
        hn_s = rmsnorm(hs, norm_mix[i])
        if i % 2 == 0:
            gw = (gdn_w_in[j], gdn_conv_w[j], gdn_A_log[j], gdn_dt_bias[j], gdn_norm[j], gdn_w_out[j])
            mp, st_p, cv_p = gdn_mixer(hn_p, jnp.zeros((BATCH, GDN_V_HEADS, GDN_DK, GDN_DV), hp.dtype),
                                       jnp.zeros((BATCH, GDN_CONV_W - 1, GDN_CONV_DIM), hp.dtype), *gw)
            ms, st_s, cv_s = gdn_mixer(hn_s, state_gdn[j], state_gdn_conv[j], *gw)
            gdn_p.append(st_p)
            gdnc_p.append(cv_p)
            gdn_s.append(st_s)
            gdnc_s.append(cv_s)
        else:
            nw = (nsa_w_in[j], nsa_q_norm[j], nsa_k_norm[j], nsa_cmp_pe[j], nsa_cmp_w1[j], nsa_cmp_w2[j], rel_bias, nsa_w_out[j])
            mp, rows_p, wst_p = nsa_mixer(hn_p, jnp.zeros((BATCH, 0, 4, NSA_KV_HEADS, NSA_HD), hp.dtype),
                                          jnp.zeros((BATCH, WINDOW, 2, NSA_KV_HEADS, NSA_HD), hp.dtype),
                                          min(WINDOW, SEQ), *nw)
            past = cache_nsa_kv[j][page_table].reshape(DEC_BATCH, n_pages * PAGE_SIZE, 4, NSA_KV_HEADS, NSA_HD)
            prefix = jnp.pad(state_nsa_win[j], ((0, 0), (WINDOW - win_buf, 0), (0, 0), (0, 0), (0, 0)))
            ms, rows_s, wst_s = nsa_mixer(hn_s, past, prefix, win_buf, *nw)
            kv_p.append(rows_p)
            win_p.append(wst_p)
            kv_s.append(rows_s)
            win_s.append(wst_s)
        hp = hp + mp
        hs = hs + ms
        fw = (ffn_w_up[i], ffn_conv_w[i], ffn_conv_b[i], ffn_w_down[i])
        fp, cp = conv_ffn(rmsnorm(hp, norm_ffn[i]), jnp.zeros((BATCH, FFN_CONV_W - 1, D_FF), hp.dtype), *fw)
        fs, cs = conv_ffn(rmsnorm(hs, norm_ffn[i]), state_ffn_conv[i], *fw)
        ffn_p.append(cp)
        ffn_s.append(cs)
        hp = hp + fp
        hs = hs + fs
    return (hp, hs,
            jnp.stack(gdn_p), jnp.stack(gdnc_p), jnp.stack(kv_p), jnp.stack(win_p), jnp.stack(ffn_p),
            jnp.stack(gdn_s), jnp.stack(gdnc_s), jnp.stack(kv_s), jnp.stack(win_s), jnp.stack(ffn_s))
```

```python
import functools
import math

import jax
import jax.numpy as jnp
from jax import lax
from jax.experimental import pallas as pl
from jax.experimental.pallas import tpu as pltpu

F32 = jnp.float32
BF16 = jnp.bfloat16
RMS_EPS = 1e-6
NEG_INF = -1e30

LANE = 128
VMEM_LIMIT = 56 * 1024 * 1024

D_MODEL = 2048
GDN_K_HEADS = 16
GDN_V_HEADS = 32
GDN_DK = 128
GDN_DV = 128
GDN_KDIM = GDN_K_HEADS * GDN_DK
GDN_VDIM = GDN_V_HEADS * GDN_DV
GDN_CONV_DIM = 2 * GDN_KDIM + GDN_VDIM
GDN_CHUNK = 64

NSA_HEADS = 16
NSA_KV_HEADS = 4
NSA_HD = 128
NSA_GROUP = NSA_HEADS // NSA_KV_HEADS
NSA_QDIM = NSA_HEADS * NSA_HD
NSA_KVDIM = NSA_KV_HEADS * NSA_HD
CMP_BLOCK = 32
CMP_STRIDE = 16
CMP_HIDDEN = 2 * NSA_HD
SEL_BLOCK = 64
N_SEL = 16
WINDOW = 512
FORCE_BONUS = 1e3
SEL_QBLOCK = 64
WIN_QBLOCK = 128
N_BUCKETS = 32
REL_MAX_DIST = 1024
PAGE_SIZE = 128


def _proj_kernel(*refs, has_norm, has_res, has_post):
    it = iter(refs)
    x_ref = next(it)
    nw_ref = next(it) if has_norm else None
    w_ref = next(it)
    res_ref = next(it) if has_res else None
    pw_ref = next(it) if has_post else None
    pm_ref = next(it) if has_post else None
    o_ref = next(it)
    xs_ref = next(it)

    @pl.when(pl.program_id(1) == 0)
    def _():
        x = x_ref[...].astype(F32)
        if has_norm:
            ms = jnp.mean(x * x, axis=-1, keepdims=True)
            x = x * lax.rsqrt(ms + RMS_EPS) * nw_ref[...]
        xs_ref[...] = x.astype(BF16)

    y = jnp.dot(xs_ref[...], w_ref[...], preferred_element_type=F32)
    if has_res:
        y = y + res_ref[...]
    if has_post:
        tn = y.shape[1]
        for g in range(tn // LANE):
            sl = slice(g * LANE, (g + 1) * LANE)
            yg = y[:, sl]
            mode = pm_ref[:, sl]
            ms = jnp.mean(yg * yg, axis=-1, keepdims=True)
            normed = yg * lax.rsqrt(ms + RMS_EPS) * pw_ref[:, sl]
            sig = jax.nn.sigmoid(yg)
            o_ref[:, sl] = jnp.where(mode == 1.0, normed, jnp.where(mode == 2.0, sig, yg))
    else:
        o_ref[...] = y.astype(o_ref.dtype)


def _pick_tile(n, candidates):
    for c in candidates:
        if n % c == 0:
            return c
    return n


def proj(x, w, *, norm_w=None, res=None, post=None, out_dtype=F32, name="proj"):
    m, k = x.shape
    n = w.shape[1]
    tm = _pick_tile(m, (512, 256))
    tn = _pick_tile(n, (512, 384, 256, 128))
    in_specs = [pl.BlockSpec((tm, k), lambda i, j: (i, 0))]
    args = [x]
    if norm_w is not None:
        in_specs.append(pl.BlockSpec((1, k), lambda i, j: (0, 0)))
        args.append(norm_w.reshape(1, k).astype(F32))
    in_specs.append(pl.BlockSpec((k, tn), lambda i, j: (0, j)))
    args.append(w)
    if res is not None:
        in_specs.append(pl.BlockSpec((tm, tn), lambda i, j: (i, j)))
        args.append(res)
    if post is not None:
        for a in post:
            in_specs.append(pl.BlockSpec((1, tn), lambda i, j: (0, j)))
            args.append(a.reshape(1, n).astype(F32))
    kern = functools.partial(_proj_kernel, has_norm=norm_w is not None, has_res=res is not None,
                             has_post=post is not None)
    return pl.pallas_call(
        kern,
        grid=(m // tm, n // tn),
        in_specs=in_specs,
        out_specs=pl.BlockSpec((tm, tn), lambda i, j: (i, j)),
        out_shape=jax.ShapeDtypeStruct((m, n), out_dtype),
        scratch_shapes=[pltpu.VMEM((tm, k), BF16)],
        compiler_params=pltpu.CompilerParams(dimension_semantics=("parallel", "arbitrary"),
                                             vmem_limit_bytes=VMEM_LIMIT),
        name=name,
    )(*args)


def _ffn_gate_kernel(gate_ref, val_ref, pre_ref, cw_ref, cb_ref, o_ref, pad_ref, *, width):
    t = gate_ref.shape[0]
    halo = 8
    pad_ref[0:halo, :] = jnp.zeros((halo, pad_ref.shape[1]), F32)
    pad_ref[halo - (width - 1):halo, :] = pre_ref[...]
    pad_ref[halo:halo + t, :] = gate_ref[...]
    acc = cb_ref[...] + jnp.zeros((t, gate_ref.shape[1]), F32)
    for i in range(width):
        off = halo - (width - 1) + i
        acc = acc + pad_ref[off:off + t, :] * cw_ref[i:i + 1, :]
    o_ref[...] = (jax.nn.silu(acc) * val_ref[...]).astype(o_ref.dtype)


def ffn_gate(up, prefix, conv_w, conv_b, d_ff):
    b, t, _ = up.shape
    width = conv_w.shape[1]
    nblk = d_ff // LANE
    kern = functools.partial(_ffn_gate_kernel, width=width)
    return pl.pallas_call(
        kern,
        grid=(b, nblk),
        in_specs=[
            pl.BlockSpec((None, t, LANE), lambda i, j: (i, 0, j)),
            pl.BlockSpec((None, t, LANE), lambda i, j: (i, 0, j + nblk)),
            pl.BlockSpec((None, width - 1, LANE), lambda i, j: (i, 0, j)),
            pl.BlockSpec((width, LANE), lambda i, j: (0, j)),
            pl.BlockSpec((1, LANE), lambda i, j: (0, j)),
        ],
        out_specs=pl.BlockSpec((None, t, LANE), lambda i, j: (i, 0, j)),
        out_shape=jax.ShapeDtypeStruct((b, t, d_ff), BF16),
        scratch_shapes=[pltpu.VMEM((t + 8, LANE), F32)],
        compiler_params=pltpu.CompilerParams(dimension_semantics=("parallel", "parallel")),
        name="ffn_gate",
    )(up, up, prefix, conv_w.T, conv_b.reshape(1, d_ff))


def conv_ffn(h2d, b, t, prefix, norm_w, w_up, conv_w, conv_b, w_down):
    d_ff = conv_w.shape[0]
    up = proj(h2d, w_up, norm_w=norm_w, name="ffn_up").reshape(b, t, 2 * d_ff)
    hidden = ffn_gate(up, prefix, conv_w, conv_b, d_ff)
    keep = conv_w.shape[1] - 1
    assert t >= keep
    new_prefix = up[:, t - keep:, :d_ff]
    out = proj(hidden.reshape(b * t, d_ff), w_down, res=h2d, name="ffn_down")
    return out, new_prefix


def rmsnorm(x, w):
    xf = x.astype(F32)
    y = xf * lax.rsqrt(jnp.mean(xf * xf, axis=-1, keepdims=True) + RMS_EPS)
    return (y * w.astype(F32)).astype(x.dtype)


def l2norm(x):
    xf = x.astype(F32)
    return xf * lax.rsqrt(jnp.sum(xf * xf, axis=-1, keepdims=True) + RMS_EPS)


def causal_dwconv(x, prefix, w, b=None):
    width = w.shape[-1]
    t = x.shape[1]
    xp = jnp.concatenate([prefix.astype(x.dtype), x], axis=1)
    y = xp[:, 0:t] * w[:, 0]
    for i in range(1, width):
        y = y + xp[:, i:i + t] * w[:, i]
    if b is not None:
        y = y + b
    return y, xp[:, t:]


def t5_bucket(dist):
    d = jnp.maximum(dist, 0)
    max_exact = N_BUCKETS // 2
    scale = (N_BUCKETS - max_exact) / math.log(REL_MAX_DIST / max_exact)
    large = max_exact + (jnp.log(jnp.maximum(d, 1).astype(F32) / max_exact) * scale).astype(jnp.int32)
    return jnp.where(d < max_exact, d, jnp.minimum(large, N_BUCKETS - 1))


def gated_delta_rule(q, k, v, g, beta, s0):
    b, t, h, _ = q.shape
    dv = v.shape[-1]
    L = math.gcd(t, GDN_CHUNK)
    nc = t // L

    def chunks(a):
        a = a.reshape((b, nc, L, h) + a.shape[3:])
        return jnp.moveaxis(a, (1, 3), (0, 2))

    qc, kc, vc, gc, bc = chunks(q), chunks(k), chunks(v), chunks(g), chunks(beta)
    G = jnp.cumsum(gc, axis=-1)
    diff = G[..., :, None] - G[..., None, :]
    incl = jnp.tril(jnp.ones((L, L), bool))
    strict = jnp.tril(jnp.ones((L, L), bool), -1)
    dec_incl = jnp.exp(jnp.where(incl, diff, -jnp.inf))
    dec_strict = jnp.exp(jnp.where(strict, diff, -jnp.inf))
    kk = jnp.einsum('cbhid,cbhjd->cbhij', kc, kc)
    a_mat = jnp.eye(L, dtype=F32) + bc[..., :, None] * kk * dec_strict
    u_eff = lax.linalg.triangular_solve(a_mat, bc[..., None] * vc, left_side=True, lower=True, unit_diagonal=True)
    w_k = lax.linalg.triangular_solve(a_mat, (bc * jnp.exp(G))[..., None] * kc, left_side=True, lower=True,
                                      unit_diagonal=True)
    a_qk = jnp.einsum('cbhid,cbhjd->cbhij', qc, kc) * dec_incl
    q_dec = qc * jnp.exp(G)[..., None]
    g_last = G[..., -1:]
    k_dec = kc * jnp.exp(g_last - G)[..., None]

    def step(s, xs):
        u_c, w_c, aqk_c, q_c, k_c, gl_c = xs
        u = u_c - jnp.einsum('bhik,bhkv->bhiv', w_c, s)
        o = jnp.einsum('bhik,bhkv->bhiv', q_c, s) + jnp.einsum('bhij,bhjv->bhiv', aqk_c, u)
        s = jnp.exp(gl_c)[..., None] * s + jnp.einsum('bhik,bhiv->bhkv', k_c, u)
        return s, o

    s_fin, o = lax.scan(step, s0, (u_eff, w_k, a_qk, q_dec, k_dec, g_last))
    o = jnp.moveaxis(o, (0, 2), (1, 3)).reshape(b, t, h, dv)
    return o, s_fin


def gdn_mixer(h2d, b, t, s0, conv_prefix, norm_in, w_in, conv_w, a_log, dt_bias, norm_w, w_out):
    w_main, w_gate = w_in
    proj_out = proj(h2d, w_main, norm_w=norm_in, name="gdn_in").reshape(b, t, -1)
    gate_out = proj(h2d, w_gate, norm_w=norm_in, name="gdn_in_gate").reshape(b, t, -1)
    qkv, z = proj_out[..., :GDN_CONV_DIM], proj_out[..., GDN_CONV_DIM:]
    bl, al = gate_out[..., :GDN_V_HEADS], gate_out[..., GDN_V_HEADS:2 * GDN_V_HEADS]
    qkv, new_prefix = causal_dwconv(qkv, conv_prefix, conv_w)
    qkv = jax.nn.silu(qkv)
    q = qkv[..., :GDN_KDIM].reshape(b, t, GDN_K_HEADS, GDN_DK)
    k = qkv[..., GDN_KDIM:2 * GDN_KDIM].reshape(b, t, GDN_K_HEADS, GDN_DK)
    v = qkv[..., 2 * GDN_KDIM:].reshape(b, t, GDN_V_HEADS, GDN_DV).astype(F32)
    rep = GDN_V_HEADS // GDN_K_HEADS
    q = jnp.repeat(l2norm(q), rep, axis=2) * (GDN_DK ** -0.5)
    k = jnp.repeat(l2norm(k), rep, axis=2)
    beta = jax.nn.sigmoid(bl.astype(F32))
    g = -jnp.exp(a_log.astype(F32)) * jax.nn.softplus(al.astype(F32) + dt_bias.astype(F32))
    o, s_new = gated_delta_rule(q, k, v, g, beta, s0.astype(F32))
    o = rmsnorm(o, norm_w) * jax.nn.silu(z.reshape(b, t, GDN_V_HEADS, GDN_DV).astype(F32))
    y = proj(o.reshape(b * t, GDN_VDIM), w_out, res=h2d, name="gdn_out")
    return y, s_new.astype(s0.dtype), new_prefix


def nsa_compress(rows, pe, w1, w2):
    b, n = rows.shape[:2]
    ns = n // CMP_STRIDE
    nsub = CMP_BLOCK // CMP_STRIDE
    nc = ns - nsub + 1
    r = rows[:, :ns * CMP_STRIDE].reshape(b, ns, CMP_STRIDE, NSA_KV_HEADS, NSA_HD)
    w1r = w1.reshape(nsub, CMP_STRIDE, NSA_HD, CMP_HIDDEN)
    hid = pe.reshape(-1) @ w1
    for i in range(nsub):
        hid = hid + jnp.einsum('bnsgd,sde->bnge', r[:, i:i + nc], w1r[i])
    return jax.nn.gelu(hid) @ w2


def nsa_mixer(h2d, b, t, past, win_prefix, win_keep, norm_in, w_in_pad, post, k_norm, cmp_pe, cmp_w1, cmp_w2,
              rel_bias, w_out):
    p_len = past.shape[1]
    n_all = p_len + t
    w_main, w_gate = w_in_pad
    post_main, post_gate = post
    proj_out = proj(h2d, w_main, norm_w=norm_in, post=post_main, name="nsa_in").reshape(b, t, -1)
    gate_out = proj(h2d, w_gate, norm_w=norm_in, post=post_gate, name="nsa_in_gate").reshape(b, t, -1)
    q = proj_out[..., :NSA_QDIM].reshape(b, t, NSA_HEADS, NSA_HD)
    kv = proj_out[..., NSA_QDIM:NSA_QDIM + 6 * NSA_KVDIM].reshape(b, t, 6, NSA_KV_HEADS, NSA_HD)
    gates = gate_out[..., :3 * NSA_HEADS].reshape(b, t, 3, NSA_HEADS)
    qg = q.reshape(b, t, NSA_KV_HEADS, NSA_GROUP, NSA_HD)
    new_rows = kv[:, :, 0:4]
    new_win = kv[:, :, 4:6]
    rows = jnp.concatenate([past, new_rows], axis=1)
    qpos = p_len + jnp.arange(t)
    rel_h = rel_bias.astype(F32)

    kc = rmsnorm(nsa_compress(rows[:, :, 0], cmp_pe[0], cmp_w1[0], cmp_w2[0]), k_norm[0])
    vc = nsa_compress(rows[:, :, 1], cmp_pe[1], cmp_w1[1], cmp_w2[1])
    nc = kc.shape[1]
    c_end = jnp.arange(nc) * CMP_STRIDE + (CMP_BLOCK - 1)
    c_dist = qpos[:, None] - c_end[None, :]
    c_ok = (c_dist >= 0)[:, None, None, :]
    c_bias = rel_h[t5_bucket(c_dist)].reshape(t, nc, NSA_KV_HEADS, NSA_GROUP).transpose(0, 2, 3, 1)
    s = jnp.einsum('btgjd,bngd->btgjn', qg, kc, preferred_element_type=F32) + c_bias
    p_cmp = jax.nn.softmax(jnp.where(c_ok, s, NEG_INF), axis=-1) * c_ok
    o_cmp = jnp.einsum('btgjn,bngd->btgjd', p_cmp.astype(vc.dtype), vc)

    n_sb = -(-n_all // SEL_BLOCK)
    sb_start = jnp.arange(n_sb) * SEL_BLOCK
    c_start = c_end - (CMP_BLOCK - 1)
    ovl = jnp.maximum(jnp.minimum(c_end[:, None], sb_start[None, :] + SEL_BLOCK - 1)
                      - jnp.maximum(c_start[:, None], sb_start[None, :]) + 1, 0).astype(F32) / CMP_BLOCK
    imp = jnp.einsum('btgjn,nm->btgm', p_cmp, ovl)
    cur = qpos // SEL_BLOCK
    blk = jnp.arange(n_sb)
    sb_ok = sb_start[None, :] <= qpos[:, None]
    forced = (blk[None, :] == 0) | (blk[None, :] == cur[:, None]) | (blk[None, :] == cur[:, None] - 1)
    score = jnp.where(sb_ok[:, None, :], imp + jnp.where(forced, FORCE_BONUS, 0.0)[:, None, :], NEG_INF)
    n_pick = min(N_SEL, n_sb)
    top_v, top_i = lax.top_k(score, n_pick)
    top_ok = top_v > 0.5 * NEG_INF

    pad = n_sb * SEL_BLOCK - n_all
    ks = jnp.pad(rows[:, :, 2], ((0, 0), (0, pad), (0, 0), (0, 0))).reshape(
        b, n_sb, SEL_BLOCK, NSA_KV_HEADS, NSA_HD).transpose(0, 3, 1, 2, 4)
    vs = jnp.pad(rows[:, :, 3], ((0, 0), (0, pad), (0, 0), (0, 0))).reshape(
        b, n_sb, SEL_BLOCK, NSA_KV_HEADS, NSA_HD).transpose(0, 3, 1, 2, 4)
    rel_g = rel_h.reshape(N_BUCKETS, NSA_KV_HEADS, NSA_GROUP).transpose(1, 0, 2)
    b_ix = jnp.arange(b)[:, None, None, None]
    g_ix = jnp.arange(NSA_KV_HEADS)[None, None, :, None]
    qb_len = math.gcd(t, SEL_QBLOCK)
    nqb = t // qb_len

    def sel_attend(args):
        q_b, pos_b, idx_b, ok_b = args
        kg = ks[b_ix, g_ix, idx_b]
        vg = vs[b_ix, g_ix, idx_b]
        kpos = idx_b[..., None] * SEL_BLOCK + jnp.arange(SEL_BLOCK)
        dist = pos_b[None, :, None, None, None] - kpos
        mask = (ok_b[..., None] & (dist >= 0))[:, :, :, None]
        bias = jnp.moveaxis(rel_g[g_ix[..., None], t5_bucket(dist)], -1, 3)
        sc = jnp.einsum('bqgjd,bqgnkd->bqgjnk', q_b, kg, preferred_element_type=F32) + bias
        sc = jnp.where(mask, sc, NEG_INF)
        p = jax.nn.softmax(sc.reshape(sc.shape[:4] + (-1,)), axis=-1).reshape(sc.shape)
        return jnp.einsum('bqgjnk,bqgnkd->bqgjd', p.astype(vg.dtype), vg)

    def to_blocks(a):
        return jnp.moveaxis(a.reshape((a.shape[0], nqb, qb_len) + a.shape[2:]), 1, 0)

    o_sel = lax.map(sel_attend, (to_blocks(qg), qpos.reshape(nqb, qb_len), to_blocks(top_i), to_blocks(top_ok)))
    o_sel = jnp.moveaxis(o_sel, 0, 1).reshape(b, t, NSA_HEADS, NSA_HD)

    rows_w = jnp.concatenate([win_prefix, new_win], axis=1)
    wpos = p_len - WINDOW + jnp.arange(WINDOW + t)
    wb = math.gcd(t, WIN_QBLOCK)
    nwb = t // wb
    w_idx = jnp.arange(nwb)[:, None] * wb + jnp.arange(WINDOW + wb)[None, :]
    slab = rows_w[:, w_idx]
    kpos = wpos[w_idx]
    w_dist = qpos.reshape(nwb, wb)[:, :, None] - kpos[:, None, :]
    w_ok = (w_dist >= 0) & (w_dist < WINDOW) & (kpos[:, None, :] >= 0)
    w_bias = rel_h[t5_bucket(w_dist)].reshape(nwb, wb, WINDOW + wb, NSA_KV_HEADS, NSA_GROUP).transpose(0, 3, 4, 1, 2)
    qw = qg.reshape(b, nwb, wb, NSA_KV_HEADS, NSA_GROUP, NSA_HD)
    sw = jnp.einsum('bnqgjd,bnkgd->bngjqk', qw, slab[:, :, :, 0], preferred_element_type=F32) + w_bias
    pw = jax.nn.softmax(jnp.where(w_ok[:, None, None], sw, NEG_INF), axis=-1)
    o_win = jnp.einsum('bngjqk,bnkgd->bnqgjd', pw.astype(slab.dtype), slab[:, :, :, 1]).reshape(
        b, t, NSA_HEADS, NSA_HD)

    o = (gates[:, :, 0, :, None] * o_cmp.reshape(b, t, NSA_HEADS, NSA_HD)
         + gates[:, :, 1, :, None] * o_sel
         + gates[:, :, 2, :, None] * o_win)
    y = proj(o.reshape(b * t, NSA_QDIM), w_out, res=h2d, name="nsa_out")
    return y, new_rows, rows_w[:, rows_w.shape[1] - win_keep:]


def _pad_cols(w, n_pad):
    return jnp.pad(w, ((0, 0), (0, n_pad - w.shape[1])))


def _nsa_in_post(q_norm, k_norm):
    ones_kv = jnp.ones((NSA_KVDIM,), F32)
    zeros_kv = jnp.zeros((NSA_KVDIM,), F32)
    pw = jnp.concatenate([
        jnp.tile(q_norm.astype(F32) * (NSA_HD ** -0.5), NSA_HEADS),
        ones_kv, ones_kv, jnp.tile(k_norm[1].astype(F32), NSA_KV_HEADS), ones_kv,
        jnp.tile(k_norm[2].astype(F32), NSA_KV_HEADS), ones_kv])
    pm = jnp.concatenate([
        jnp.ones((NSA_QDIM,), F32),
        zeros_kv, zeros_kv, ones_kv, zeros_kv, ones_kv, zeros_kv])
    return (pw, pm), (jnp.ones((LANE,), F32), jnp.full((LANE,), 2.0, F32))


def kernel(x_prompt, x_sample, state_gdn, state_gdn_conv, cache_nsa_kv, state_nsa_win, state_ffn_conv, page_table,
           norm_mix, norm_ffn, gdn_w_in, gdn_conv_w, gdn_A_log, gdn_dt_bias, gdn_norm, gdn_w_out,
           nsa_w_in, nsa_q_norm, nsa_k_norm, nsa_cmp_pe, nsa_cmp_w1, nsa_cmp_w2, rel_bias, nsa_w_out,
           ffn_w_up, ffn_conv_w, ffn_conv_b, ffn_w_down):
    depth = norm_mix.shape[0]
    bp, tp, d = x_prompt.shape
    bs, ts, _ = x_sample.shape
    n_pages = page_table.shape[1]
    win_buf = state_nsa_win.shape[2]
    d_ff = ffn_conv_w.shape[1]
    conv_keep = ffn_conv_w.shape[2] - 1
    hp = x_prompt.reshape(bp * tp, d)
    hs = x_sample.reshape(bs * ts, d)
    gdn_p, gdnc_p, kv_p, win_p, ffn_p = [], [], [], [], []
    gdn_s, gdnc_s, kv_s, win_s, ffn_s = [], [], [], [], []
    for i in range(depth):
        j = i // 2
        if i % 2 == 0:
            n_main = GDN_CONV_DIM + GDN_VDIM
            w_in = (gdn_w_in[j][:, :n_main].astype(BF16), _pad_cols(gdn_w_in[j][:, n_main:], LANE).astype(BF16))
            gw = (norm_mix[i], w_in, gdn_conv_w[j], gdn_A_log[j], gdn_dt_bias[j], gdn_norm[j],
                  gdn_w_out[j].astype(BF16))
            hp, st_p, cv_p = gdn_mixer(hp, bp, tp, jnp.zeros((bp, GDN_V_HEADS, GDN_DK, GDN_DV), F32),
                                       jnp.zeros((bp, gdn_conv_w.shape[2] - 1, GDN_CONV_DIM), F32), *gw)
            hs, st_s, cv_s = gdn_mixer(hs, bs, ts, state_gdn[j], state_gdn_conv[j], *gw)
            gdn_p.append(st_p)
            gdnc_p.append(cv_p)
            gdn_s.append(st_s)
            gdnc_s.append(cv_s)
        else:
            n_main = NSA_QDIM + 6 * NSA_KVDIM
            w_in_pad = (nsa_w_in[j][:, :n_main].astype(BF16), _pad_cols(nsa_w_in[j][:, n_main:], LANE).astype(BF16))
            post = _nsa_in_post(nsa_q_norm[j], nsa_k_norm[j])
            nw = (norm_mix[i], w_in_pad, post, nsa_k_norm[j], nsa_cmp_pe[j], nsa_cmp_w1[j], nsa_cmp_w2[j], rel_bias,
                  nsa_w_out[j].astype(BF16))
            hp, rows_p, wst_p = nsa_mixer(hp, bp, tp, jnp.zeros((bp, 0, 4, NSA_KV_HEADS, NSA_HD), F32),
                                          jnp.zeros((bp, WINDOW, 2, NSA_KV_HEADS, NSA_HD), F32),
                                          min(WINDOW, tp), *nw)
            past = cache_nsa_kv[j][page_table].reshape(bs, n_pages * PAGE_SIZE, 4, NSA_KV_HEADS, NSA_HD)
            prefix = jnp.pad(state_nsa_win[j], ((0, 0), (WINDOW - win_buf, 0), (0, 0), (0, 0), (0, 0)))
            hs, rows_s, wst_s = nsa_mixer(hs, bs, ts, past, prefix, win_buf, *nw)
            kv_p.append(rows_p)
            win_p.append(wst_p)
            kv_s.append(rows_s)
            win_s.append(wst_s)
        fw = (norm_ffn[i], ffn_w_up[i].astype(BF16), ffn_conv_w[i], ffn_conv_b[i], ffn_w_down[i].astype(BF16))
        hp, cp = conv_ffn(hp, bp, tp, jnp.zeros((bp, conv_keep, d_ff), F32), *fw)
        hs, cs = conv_ffn(hs, bs, ts, state_ffn_conv[i], *fw)
        ffn_p.append(cp)
        ffn_s.append(cs)
    return (hp.reshape(bp, tp, d), hs.reshape(bs, ts, d),
            jnp.stack(gdn_p), jnp.stack(gdnc_p), jnp.stack(kv_p), jnp.stack(win_p), jnp.stack(ffn_p),
            jnp.stack(gdn_s), jnp.stack(gdnc_s), jnp.stack(kv_s), jnp.stack(win_s), jnp.stack(ffn_s))
```

```python
import functools
import math

import jax
import jax.numpy as jnp
from jax import lax
from jax.experimental import pallas as pl
from jax.experimental.pallas import tpu as pltpu

F32 = jnp.float32
BF16 = jnp.bfloat16
RMS_EPS = 1e-6
NEG_INF = -1e30

LANE = 128
SUBLANE = 8
VMEM_LIMIT = 56 * 1024 * 1024

D_MODEL = 2048
GDN_K_HEADS = 16
GDN_V_HEADS = 32
GDN_DK = 128
GDN_DV = 128
GDN_KDIM = GDN_K_HEADS * GDN_DK
GDN_VDIM = GDN_V_HEADS * GDN_DV
GDN_CONV_DIM = 2 * GDN_KDIM + GDN_VDIM
GDN_CHUNK = 64

NSA_HEADS = 16
NSA_KV_HEADS = 4
NSA_HD = 128
NSA_GROUP = NSA_HEADS // NSA_KV_HEADS
NSA_QDIM = NSA_HEADS * NSA_HD
NSA_KVDIM = NSA_KV_HEADS * NSA_HD
CMP_BLOCK = 32
CMP_STRIDE = 16
CMP_HIDDEN = 2 * NSA_HD
SEL_BLOCK = 64
SEL_SHIFT = 6
N_SEL = 16
WINDOW = 512
FORCE_BONUS = 1e3
N_BUCKETS = 32
REL_MAX_DIST = 1024
PAGE_SIZE = 128
T5_SATURATION = 790

NT_DIMS = (((1,), (1,)), ((), ()))


def _proj_kernel(*refs, n_x, has_norm, has_res, has_post):
    it = iter(refs)
    x_refs = [next(it) for _ in range(n_x)]
    nw_ref = next(it) if has_norm else None
    w_ref = next(it)
    res_ref = next(it) if has_res else None
    pw_ref = next(it) if has_post else None
    pm_ref = next(it) if has_post else None
    o_ref = next(it)
    xs_ref = next(it)

    @pl.when(pl.program_id(1) == 0)
    def _():
        x = x_refs[0][...].astype(F32)
        for r in x_refs[1:]:
            x = x + r[...].astype(F32)
        if has_norm:
            ms = jnp.mean(x * x, axis=-1, keepdims=True)
            x = x * lax.rsqrt(ms + RMS_EPS) * nw_ref[...]
        xs_ref[...] = x.astype(BF16)

    y = jnp.dot(xs_ref[...], w_ref[...], preferred_element_type=F32)
    if has_res:
        y = y + res_ref[...]
    if has_post:
        tn = y.shape[1]
        for g in range(tn // LANE):
            sl = slice(g * LANE, (g + 1) * LANE)
            yg = y[:, sl]
            mode = pm_ref[:, sl]
            ms = jnp.mean(yg * yg, axis=-1, keepdims=True)
            normed = yg * lax.rsqrt(ms + RMS_EPS) * pw_ref[:, sl]
            sig = jax.nn.sigmoid(yg)
            o_ref[:, sl] = jnp.where(mode == 1.0, normed, jnp.where(mode == 2.0, sig, yg))
    else:
        o_ref[...] = y.astype(o_ref.dtype)


def _pick_tile(n, candidates):
    for c in candidates:
        if n % c == 0:
            return c
    return n


def proj(xs, w, *, norm_w=None, res=None, post=None, out_dtype=F32, name="proj"):
    if not isinstance(xs, (tuple, list)):
        xs = (xs,)
    m, k = xs[0].shape
    n = w.shape[1]
    tm = _pick_tile(m, (512, 256))
    tn = _pick_tile(n, (512, 384, 256, 128))
    in_specs = [pl.BlockSpec((tm, k), lambda i, j: (i, 0)) for _ in xs]
    args = list(xs)
    if norm_w is not None:
        in_specs.append(pl.BlockSpec((1, k), lambda i, j: (0, 0)))
        args.append(norm_w.reshape(1, k).astype(F32))
    in_specs.append(pl.BlockSpec((k, tn), lambda i, j: (0, j)))
    args.append(w)
    if res is not None:
        in_specs.append(pl.BlockSpec((tm, tn), lambda i, j: (i, j)))
        args.append(res)
    if post is not None:
        for a in post:
            in_specs.append(pl.BlockSpec((1, tn), lambda i, j: (0, j)))
            args.append(a.reshape(1, n).astype(F32))
    kern = functools.partial(_proj_kernel, n_x=len(xs), has_norm=norm_w is not None, has_res=res is not None,
                             has_post=post is not None)
    return pl.pallas_call(
        kern,
        grid=(m // tm, n // tn),
        in_specs=in_specs,
        out_specs=pl.BlockSpec((tm, tn), lambda i, j: (i, j)),
        out_shape=jax.ShapeDtypeStruct((m, n), out_dtype),
        scratch_shapes=[pltpu.VMEM((tm, k), BF16)],
        compiler_params=pltpu.CompilerParams(dimension_semantics=("parallel", "arbitrary"),
                                             vmem_limit_bytes=VMEM_LIMIT),
        name=name,
    )(*args)


def _ffn_gate_kernel(gate_ref, val_ref, pre_ref, cw_ref, cb_ref, o_ref, pad_ref, *, width):
    t = gate_ref.shape[0]
    halo = SUBLANE
    pad_ref[0:halo, :] = jnp.zeros((halo, pad_ref.shape[1]), F32)
    pad_ref[halo - (width - 1):halo, :] = pre_ref[...]
    pad_ref[halo:halo + t, :] = gate_ref[...]
    acc = cb_ref[...] + jnp.zeros((t, gate_ref.shape[1]), F32)
    for i in range(width):
        off = halo - (width - 1) + i
        acc = acc + pad_ref[off:off + t, :] * cw_ref[i:i + 1, :]
    o_ref[...] = (jax.nn.silu(acc) * val_ref[...]).astype(o_ref.dtype)


def ffn_gate(up, prefix, conv_w, conv_b, d_ff):
    b, t, _ = up.shape
    width = conv_w.shape[1]
    nblk = d_ff // LANE
    kern = functools.partial(_ffn_gate_kernel, width=width)
    return pl.pallas_call(
        kern,
        grid=(b, nblk),
        in_specs=[
            pl.BlockSpec((None, t, LANE), lambda i, j: (i, 0, j)),
            pl.BlockSpec((None, t, LANE), lambda i, j: (i, 0, j + nblk)),
            pl.BlockSpec((None, width - 1, LANE), lambda i, j: (i, 0, j)),
            pl.BlockSpec((width, LANE), lambda i, j: (0, j)),
            pl.BlockSpec((1, LANE), lambda i, j: (0, j)),
        ],
        out_specs=pl.BlockSpec((None, t, LANE), lambda i, j: (i, 0, j)),
        out_shape=jax.ShapeDtypeStruct((b, t, d_ff), BF16),
        scratch_shapes=[pltpu.VMEM((t + SUBLANE, LANE), F32)],
        compiler_params=pltpu.CompilerParams(dimension_semantics=("parallel", "parallel")),
        name="ffn_gate",
    )(up, up, prefix, conv_w.T, conv_b.reshape(1, d_ff))


def conv_ffn(h2d, b, t, prefix, norm_w, w_up, conv_w, conv_b, w_down):
    d_ff = conv_w.shape[0]
    up = proj(h2d, w_up, norm_w=norm_w, name="ffn_up").reshape(b, t, 2 * d_ff)
    hidden = ffn_gate(up, prefix, conv_w, conv_b, d_ff)
    keep = conv_w.shape[1] - 1
    assert t >= keep
    new_prefix = up[:, t - keep:, :d_ff]
    out = proj(hidden.reshape(b * t, d_ff), w_down, res=h2d, name="ffn_down")
    return out, new_prefix


def rmsnorm(x, w):
    xf = x.astype(F32)
    y = xf * lax.rsqrt(jnp.mean(xf * xf, axis=-1, keepdims=True) + RMS_EPS)
    return (y * w.astype(F32)).astype(x.dtype)


def l2norm(x):
    xf = x.astype(F32)
    return xf * lax.rsqrt(jnp.sum(xf * xf, axis=-1, keepdims=True) + RMS_EPS)


def causal_dwconv(x, prefix, w, b=None):
    width = w.shape[-1]
    t = x.shape[1]
    xp = jnp.concatenate([prefix.astype(x.dtype), x], axis=1)
    y = xp[:, 0:t] * w[:, 0]
    for i in range(1, width):
        y = y + xp[:, i:i + t] * w[:, i]
    if b is not None:
        y = y + b
    return y, xp[:, t:]


def t5_bucket(dist):
    d = jnp.maximum(dist, 0)
    max_exact = N_BUCKETS // 2
    scale = (N_BUCKETS - max_exact) / math.log(REL_MAX_DIST / max_exact)
    large = max_exact + (jnp.log(jnp.maximum(d, 1).astype(F32) / max_exact) * scale).astype(jnp.int32)
    return jnp.where(d < max_exact, d, jnp.minimum(large, N_BUCKETS - 1))


def gated_delta_rule(q, k, v, g, beta, s0):
    b, t, h, _ = q.shape
    dv = v.shape[-1]
    L = math.gcd(t, GDN_CHUNK)
    nc = t // L

    def chunks(a):
        a = a.reshape((b, nc, L, h) + a.shape[3:])
        return jnp.moveaxis(a, (1, 3), (0, 2))

    qc, kc, vc, gc, bc = chunks(q), chunks(k), chunks(v), chunks(g), chunks(beta)
    G = jnp.cumsum(gc, axis=-1)
    diff = G[..., :, None] - G[..., None, :]
    incl = jnp.tril(jnp.ones((L, L), bool))
    strict = jnp.tril(jnp.ones((L, L), bool), -1)
    dec_incl = jnp.exp(jnp.where(incl, diff, -jnp.inf))
    dec_strict = jnp.exp(jnp.where(strict, diff, -jnp.inf))
    kk = jnp.einsum('cbhid,cbhjd->cbhij', kc, kc)
    a_mat = jnp.eye(L, dtype=F32) + bc[..., :, None] * kk * dec_strict
    u_eff = lax.linalg.triangular_solve(a_mat, bc[..., None] * vc, left_side=True, lower=True, unit_diagonal=True)
    w_k = lax.linalg.triangular_solve(a_mat, (bc * jnp.exp(G))[..., None] * kc, left_side=True, lower=True,
                                      unit_diagonal=True)
    a_qk = jnp.einsum('cbhid,cbhjd->cbhij', qc, kc) * dec_incl
    q_dec = qc * jnp.exp(G)[..., None]
    g_last = G[..., -1:]
    k_dec = kc * jnp.exp(g_last - G)[..., None]

    def step(s, xs):
        u_c, w_c, aqk_c, q_c, k_c, gl_c = xs
        u = u_c - jnp.einsum('bhik,bhkv->bhiv', w_c, s)
        o = jnp.einsum('bhik,bhkv->bhiv', q_c, s) + jnp.einsum('bhij,bhjv->bhiv', aqk_c, u)
        s = jnp.exp(gl_c)[..., None] * s + jnp.einsum('bhik,bhiv->bhkv', k_c, u)
        return s, o

    s_fin, o = lax.scan(step, s0, (u_eff, w_k, a_qk, q_dec, k_dec, g_last))
    o = jnp.moveaxis(o, (0, 2), (1, 3)).reshape(b, t, h, dv)
    return o, s_fin


def gdn_mixer(h2d, b, t, s0, conv_prefix, norm_in, w_in, conv_w, a_log, dt_bias, norm_w, w_out):
    w_main, w_gate = w_in
    proj_out = proj(h2d, w_main, norm_w=norm_in, name="gdn_in").reshape(b, t, -1)
    gate_out = proj(h2d, w_gate, norm_w=norm_in, name="gdn_in_gate").reshape(b, t, -1)
    qkv, z = proj_out[..., :GDN_CONV_DIM], proj_out[..., GDN_CONV_DIM:]
    bl, al = gate_out[..., :GDN_V_HEADS], gate_out[..., GDN_V_HEADS:2 * GDN_V_HEADS]
    qkv, new_prefix = causal_dwconv(qkv, conv_prefix, conv_w)
    qkv = jax.nn.silu(qkv)
    q = qkv[..., :GDN_KDIM].reshape(b, t, GDN_K_HEADS, GDN_DK)
    k = qkv[..., GDN_KDIM:2 * GDN_KDIM].reshape(b, t, GDN_K_HEADS, GDN_DK)
    v = qkv[..., 2 * GDN_KDIM:].reshape(b, t, GDN_V_HEADS, GDN_DV).astype(F32)
    rep = GDN_V_HEADS // GDN_K_HEADS
    q = jnp.repeat(l2norm(q), rep, axis=2) * (GDN_DK ** -0.5)
    k = jnp.repeat(l2norm(k), rep, axis=2)
    beta = jax.nn.sigmoid(bl.astype(F32))
    g = -jnp.exp(a_log.astype(F32)) * jax.nn.softplus(al.astype(F32) + dt_bias.astype(F32))
    o, s_new = gated_delta_rule(q, k, v, g, beta, s0.astype(F32))
    o = rmsnorm(o, norm_w) * jax.nn.silu(z.reshape(b, t, GDN_V_HEADS, GDN_DV).astype(F32))
    y = proj(o.reshape(b * t, GDN_VDIM), w_out, res=h2d, name="gdn_out")
    return y, s_new.astype(s0.dtype), new_prefix


def _compress_kernel(*refs, paged, n_pages, sp):
    if paged:
        refs = refs[1:]
    x_refs = refs[:NSA_KV_HEADS]
    w1_ref, pe_ref, w2_ref, nw_ref, o_ref, xs_ref, acc_ref = refs[NSA_KV_HEADS:]
    p = pl.program_id(2)
    s_all = n_pages * sp
    row0 = pl.multiple_of(p * sp, SUBLANE)
    for r in range(CMP_STRIDE):
        for g in range(NSA_KV_HEADS):
            xs_ref[r, g, pl.ds(row0, sp), :] = x_refs[g][pl.ds(r, sp, stride=CMP_STRIDE), :]

    @pl.when(p == n_pages - 1)
    def _():
        is_k = pl.program_id(1) == 0
        acc_ref[s_all:s_all + SUBLANE, :] = jnp.zeros((SUBLANE, 2 * CMP_HIDDEN), F32)
        for g in range(NSA_KV_HEADS):
            acc_ref[0:s_all, :] = jnp.zeros((s_all, 2 * CMP_HIDDEN), F32)

            def body(r, carry):
                acc_ref[0:s_all, :] += jnp.dot(xs_ref[r, g].astype(BF16), w1_ref[r], preferred_element_type=F32)
                return carry

            lax.fori_loop(0, CMP_STRIDE, body, 0)
            hid = acc_ref[0:s_all, 0:CMP_HIDDEN] + acc_ref[1:s_all + 1, CMP_HIDDEN:2 * CMP_HIDDEN] + pe_ref[...]
            act = jax.nn.gelu(hid)
            y = jnp.dot(act.astype(BF16), w2_ref[...], preferred_element_type=F32)
            ms = jnp.mean(y * y, axis=-1, keepdims=True)
            normed = y * lax.rsqrt(ms + RMS_EPS) * nw_ref[...]
            o_ref[g] = jnp.where(is_k, normed, y)


def nsa_compress(x, page_table, col0, w1r, pe_hid, w2, k_norm0):
    paged = page_table is not None
    rows = x.shape[1]
    sp = rows // CMP_STRIDE
    def x_spec(g):
        def col(c):
            return (col0 + c) * NSA_KV_HEADS + g
        if paged:
            return pl.BlockSpec((None, rows, NSA_HD), lambda i, c, p, pt: (pt[i, p], 0, col(c)))
        return pl.BlockSpec((None, rows, NSA_HD), lambda i, c, p: (i, 0, col(c)))

    if paged:
        b, n_pages = page_table.shape
    else:
        b, n_pages = x.shape[0], 1
    s_all = n_pages * sp

    def fixed(shape, imap):
        if paged:
            return pl.BlockSpec(shape, lambda i, c, p, pt: imap(i, c, p))
        return pl.BlockSpec(shape, imap)

    in_specs = [x_spec(g) for g in range(NSA_KV_HEADS)] + [
        fixed((None, CMP_STRIDE, NSA_HD, 2 * CMP_HIDDEN), lambda i, c, p: (c, 0, 0, 0)),
        fixed((None, 1, CMP_HIDDEN), lambda i, c, p: (c, 0, 0)),
        fixed((None, CMP_HIDDEN, NSA_HD), lambda i, c, p: (c, 0, 0)),
        fixed((1, NSA_HD), lambda i, c, p: (0, 0)),
    ]
    out_spec = fixed((None, None, NSA_KV_HEADS, s_all, NSA_HD), lambda i, c, p: (i, c, 0, 0, 0))
    scratch = [pltpu.VMEM((CMP_STRIDE, NSA_KV_HEADS, s_all, NSA_HD), F32),
               pltpu.VMEM((s_all + SUBLANE, 2 * CMP_HIDDEN), F32)]
    kern = functools.partial(_compress_kernel, paged=paged, n_pages=n_pages, sp=sp)
    out_shape = jax.ShapeDtypeStruct((b, 2, NSA_KV_HEADS, s_all, NSA_HD), F32)
    params = pltpu.CompilerParams(dimension_semantics=("parallel", "arbitrary", "arbitrary"),
                                  vmem_limit_bytes=VMEM_LIMIT)
    args = (x,) * NSA_KV_HEADS + (w1r, pe_hid, w2, k_norm0.reshape(1, NSA_HD).astype(F32))
    if paged:
        grid_spec = pltpu.PrefetchScalarGridSpec(num_scalar_prefetch=1, grid=(b, 2, n_pages), in_specs=in_specs,
                                                 out_specs=out_spec, scratch_shapes=scratch)
        return pl.pallas_call(kern, grid_spec=grid_spec, out_shape=out_shape, compiler_params=params,
                              name="nsa_compress_paged")(page_table, *args)
    return pl.pallas_call(kern, grid=(b, 2, n_pages), in_specs=in_specs, out_specs=out_spec, out_shape=out_shape,
                          scratch_shapes=scratch, compiler_params=params, name="nsa_compress")(*args)


def _stack_heads(qb):
    return jnp.concatenate([qb[:, j * LANE:(j + 1) * LANE] for j in range(NSA_GROUP)], axis=0)


def _cmp_attn_kernel(q_ref, kc_ref, vc_ref, cb_ref, gate_ref, ovl_ref, o_ref, sel_ref, *, tq, q0, n_sb):
    i = pl.program_id(2)
    s_all = kc_ref.shape[0]
    nsbp = sel_ref.shape[-1]
    q4 = _stack_heads(q_ref[...]).astype(BF16)
    s = lax.dot_general(q4, kc_ref[...].astype(BF16), NT_DIMS, preferred_element_type=F32)
    bias = cb_ref[...].reshape(NSA_GROUP * tq, s_all)
    s = s + bias
    ok = bias > 0.5 * NEG_INF
    m = jnp.max(s, axis=-1, keepdims=True)
    e = jnp.exp(s - m)
    p = jnp.where(ok, e / jnp.sum(e, axis=-1, keepdims=True), 0.0)
    o4 = jnp.dot(p.astype(BF16), vc_ref[...].astype(BF16), preferred_element_type=F32)
    for j in range(NSA_GROUP):
        o_ref[:, j * LANE:(j + 1) * LANE] = o4[j * tq:(j + 1) * tq] * gate_ref[:, j:j + 1]

    psum = p[0:tq] + p[tq:2 * tq] + p[2 * tq:3 * tq] + p[3 * tq:4 * tq]
    ovl = ovl_ref[...]
    p_hi = psum.astype(BF16)
    r1 = psum - p_hi.astype(F32)
    p_mid = r1.astype(BF16)
    p_lo = (r1 - p_mid.astype(F32)).astype(BF16)
    imp = (jnp.dot(p_hi, ovl, preferred_element_type=F32) + jnp.dot(p_mid, ovl, preferred_element_type=F32)
           + jnp.dot(p_lo, ovl, preferred_element_type=F32))

    lane = lax.broadcasted_iota(jnp.int32, (tq, nsbp), 1)
    qpos = q0 + i * tq + lax.broadcasted_iota(jnp.int32, (tq, nsbp), 0)
    cur = qpos >> SEL_SHIFT
    forced = (lane == 0) | (lane == cur) | (lane == cur - 1)
    sb_ok = (lane << SEL_SHIFT) <= qpos
    score = jnp.where(sb_ok, imp + jnp.where(forced, FORCE_BONUS, 0.0), NEG_INF)
    work = jnp.where(lane < n_sb, score, -jnp.inf)
    lane_f = lane.astype(F32)
    selneg = jnp.full((tq, nsbp), NEG_INF, F32)
    for _ in range(N_SEL):
        mx = jnp.max(work, axis=-1, keepdims=True)
        first = jnp.min(jnp.where(work == mx, lane_f, 1e9), axis=-1, keepdims=True)
        hit = lane_f == first
        selneg = jnp.where(hit & (mx > 0.5 * NEG_INF), 0.0, selneg)
        work = jnp.where(hit, -jnp.inf, work)
    sel_ref[...] = selneg


def nsa_cmp_attention(proj3, kvc, cmp_bias, gates_r, ovl, tq, q0, n_sb):
    b, t, _ = proj3.shape
    s_all = kvc.shape[3]
    nsbp = ovl.shape[1]
    kern = functools.partial(_cmp_attn_kernel, tq=tq, q0=q0, n_sb=n_sb)
    return pl.pallas_call(
        kern,
        grid=(b, NSA_KV_HEADS, t // tq),
        in_specs=[
            pl.BlockSpec((None, tq, NSA_GROUP * NSA_HD), lambda i, g, q: (i, q, g)),
            pl.BlockSpec((None, None, None, s_all, NSA_HD), lambda i, g, q: (i, 0, g, 0, 0)),
            pl.BlockSpec((None, None, None, s_all, NSA_HD), lambda i, g, q: (i, 1, g, 0, 0)),
            pl.BlockSpec((NSA_GROUP, tq, s_all), lambda i, g, q: (g, q, 0)),
            pl.BlockSpec((None, None, tq, 3 * NSA_GROUP), lambda i, g, q: (i, g, q, 0)),
            pl.BlockSpec((s_all, nsbp), lambda i, g, q: (0, 0)),
        ],
        out_specs=[
            pl.BlockSpec((None, tq, NSA_GROUP * NSA_HD), lambda i, g, q: (i, q, g)),
            pl.BlockSpec((None, None, tq, nsbp), lambda i, g, q: (i, g, q, 0)),
        ],
        out_shape=[jax.ShapeDtypeStruct((b, t, NSA_QDIM), F32),
                   jax.ShapeDtypeStruct((b, NSA_KV_HEADS, t, nsbp), F32)],
        compiler_params=pltpu.CompilerParams(dimension_semantics=("parallel", "parallel", "parallel"),
                                             vmem_limit_bytes=VMEM_LIMIT),
        name="nsa_cmp_attn",
    )(proj3, kvc, kvc, cmp_bias, gates_r, ovl)


def _softmax_tile_update(s, v, m_prev, l_prev, acc_prev):
    m_new = jnp.maximum(m_prev, jnp.max(s, axis=-1, keepdims=True))
    alpha = jnp.exp(m_prev - m_new)
    p = jnp.exp(s - m_new)
    l_new = alpha * l_prev + jnp.sum(p, axis=-1, keepdims=True)
    acc_new = alpha * acc_prev + jnp.dot(p.astype(BF16), v, preferred_element_type=F32)
    return m_new, l_new, acc_new


def _tile_scores(q4, k, bias3, sel, tq, tk, qpos0, kpos0, window):
    s = lax.dot_general(q4, k, NT_DIMS, preferred_element_type=F32).reshape(NSA_GROUP, tq, tk) + bias3
    kpos = kpos0 + lax.broadcasted_iota(jnp.int32, (tq, tk), 1)
    qpos = qpos0 + lax.broadcasted_iota(jnp.int32, (tq, tk), 0)
    dist = qpos - kpos
    mask = dist >= 0
    if window is not None:
        mask = mask & (dist < window)
    if sel is not None:
        nsbp = sel.shape[1]
        blk = (kpos0 + lax.broadcasted_iota(jnp.int32, (nsbp, tk), 1)) >> SEL_SHIFT
        onehot = jnp.where(blk == lax.broadcasted_iota(jnp.int32, (nsbp, tk), 0), 1.0, 0.0).astype(BF16)
        s = s + jnp.dot(sel.astype(BF16), onehot, preferred_element_type=F32)[None]
    return jnp.where(mask[None], s, NEG_INF).reshape(NSA_GROUP * tq, tk)


def _flash_prompt_kernel(*refs, tq, tk, nkk, use_sel, window, gcol):
    if use_sel:
        q_ref, k_ref, v_ref, b_ref, gate_ref, sel_ref, o_ref, q4_ref, m_ref, l_ref, acc_ref = refs
    else:
        q_ref, k_ref, v_ref, b_ref, gate_ref, o_ref, q4_ref, m_ref, l_ref, acc_ref = refs
        sel_ref = None
    i = pl.program_id(2)
    jj = pl.program_id(3)
    if use_sel:
        j, valid = jj, jj <= i
    else:
        j = i - (nkk - 1) + jj
        valid = j >= 0

    @pl.when(jj == 0)
    def _():
        q4_ref[...] = _stack_heads(q_ref[...]).astype(BF16)
        m_ref[...] = jnp.full(m_ref.shape, NEG_INF, F32)
        l_ref[...] = jnp.zeros(l_ref.shape, F32)
        acc_ref[...] = jnp.zeros(acc_ref.shape, F32)

    @pl.when(valid)
    def _():
        sel = sel_ref[...] if use_sel else None
        s = _tile_scores(q4_ref[...], k_ref[...].astype(BF16), b_ref[...], sel, tq, tk, i * tq, j * tk, window)
        m_new, l_new, acc_new = _softmax_tile_update(s, v_ref[...].astype(BF16), m_ref[...], l_ref[...],
                                                     acc_ref[...])
        m_ref[...] = m_new
        l_ref[...] = l_new
        acc_ref[...] = acc_new

    @pl.when(jj == nkk - 1)
    def _():
        o4 = acc_ref[...] / l_ref[...]
        for h in range(NSA_GROUP):
            o_ref[:, h * LANE:(h + 1) * LANE] = o4[h * tq:(h + 1) * tq] * gate_ref[:, gcol + h:gcol + h + 1]


def nsa_flash_prompt(proj3, bias_tab, gates_r, sel, *, tile, kcol, vcol, window, gcol, name):
    b, t, _ = proj3.shape
    nq = t // tile
    n_dd = bias_tab.shape[1]
    use_sel = sel is not None
    nkk = nq if use_sel else (window + tile - 1) // tile + 1

    def kidx(q, jj):
        return jnp.minimum(jj, q) if use_sel else jnp.maximum(q - (nkk - 1) + jj, 0)

    in_specs = [
        pl.BlockSpec((None, tile, NSA_GROUP * NSA_HD), lambda i, g, q, jj: (i, q, g)),
        pl.BlockSpec((None, tile, NSA_HD), lambda i, g, q, jj: (i, kidx(q, jj), kcol + g)),
        pl.BlockSpec((None, tile, NSA_HD), lambda i, g, q, jj: (i, kidx(q, jj), vcol + g)),
        pl.BlockSpec((NSA_GROUP, None, tile, tile),
                     lambda i, g, q, jj: (g, jnp.minimum(q - kidx(q, jj), n_dd - 1), 0, 0)),
        pl.BlockSpec((None, None, tile, 3 * NSA_GROUP), lambda i, g, q, jj: (i, g, q, 0)),
    ]
    args = [proj3, proj3, proj3, bias_tab, gates_r]
    if use_sel:
        nsbp = sel.shape[-1]
        in_specs.append(pl.BlockSpec((None, None, tile, nsbp), lambda i, g, q, jj: (i, g, q, 0)))
        args.append(sel)
    kern = functools.partial(_flash_prompt_kernel, tq=tile, tk=tile, nkk=nkk, use_sel=use_sel, window=window,
                             gcol=gcol)
    return pl.pallas_call(
        kern,
        grid=(b, NSA_KV_HEADS, nq, nkk),
        in_specs=in_specs,
        out_specs=pl.BlockSpec((None, tile, NSA_GROUP * NSA_HD), lambda i, g, q, jj: (i, q, g)),
        out_shape=jax.ShapeDtypeStruct((b, t, NSA_QDIM), F32),
        scratch_shapes=[pltpu.VMEM((NSA_GROUP * tile, NSA_HD), BF16),
                        pltpu.VMEM((NSA_GROUP * tile, 1), F32),
                        pltpu.VMEM((NSA_GROUP * tile, 1), F32),
                        pltpu.VMEM((NSA_GROUP * tile, NSA_HD), F32)],
        compiler_params=pltpu.CompilerParams(
            dimension_semantics=("parallel", "parallel", "parallel", "arbitrary"), vmem_limit_bytes=VMEM_LIMIT),
        name=name,
    )(*args)


def _flash_sample_kernel(*refs, tq, n_tiles, kbase, q0, use_sel, window, gcol):
    if use_sel:
        (_, q_ref, k_ref, v_ref, kn_ref, vn_ref, b_ref, bn_ref, gate_ref, sel_ref, o_ref,
         q4_ref, m_ref, l_ref, acc_ref) = refs
    else:
        (_, q_ref, k_ref, v_ref, kn_ref, vn_ref, b_ref, bn_ref, gate_ref, o_ref,
         q4_ref, m_ref, l_ref, acc_ref) = refs
        sel_ref = None
    p = pl.program_id(1)
    tk = k_ref.shape[0]

    @pl.when(p == 0)
    def _():
        for g in range(NSA_KV_HEADS):
            q4_ref[g] = _stack_heads(q_ref[:, g * NSA_GROUP * LANE:(g + 1) * NSA_GROUP * LANE]).astype(BF16)
        m_ref[...] = jnp.full(m_ref.shape, NEG_INF, F32)
        l_ref[...] = jnp.zeros(l_ref.shape, F32)
        acc_ref[...] = jnp.zeros(acc_ref.shape, F32)

    def update(g, k, v, bias3, kpos0):
        sel = sel_ref[g] if use_sel else None
        s = _tile_scores(q4_ref[g], k, bias3, sel, tq, tk, q0, kpos0, window)
        m_new, l_new, acc_new = _softmax_tile_update(s, v, m_ref[g], l_ref[g], acc_ref[g])
        m_ref[g] = m_new
        l_ref[g] = l_new
        acc_ref[g] = acc_new

    for g in range(NSA_KV_HEADS):
        cols = slice(g * LANE, (g + 1) * LANE)
        update(g, k_ref[:, cols].astype(BF16), v_ref[:, cols].astype(BF16),
               b_ref[g * NSA_GROUP:(g + 1) * NSA_GROUP], kbase + p * tk)

    @pl.when(p == n_tiles - 1)
    def _():
        pad = jnp.zeros((tk - tq, LANE), F32)
        for g in range(NSA_KV_HEADS):
            cols = slice(g * LANE, (g + 1) * LANE)
            kn = jnp.concatenate([kn_ref[:, cols], pad], axis=0).astype(BF16)
            vn = jnp.concatenate([vn_ref[:, cols], pad], axis=0).astype(BF16)
            update(g, kn, vn, bn_ref[g * NSA_GROUP:(g + 1) * NSA_GROUP], q0)
            o4 = acc_ref[g] / l_ref[g]
            for j in range(NSA_GROUP):
                h = g * NSA_GROUP + j
                o_ref[:, h * LANE:(h + 1) * LANE] = o4[j * tq:(j + 1) * tq] * gate_ref[g, :, gcol + j:gcol + j + 1]


def nsa_flash_sample(proj3, pool, table, bias_tab, gates_r, sel, *, kcol, vcol, newk, newv, kbase, q0, window,
                     gcol, name):
    b, tq, _ = proj3.shape
    n_tiles = table.shape[1]
    tk = pool.shape[1]
    n_dd = bias_tab.shape[1]
    qtile = (q0 - kbase) // tk
    use_sel = sel is not None
    in_specs = [
        pl.BlockSpec((None, tq, NSA_QDIM), lambda i, p, pt: (i, 0, 0)),
        pl.BlockSpec((None, tk, NSA_KVDIM), lambda i, p, pt: (pt[i, p], 0, kcol)),
        pl.BlockSpec((None, tk, NSA_KVDIM), lambda i, p, pt: (pt[i, p], 0, vcol)),
        pl.BlockSpec((None, tq, NSA_KVDIM), lambda i, p, pt: (i, 0, newk)),
        pl.BlockSpec((None, tq, NSA_KVDIM), lambda i, p, pt: (i, 0, newv)),
        pl.BlockSpec((NSA_HEADS, None, tq, tk), lambda i, p, pt: (0, jnp.clip(qtile - p, 0, n_dd - 1), 0, 0)),
        pl.BlockSpec((NSA_HEADS, None, tq, tk), lambda i, p, pt: (0, 0, 0, 0)),
        pl.BlockSpec((None, NSA_KV_HEADS, tq, 3 * NSA_GROUP), lambda i, p, pt: (i, 0, 0, 0)),
    ]
    args = [proj3, pool, pool, proj3, proj3, bias_tab, bias_tab, gates_r]
    if use_sel:
        nsbp = sel.shape[-1]
        in_specs.append(pl.BlockSpec((None, NSA_KV_HEADS, tq, nsbp), lambda i, p, pt: (i, 0, 0, 0)))
        args.append(sel)
    kern = functools.partial(_flash_sample_kernel, tq=tq, n_tiles=n_tiles, kbase=kbase, q0=q0, use_sel=use_sel,
                             window=window, gcol=gcol)
    rows4 = NSA_GROUP * tq
    grid_spec = pltpu.PrefetchScalarGridSpec(
        num_scalar_prefetch=1, grid=(b, n_tiles), in_specs=in_specs,
        out_specs=pl.BlockSpec((None, tq, NSA_QDIM), lambda i, p, pt: (i, 0, 0)),
        scratch_shapes=[pltpu.VMEM((NSA_KV_HEADS, rows4, NSA_HD), BF16),
                        pltpu.VMEM((NSA_KV_HEADS, rows4, 1), F32),
                        pltpu.VMEM((NSA_KV_HEADS, rows4, 1), F32),
                        pltpu.VMEM((NSA_KV_HEADS, rows4, NSA_HD), F32)])
    return pl.pallas_call(
        kern, grid_spec=grid_spec, out_shape=jax.ShapeDtypeStruct((b, tq, NSA_QDIM), F32),
        compiler_params=pltpu.CompilerParams(dimension_semantics=("parallel", "arbitrary"),
                                             vmem_limit_bytes=VMEM_LIMIT),
        name=name,
    )(table, *args)


def _bias_table(rel_bias, unit, tq, tk):
    n_dd = -(-(T5_SATURATION + tk - 1) // unit) + 1
    dist = (jnp.arange(n_dd)[:, None, None] * unit + jnp.arange(tq)[None, :, None]
            - jnp.arange(tk)[None, None, :])
    return jnp.moveaxis(rel_bias.astype(F32)[t5_bucket(dist)], -1, 0)


def _cmp_tables(rel_bias, q0, t, s_all, nc, n_sb, nsbp):
    c = jnp.arange(s_all)
    c_end = c * CMP_STRIDE + (CMP_BLOCK - 1)
    dist = (q0 + jnp.arange(t))[:, None] - c_end[None, :]
    ok = (dist >= 0) & (c < nc)[None, :]
    bias = jnp.where(ok[None], jnp.moveaxis(rel_bias.astype(F32)[t5_bucket(dist)], -1, 0), NEG_INF)
    sb_start = jnp.arange(nsbp) * SEL_BLOCK
    c_start = c * CMP_STRIDE
    ovl = jnp.maximum(jnp.minimum(c_end[:, None], sb_start[None, :] + SEL_BLOCK - 1)
                      - jnp.maximum(c_start[:, None], sb_start[None, :]) + 1, 0).astype(F32) / CMP_BLOCK
    ovl = jnp.where((c < nc)[:, None] & (jnp.arange(nsbp) < n_sb)[None, :], ovl, 0.0)
    return bias, ovl.astype(BF16)


def _nsa_weights(w_in, q_norm, k_norm, cmp_pe, cmp_w1, cmp_w2, rel_bias, w_out):
    n_main = NSA_QDIM + 6 * NSA_KVDIM
    nsub = CMP_BLOCK // CMP_STRIDE
    w1r = cmp_w1.reshape(2, nsub, CMP_STRIDE, NSA_HD, CMP_HIDDEN)
    w1r = jnp.moveaxis(w1r, 1, 3).reshape(2, CMP_STRIDE, NSA_HD, nsub * CMP_HIDDEN).astype(BF16)
    pe_hid = jnp.einsum('ck,cke->ce', cmp_pe.reshape(2, -1), cmp_w1, precision=lax.Precision.HIGHEST)
    return dict(
        w_main=w_in[:, :n_main].astype(BF16),
        w_gate=_pad_cols(w_in[:, n_main:], LANE).astype(BF16),
        post=_nsa_in_post(q_norm, k_norm),
        w1r=w1r, pe_hid=pe_hid.reshape(2, 1, CMP_HIDDEN), w2=cmp_w2.astype(BF16), k_norm0=k_norm[0],
        rel_bias=rel_bias, w_out=w_out.astype(BF16))


def nsa_mixer(h2d, b, t, norm_in, nw, paged):
    post_main, post_gate = nw["post"]
    proj2 = proj(h2d, nw["w_main"], norm_w=norm_in, post=post_main, name="nsa_in")
    gate_out = proj(h2d, nw["w_gate"], norm_w=norm_in, post=post_gate, name="nsa_in_gate")
    proj3 = proj2.reshape(b, t, -1)
    gates_r = gate_out[:, :3 * NSA_HEADS].reshape(b, t, 3, NSA_KV_HEADS, NSA_GROUP)
    gates_r = gates_r.transpose(0, 3, 1, 2, 4).reshape(b, NSA_KV_HEADS, t, 3 * NSA_GROUP)
    kv_blk0 = NSA_QDIM // NSA_KVDIM
    kv_col0 = NSA_QDIM // NSA_HD
    rel_bias = nw["rel_bias"]
    if paged is None:
        p_len, tq = 0, 256
        kvc = nsa_compress(proj3, None, kv_blk0, nw["w1r"], nw["pe_hid"], nw["w2"], nw["k_norm0"])
    else:
        pool, page_table, win_state = paged
        p_len, tq = page_table.shape[1] * PAGE_SIZE, t
        kvc = nsa_compress(pool, page_table, 0, nw["w1r"], nw["pe_hid"], nw["w2"], nw["k_norm0"])
    n_all = p_len + t
    s_all = kvc.shape[3]
    nc = n_all // CMP_STRIDE - CMP_BLOCK // CMP_STRIDE + 1
    n_sb = -(-n_all // SEL_BLOCK)
    nsbp = -(-n_sb // LANE) * LANE
    cmp_bias, ovl = _cmp_tables(rel_bias, p_len, t, s_all, nc, n_sb, nsbp)
    o_cmp, sel = nsa_cmp_attention(proj3, kvc, cmp_bias, gates_r, ovl, tq, p_len, n_sb)
    if paged is None:
        tab = _bias_table(rel_bias, tq, tq, tq)
        o_sel = nsa_flash_prompt(proj3, tab, gates_r, sel, tile=tq, kcol=kv_col0 + 2 * NSA_KV_HEADS,
                                 vcol=kv_col0 + 3 * NSA_KV_HEADS, window=None, gcol=NSA_GROUP, name="nsa_sel")
        o_win = nsa_flash_prompt(proj3, tab, gates_r, None, tile=tq, kcol=kv_col0 + 4 * NSA_KV_HEADS,
                                 vcol=kv_col0 + 5 * NSA_KV_HEADS, window=WINDOW, gcol=2 * NSA_GROUP, name="nsa_win")
    else:
        tab = _bias_table(rel_bias, PAGE_SIZE, t, PAGE_SIZE)
        o_sel = nsa_flash_sample(proj3, pool, page_table, tab, gates_r, sel, kcol=2, vcol=3, newk=kv_blk0 + 2,
                                 newv=kv_blk0 + 3, kbase=0, q0=p_len, window=None, gcol=NSA_GROUP,
                                 name="nsa_sel_paged")
        n_wt = WINDOW // PAGE_SIZE
        win_pool = win_state.reshape(b * n_wt, PAGE_SIZE, 2 * NSA_KVDIM)
        win_table = jnp.arange(b * n_wt, dtype=jnp.int32).reshape(b, n_wt)
        o_win = nsa_flash_sample(proj3, win_pool, win_table, tab, gates_r, None, kcol=0, vcol=1, newk=kv_blk0 + 4,
                                 newv=kv_blk0 + 5, kbase=p_len - WINDOW, q0=p_len, window=WINDOW,
                                 gcol=2 * NSA_GROUP, name="nsa_win_paged")
    y = proj((o_cmp.reshape(b * t, NSA_QDIM), o_sel.reshape(b * t, NSA_QDIM), o_win.reshape(b * t, NSA_QDIM)),
             nw["w_out"], res=h2d, name="nsa_out")
    new_rows = proj3[:, :, NSA_QDIM:NSA_QDIM + 4 * NSA_KVDIM].reshape(b, t, 4, NSA_KV_HEADS, NSA_HD)
    new_win = proj3[:, :, NSA_QDIM + 4 * NSA_KVDIM:].reshape(b, t, 2, NSA_KV_HEADS, NSA_HD)
    return y, new_rows, new_win


def _pad_cols(w, n_pad):
    return jnp.pad(w, ((0, 0), (0, n_pad - w.shape[1])))


def _nsa_in_post(q_norm, k_norm):
    ones_kv = jnp.ones((NSA_KVDIM,), F32)
    zeros_kv = jnp.zeros((NSA_KVDIM,), F32)
    pw = jnp.concatenate([
        jnp.tile(q_norm.astype(F32) * (NSA_HD ** -0.5), NSA_HEADS),
        ones_kv, ones_kv, jnp.tile(k_norm[1].astype(F32), NSA_KV_HEADS), ones_kv,
        jnp.tile(k_norm[2].astype(F32), NSA_KV_HEADS), ones_kv])
    pm = jnp.concatenate([
        jnp.ones((NSA_QDIM,), F32),
        zeros_kv, zeros_kv, ones_kv, zeros_kv, ones_kv, zeros_kv])
    return (pw, pm), (jnp.ones((LANE,), F32), jnp.full((LANE,), 2.0, F32))


def kernel(x_prompt, x_sample, state_gdn, state_gdn_conv, cache_nsa_kv, state_nsa_win, state_ffn_conv, page_table,
           norm_mix, norm_ffn, gdn_w_in, gdn_conv_w, gdn_A_log, gdn_dt_bias, gdn_norm, gdn_w_out,
           nsa_w_in, nsa_q_norm, nsa_k_norm, nsa_cmp_pe, nsa_cmp_w1, nsa_cmp_w2, rel_bias, nsa_w_out,
           ffn_w_up, ffn_conv_w, ffn_conv_b, ffn_w_down):
    depth = norm_mix.shape[0]
    bp, tp, d = x_prompt.shape
    bs, ts, _ = x_sample.shape
    win_buf = state_nsa_win.shape[2]
    assert win_buf == WINDOW and tp >= WINDOW and cache_nsa_kv.shape[2] == PAGE_SIZE
    d_ff = ffn_conv_w.shape[1]
    conv_keep = ffn_conv_w.shape[2] - 1
    hp = x_prompt.reshape(bp * tp, d)
    hs = x_sample.reshape(bs * ts, d)
    gdn_p, gdnc_p, kv_p, win_p, ffn_p = [], [], [], [], []
    gdn_s, gdnc_s, kv_s, win_s, ffn_s = [], [], [], [], []
    for i in range(depth):
        j = i // 2
        if i % 2 == 0:
            n_main = GDN_CONV_DIM + GDN_VDIM
            w_in = (gdn_w_in[j][:, :n_main].astype(BF16), _pad_cols(gdn_w_in[j][:, n_main:], LANE).astype(BF16))
            gw = (norm_mix[i], w_in, gdn_conv_w[j], gdn_A_log[j], gdn_dt_bias[j], gdn_norm[j],
                  gdn_w_out[j].astype(BF16))
            hp, st_p, cv_p = gdn_mixer(hp, bp, tp, jnp.zeros((bp, GDN_V_HEADS, GDN_DK, GDN_DV), F32),
                                       jnp.zeros((bp, gdn_conv_w.shape[2] - 1, GDN_CONV_DIM), F32), *gw)
            hs, st_s, cv_s = gdn_mixer(hs, bs, ts, state_gdn[j], state_gdn_conv[j], *gw)
            gdn_p.append(st_p)
            gdnc_p.append(cv_p)
            gdn_s.append(st_s)
            gdnc_s.append(cv_s)
        else:
            nw = _nsa_weights(nsa_w_in[j], nsa_q_norm[j], nsa_k_norm[j], nsa_cmp_pe[j], nsa_cmp_w1[j],
                              nsa_cmp_w2[j], rel_bias, nsa_w_out[j])
            hp, rows_p, nwin_p = nsa_mixer(hp, bp, tp, norm_mix[i], nw, None)
            pool = cache_nsa_kv[j].reshape(cache_nsa_kv.shape[1], PAGE_SIZE, 4 * NSA_KVDIM)
            win_state = state_nsa_win[j].reshape(bs, win_buf, 2 * NSA_KVDIM)
            hs, rows_s, nwin_s = nsa_mixer(hs, bs, ts, norm_mix[i], nw, (pool, page_table, win_state))
            kv_p.append(rows_p)
            win_p.append(nwin_p[:, tp - WINDOW:])
            kv_s.append(rows_s)
            win_s.append(jnp.concatenate([state_nsa_win[j][:, ts:], nwin_s], axis=1))
        fw = (norm_ffn[i], ffn_w_up[i].astype(BF16), ffn_conv_w[i], ffn_conv_b[i], ffn_w_down[i].astype(BF16))
        hp, cp = conv_ffn(hp, bp, tp, jnp.zeros((bp, conv_keep, d_ff), F32), *fw)
        hs, cs = conv_ffn(hs, bs, ts, state_ffn_conv[i], *fw)
        ffn_p.append(cp)
        ffn_s.append(cs)
    return (hp.reshape(bp, tp, d), hs.reshape(bs, ts, d),
            jnp.stack(gdn_p), jnp.stack(gdnc_p), jnp.stack(kv_p), jnp.stack(win_p), jnp.stack(ffn_p),
            jnp.stack(gdn_s), jnp.stack(gdnc_s), jnp.stack(kv_s), jnp.stack(win_s), jnp.stack(ffn_s))
```

```python
import functools
import math

import jax
import jax.numpy as jnp
from jax import lax
from jax.experimental import pallas as pl
from jax.experimental.pallas import tpu as pltpu

F32 = jnp.float32
BF16 = jnp.bfloat16
RMS_EPS = 1e-6
NEG_INF = -1e30

LANE = 128
SUBLANE = 8
VMEM_LIMIT = 56 * 1024 * 1024
PROJ_VMEM_BUDGET = 40 * 1024 * 1024
FFN_DOWN_ROWS = 256

D_MODEL = 2048
GDN_K_HEADS = 16
GDN_V_HEADS = 32
GDN_DK = 128
GDN_DV = 128
GDN_KDIM = GDN_K_HEADS * GDN_DK
GDN_VDIM = GDN_V_HEADS * GDN_DV
GDN_CONV_DIM = 2 * GDN_KDIM + GDN_VDIM
GDN_CHUNK = 64

NSA_HEADS = 16
NSA_KV_HEADS = 4
NSA_HD = 128
NSA_GROUP = NSA_HEADS // NSA_KV_HEADS
NSA_QDIM = NSA_HEADS * NSA_HD
NSA_KVDIM = NSA_KV_HEADS * NSA_HD
CMP_BLOCK = 32
CMP_STRIDE = 16
CMP_HIDDEN = 2 * NSA_HD
SEL_BLOCK = 64
SEL_SHIFT = 6
N_SEL = 16
WINDOW = 512
FORCE_BONUS = 1e3
N_BUCKETS = 32
REL_MAX_DIST = 1024
PAGE_SIZE = 128
PAGES_PER_STEP = 4
T5_SATURATION = 790

NT_DIMS = (((1,), (1,)), ((), ()))


def _proj_kernel(*refs, n_x, has_norm, has_res, has_post):
    it = iter(refs)
    x_refs = [next(it) for _ in range(n_x)]
    nw_ref = next(it) if has_norm else None
    w_ref = next(it)
    res_ref = next(it) if has_res else None
    pw_ref = next(it) if has_post else None
    pm_ref = next(it) if has_post else None
    o_ref = next(it)
    xs_ref = next(it)

    @pl.when(pl.program_id(1) == 0)
    def _():
        x = x_refs[0][...].astype(F32)
        for r in x_refs[1:]:
            x = x + r[...].astype(F32)
        if has_norm:
            ms = jnp.mean(x * x, axis=-1, keepdims=True)
            x = x * lax.rsqrt(ms + RMS_EPS) * nw_ref[...]
        xs_ref[...] = x.astype(BF16)

    y = jnp.dot(xs_ref[...], w_ref[...], preferred_element_type=F32)
    if has_res:
        y = y + res_ref[...]
    if has_post:
        tn = y.shape[1]
        for g in range(tn // LANE):
            sl = slice(g * LANE, (g + 1) * LANE)
            yg = y[:, sl]
            mode = pm_ref[:, sl]
            ms = jnp.mean(yg * yg, axis=-1, keepdims=True)
            normed = yg * lax.rsqrt(ms + RMS_EPS) * pw_ref[:, sl]
            sig = jax.nn.sigmoid(yg)
            o_ref[:, sl] = jnp.where(mode == 1.0, normed, jnp.where(mode == 2.0, sig, yg))
    else:
        o_ref[...] = y.astype(o_ref.dtype)


def _pick_tile(n, candidates):
    for c in candidates:
        if n % c == 0:
            return c
    return n


def proj(xs, w, *, norm_w=None, res=None, post=None, out_dtype=F32, name="proj"):
    if not isinstance(xs, (tuple, list)):
        xs = (xs,)
    m, k = xs[0].shape
    n = w.shape[1]
    tn = _pick_tile(n, (512, 384, 256, 128))
    x_row_bytes = sum(k * x.dtype.itemsize for x in xs)

    def vmem_bytes(tm):
        return 2 * (tm * x_row_bytes + k * tn * 2 + tm * tn * 4 * (2 if res is not None else 1)) + tm * k * 2

    tm = next((c for c in (1024, 512, 256) if m % c == 0 and vmem_bytes(c) <= PROJ_VMEM_BUDGET), m)
    in_specs = [pl.BlockSpec((tm, k), lambda i, j: (i, 0)) for _ in xs]
    args = list(xs)
    if norm_w is not None:
        in_specs.append(pl.BlockSpec((1, k), lambda i, j: (0, 0)))
        args.append(norm_w.reshape(1, k).astype(F32))
    in_specs.append(pl.BlockSpec((k, tn), lambda i, j: (0, j)))
    args.append(w)
    if res is not None:
        in_specs.append(pl.BlockSpec((tm, tn), lambda i, j: (i, j)))
        args.append(res)
    if post is not None:
        for a in post:
            in_specs.append(pl.BlockSpec((1, tn), lambda i, j: (0, j)))
            args.append(a.reshape(1, n).astype(F32))
    kern = functools.partial(_proj_kernel, n_x=len(xs), has_norm=norm_w is not None, has_res=res is not None,
                             has_post=post is not None)
    return pl.pallas_call(
        kern,
        grid=(m // tm, n // tn),
        in_specs=in_specs,
        out_specs=pl.BlockSpec((tm, tn), lambda i, j: (i, j)),
        out_shape=jax.ShapeDtypeStruct((m, n), out_dtype),
        scratch_shapes=[pltpu.VMEM((tm, k), BF16)],
        compiler_params=pltpu.CompilerParams(dimension_semantics=("parallel", "arbitrary"),
                                             vmem_limit_bytes=VMEM_LIMIT),
        name=name,
    )(*args)


def _conv_gate(pad_ref, prev, gate, val, cw, cb, width):
    t = gate.shape[0]
    halo = SUBLANE
    pad_ref[halo - (width - 1):halo, :] = prev
    pad_ref[halo:halo + t, :] = gate
    acc = cb + pad_ref[halo - (width - 1):halo - (width - 1) + t, :] * cw[0:1, :]
    for i in range(1, width):
        off = halo - (width - 1) + i
        acc = acc + pad_ref[off:off + t, :] * cw[i:i + 1, :]
    return jax.nn.silu(acc) * val


def _ffn_gate_kernel(gate_ref, val_ref, pre_ref, cw_ref, cb_ref, o_ref, pad_ref, *, width):
    for i in range(gate_ref.shape[0]):
        hid = _conv_gate(pad_ref, pre_ref[i], gate_ref[i], val_ref[i], cw_ref[...], cb_ref[...], width)
        o_ref[i] = hid.astype(o_ref.dtype)


def ffn_gate(up, prefix, conv_w, conv_b, d_ff):
    b, t, _ = up.shape
    width = conv_w.shape[1]
    nblk = d_ff // LANE
    kern = functools.partial(_ffn_gate_kernel, width=width)
    return pl.pallas_call(
        kern,
        grid=(nblk,),
        in_specs=[
            pl.BlockSpec((b, t, LANE), lambda j: (0, 0, j)),
            pl.BlockSpec((b, t, LANE), lambda j: (0, 0, j + nblk)),
            pl.BlockSpec((b, width - 1, LANE), lambda j: (0, 0, j)),
            pl.BlockSpec((width, LANE), lambda j: (0, j)),
            pl.BlockSpec((1, LANE), lambda j: (0, j)),
        ],
        out_specs=pl.BlockSpec((b, t, LANE), lambda j: (0, 0, j)),
        out_shape=jax.ShapeDtypeStruct((b, t, d_ff), BF16),
        scratch_shapes=[pltpu.VMEM((t + SUBLANE, LANE), F32)],
        compiler_params=pltpu.CompilerParams(dimension_semantics=("parallel",)),
        name="ffn_gate",
    )(up, up, prefix, conv_w.T, conv_b.reshape(1, d_ff))


def _ffn_down_kernel(gate_ref, val_ref, halo_ref, pre_ref, cw_ref, cb_ref, w_ref, res_ref, o_ref, hid_ref, pad_ref, *,
                     width, tiles_per_batch):
    @pl.when(pl.program_id(1) == 0)
    def _():
        first = (pl.program_id(0) % tiles_per_batch) == 0

        def column_block(c, carry):
            cs = pl.ds(pl.multiple_of(c * LANE, LANE), LANE)
            prev = jnp.where(first, pre_ref[:, cs], halo_ref[SUBLANE - (width - 1):SUBLANE, cs])
            hid = _conv_gate(pad_ref, prev, gate_ref[:, cs], val_ref[:, cs], cw_ref[:, cs], cb_ref[:, cs], width)
            hid_ref[:, cs] = hid.astype(BF16)
            return carry

        lax.fori_loop(0, gate_ref.shape[1] // LANE, column_block, 0)

    o_ref[...] = jnp.dot(hid_ref[...], w_ref[...], preferred_element_type=F32) + res_ref[...]


def ffn_down_fused(up2d, b, t, prefix, conv_w, conv_b, w_down, res):
    d_ff, n = w_down.shape
    width = conv_w.shape[1]
    tm, tn = FFN_DOWN_ROWS, 512
    assert t % tm == 0 and n % tn == 0
    tiles_per_batch = t // tm
    kern = functools.partial(_ffn_down_kernel, width=width, tiles_per_batch=tiles_per_batch)
    return pl.pallas_call(
        kern,
        grid=(b * tiles_per_batch, n // tn),
        in_specs=[
            pl.BlockSpec((tm, d_ff), lambda i, j: (i, 0)),
            pl.BlockSpec((tm, d_ff), lambda i, j: (i, 1)),
            pl.BlockSpec((SUBLANE, d_ff), lambda i, j: (jnp.maximum(i * (tm // SUBLANE) - 1, 0), 0)),
            pl.BlockSpec((None, width - 1, d_ff), lambda i, j: (i // tiles_per_batch, 0, 0)),
            pl.BlockSpec((width, d_ff), lambda i, j: (0, 0)),
            pl.BlockSpec((1, d_ff), lambda i, j: (0, 0)),
            pl.BlockSpec((d_ff, tn), lambda i, j: (0, j)),
            pl.BlockSpec((tm, tn), lambda i, j: (i, j)),
        ],
        out_specs=pl.BlockSpec((tm, tn), lambda i, j: (i, j)),
        out_shape=jax.ShapeDtypeStruct((b * t, n), F32),
        scratch_shapes=[pltpu.VMEM((tm, d_ff), BF16), pltpu.VMEM((tm + SUBLANE, LANE), F32)],
        compiler_params=pltpu.CompilerParams(dimension_semantics=("parallel", "arbitrary"),
                                             vmem_limit_bytes=VMEM_LIMIT),
        name="ffn_down_fused",
    )(up2d, up2d, up2d, prefix, conv_w.T, conv_b.reshape(1, d_ff), w_down, res)


def conv_ffn(h2d, b, t, prefix, norm_w, w_up, conv_w, conv_b, w_down):
    d_ff = conv_w.shape[0]
    keep = conv_w.shape[1] - 1
    assert t >= keep
    up2d = proj(h2d, w_up, norm_w=norm_w, name="ffn_up")
    up = up2d.reshape(b, t, -1)
    new_prefix = up[:, t - keep:, :d_ff]
    if t % FFN_DOWN_ROWS == 0:
        return ffn_down_fused(up2d, b, t, prefix, conv_w, conv_b, w_down, h2d), new_prefix
    hidden = ffn_gate(up, prefix, conv_w, conv_b, d_ff)
    out = proj(hidden.reshape(b * t, d_ff), w_down, res=h2d, name="ffn_down")
    return out, new_prefix


def t5_bucket(dist):
    d = jnp.maximum(dist, 0)
    max_exact = N_BUCKETS // 2
    scale = (N_BUCKETS - max_exact) / math.log(REL_MAX_DIST / max_exact)
    large = max_exact + (jnp.log(jnp.maximum(d, 1).astype(F32) / max_exact) * scale).astype(jnp.int32)
    return jnp.where(d < max_exact, d, jnp.minimum(large, N_BUCKETS - 1))


GDN_CONV_COLS = 512
GDN_HEADS_PER_STEP = 4


def _gdn_conv_kernel(x_ref, pre_ref, cw_ref, o_ref, pad_ref, *, width, n_q_blocks, n_qk_blocks):
    j = pl.program_id(1)
    t = x_ref.shape[0]
    halo = SUBLANE
    pad_ref[0:halo, :] = jnp.zeros((halo, pad_ref.shape[1]), F32)
    pad_ref[halo - (width - 1):halo, :] = pre_ref[...]
    pad_ref[halo:halo + t, :] = x_ref[...]
    acc = pad_ref[halo - (width - 1):halo - (width - 1) + t, :] * cw_ref[0:1, :]
    for i in range(1, width):
        off = halo - (width - 1) + i
        acc = acc + pad_ref[off:off + t, :] * cw_ref[i:i + 1, :]
    y = jax.nn.silu(acc)
    scale = jnp.where(j < n_q_blocks, GDN_DK ** -0.5, 1.0)
    is_qk = j < n_qk_blocks
    for h in range(x_ref.shape[1] // LANE):
        seg = y[:, h * LANE:(h + 1) * LANE]
        nrm = seg * lax.rsqrt(jnp.sum(seg * seg, axis=-1, keepdims=True) + RMS_EPS) * scale
        o_ref[:, h * LANE:(h + 1) * LANE] = jnp.where(is_qk, nrm, seg)


def gdn_conv(proj3, prefix, conv_w):
    b, t, _ = proj3.shape
    width = conv_w.shape[1]
    cols = GDN_CONV_COLS
    kern = functools.partial(_gdn_conv_kernel, width=width, n_q_blocks=GDN_KDIM // cols,
                             n_qk_blocks=2 * GDN_KDIM // cols)
    return pl.pallas_call(
        kern,
        grid=(b, GDN_CONV_DIM // cols),
        in_specs=[
            pl.BlockSpec((None, t, cols), lambda i, j: (i, 0, j)),
            pl.BlockSpec((None, width - 1, cols), lambda i, j: (i, 0, j)),
            pl.BlockSpec((width, cols), lambda i, j: (0, j)),
        ],
        out_specs=pl.BlockSpec((None, t, cols), lambda i, j: (i, 0, j)),
        out_shape=jax.ShapeDtypeStruct((b, t, GDN_CONV_DIM), F32),
        scratch_shapes=[pltpu.VMEM((t + SUBLANE, cols), F32)],
        compiler_params=pltpu.CompilerParams(dimension_semantics=("parallel", "parallel"),
                                             vmem_limit_bytes=VMEM_LIMIT),
        name="gdn_conv",
    )(proj3, prefix, conv_w.T)


def _split_bf16(x):
    hi = x.astype(BF16)
    return hi, (x - hi.astype(F32)).astype(BF16)


def _dot_split(a_parts, b_parts):
    (ah, al), (bh, bl) = a_parts, b_parts
    return (jnp.dot(ah, bh, preferred_element_type=F32) + jnp.dot(ah, bl, preferred_element_type=F32)
            + jnp.dot(al, bh, preferred_element_type=F32))


def _gdn_gate_kernel(x_ref, a_ref, dt_ref, o_ref, *, t_real):
    L = GDN_CHUNK
    lane = lax.broadcasted_iota(jnp.int32, (L, LANE), 1)
    row = lax.broadcasted_iota(jnp.int32, (L, LANE), 0)
    ci = lax.broadcasted_iota(jnp.int32, (L, L), 0)
    cj = lax.broadcasted_iota(jnp.int32, (L, L), 1)
    tril = jnp.where(ci >= cj, 1.0, 0.0).astype(BF16)
    neg_a = -jnp.exp(a_ref[...])

    def chunk(c, carry):
        r0 = pl.multiple_of(c * L, L)
        x = x_ref[pl.ds(r0, L), :]
        live = (row + r0) < t_real
        beta = jnp.where(live, jax.nn.sigmoid(x), 0.0)
        z = x + dt_ref[...]
        softplus = jnp.maximum(z, 0.0) + jnp.log(1.0 + jnp.exp(-jnp.abs(z)))
        g = jnp.where(live, neg_a * softplus, 0.0)
        hi, mid = _split_bf16(g)
        lo = (g - hi.astype(F32) - mid.astype(F32)).astype(BF16)
        cum = (jnp.dot(tril, hi, preferred_element_type=F32) + jnp.dot(tril, mid, preferred_element_type=F32)
               + jnp.dot(tril, lo, preferred_element_type=F32))
        o_ref[pl.ds(r0, L), :] = jnp.where(lane < GDN_V_HEADS, beta, cum)
        return carry

    lax.fori_loop(0, x_ref.shape[0] // L, chunk, 0)


def gdn_gates(gate3, a_log, dt_bias, t_real):
    b, t, _ = gate3.shape
    pad = jnp.zeros((GDN_V_HEADS,), F32)
    a_vec = jnp.concatenate([pad, a_log.astype(F32), pad, pad]).reshape(1, LANE)
    dt_vec = jnp.concatenate([pad, dt_bias.astype(F32), pad, pad]).reshape(1, LANE)
    return pl.pallas_call(
        functools.partial(_gdn_gate_kernel, t_real=t_real),
        grid=(b,),
        in_specs=[pl.BlockSpec((None, t, LANE), lambda i: (i, 0, 0)),
                  pl.BlockSpec((1, LANE), lambda i: (0, 0)),
                  pl.BlockSpec((1, LANE), lambda i: (0, 0))],
        out_specs=pl.BlockSpec((None, t, LANE), lambda i: (i, 0, 0)),
        out_shape=jax.ShapeDtypeStruct((b, t, LANE), F32),
        compiler_params=pltpu.CompilerParams(dimension_semantics=("parallel",)),
        name="gdn_gates",
    )(gate3, a_vec, dt_vec)


def _pad_rows(x, rows):
    if x.shape[0] == rows:
        return x
    return jnp.concatenate([x, jnp.zeros((rows - x.shape[0], x.shape[1]), x.dtype)], axis=0)


def _gdn_delta_kernel(q_ref, k_ref, v_ref, z_ref, gb_ref, gt_ref, s0_ref, nw_ref, o_ref, s_out_ref, s_ref, *, nc, tl):
    hg = pl.program_id(1)
    c = pl.program_id(2)
    L = GDN_CHUNK
    hb = GDN_HEADS_PER_STEP

    @pl.when(c == 0)
    def _():
        s_ref[...] = s0_ref[...]

    gb = gb_ref[...]
    lane = lax.broadcasted_iota(jnp.int32, (L, LANE), 1)
    ri = lax.broadcasted_iota(jnp.int32, (L, L), 0)
    cj = lax.broadcasted_iota(jnp.int32, (L, L), 1)
    for hh in range(hb):
        head = hg * hb + hh
        kh_i = hh // 2
        q = _pad_rows(q_ref[:, kh_i * LANE:(kh_i + 1) * LANE], L)
        k = _pad_rows(k_ref[:, kh_i * LANE:(kh_i + 1) * LANE], L)
        v = _pad_rows(v_ref[:, hh * LANE:(hh + 1) * LANE], L)
        bcol = jnp.sum(jnp.where(lane == head, gb, 0.0), axis=-1, keepdims=True)
        gcol = jnp.sum(jnp.where(lane == head + GDN_V_HEADS, gb, 0.0), axis=-1, keepdims=True)
        grow = gt_ref[hh, pl.ds(c, 1), :]
        diff = gcol - grow
        decay = jnp.exp(diff)
        dec_strict = jnp.where(ri > cj, decay, 0.0)
        dec_incl = jnp.where(ri >= cj, decay, 0.0)
        kb = k.astype(BF16)
        kk = lax.dot_general(kb, kb, NT_DIMS, preferred_element_type=F32)
        qk = lax.dot_general(q.astype(BF16), kb, NT_DIMS, preferred_element_type=F32)
        eg = jnp.exp(gcol)
        pw = _split_bf16(-(bcol * kk * dec_strict))
        x = jnp.concatenate([bcol * v, (bcol * eg) * k], axis=1)
        n_fac = L.bit_length() - 1
        for i in range(n_fac):
            x = x + _dot_split(pw, _split_bf16(x))
            if i + 1 < n_fac:
                pw = _split_bf16(_dot_split(pw, pw))
        u_eff, w_k = x[:, :GDN_DV], x[:, GDN_DV:]
        s = s_ref[hh]
        sb = s.astype(BF16)
        u = u_eff - jnp.dot(w_k.astype(BF16), sb, preferred_element_type=F32)
        ub = u.astype(BF16)
        o = (jnp.dot((q * eg).astype(BF16), sb, preferred_element_type=F32)
             + jnp.dot((qk * dec_incl).astype(BF16), ub, preferred_element_type=F32))
        g_last = grow[:, L - 1:L]
        k_dec = (k * jnp.exp(g_last - gcol)).astype(BF16)
        s_ref[hh] = jnp.exp(g_last) * s + lax.dot_general(k_dec, ub, (((0,), (0,)), ((), ())),
                                                          preferred_element_type=F32)
        on = o * lax.rsqrt(jnp.mean(o * o, axis=-1, keepdims=True) + RMS_EPS) * nw_ref[...]
        gated = on[:tl] * jax.nn.silu(z_ref[:, hh * LANE:(hh + 1) * LANE])
        o_ref[:, hh * LANE:(hh + 1) * LANE] = gated.astype(o_ref.dtype)

    @pl.when(c == nc - 1)
    def _():
        s_out_ref[...] = s_ref[...]


def gdn_delta(qkv, proj3, gb, gt, s0, norm_w, tl):
    b, t, _ = qkv.shape
    nc = gb.shape[1] // GDN_CHUNK
    hb = GDN_HEADS_PER_STEP
    kw = (hb // 2) * GDN_DK
    vw = hb * GDN_DV
    kern = functools.partial(_gdn_delta_kernel, nc=nc, tl=tl)
    return pl.pallas_call(
        kern,
        grid=(b, GDN_V_HEADS // hb, nc),
        in_specs=[
            pl.BlockSpec((None, tl, kw), lambda i, h, c: (i, c, h)),
            pl.BlockSpec((None, tl, kw), lambda i, h, c: (i, c, GDN_KDIM // kw + h)),
            pl.BlockSpec((None, tl, vw), lambda i, h, c: (i, c, 2 * GDN_KDIM // vw + h)),
            pl.BlockSpec((None, tl, vw), lambda i, h, c: (i, c, GDN_CONV_DIM // vw + h)),
            pl.BlockSpec((None, GDN_CHUNK, LANE), lambda i, h, c: (i, c, 0)),
            pl.BlockSpec((None, hb, nc, GDN_CHUNK), lambda i, h, c: (i, h, 0, 0)),
            pl.BlockSpec((None, hb, GDN_DK, GDN_DV), lambda i, h, c: (i, h, 0, 0)),
            pl.BlockSpec((1, GDN_DV), lambda i, h, c: (0, 0)),
        ],
        out_specs=[
            pl.BlockSpec((None, tl, vw), lambda i, h, c: (i, c, h)),
            pl.BlockSpec((None, hb, GDN_DK, GDN_DV), lambda i, h, c: (i, h, 0, 0)),
        ],
        out_shape=[jax.ShapeDtypeStruct((b, t, GDN_VDIM), BF16),
                   jax.ShapeDtypeStruct(s0.shape, F32)],
        scratch_shapes=[pltpu.VMEM((hb, GDN_DK, GDN_DV), F32)],
        compiler_params=pltpu.CompilerParams(dimension_semantics=("parallel", "parallel", "arbitrary"),
                                             vmem_limit_bytes=VMEM_LIMIT),
        name="gdn_delta",
    )(qkv, qkv, qkv, proj3, gb, gt, s0, norm_w.reshape(1, GDN_DV).astype(F32))


def gdn_mixer(h2d, b, t, s0, conv_prefix, norm_in, w_in, conv_w, a_log, dt_bias, norm_w, w_out):
    w_main, w_gate = w_in
    proj3 = proj(h2d, w_main, norm_w=norm_in, name="gdn_in").reshape(b, t, -1)
    gate3 = proj(h2d, w_gate, norm_w=norm_in, name="gdn_in_gate").reshape(b, t, -1)
    keep = conv_w.shape[1] - 1
    assert t >= keep
    new_prefix = proj3[:, t - keep:, :GDN_CONV_DIM]
    qkv = gdn_conv(proj3, conv_prefix, conv_w)
    tl = min(t, GDN_CHUNK)
    t_pad = -(-t // GDN_CHUNK) * GDN_CHUNK
    gb = gdn_gates(jnp.pad(gate3, ((0, 0), (0, t_pad - t), (0, 0))), a_log, dt_bias, t)
    nc = t_pad // GDN_CHUNK
    gt = gb[:, :, GDN_V_HEADS:2 * GDN_V_HEADS].transpose(0, 2, 1).reshape(b, GDN_V_HEADS, nc, GDN_CHUNK)
    o, s_new = gdn_delta(qkv, proj3, gb, gt, s0.astype(F32), norm_w, tl)
    y = proj(o.reshape(b * t, GDN_VDIM), w_out, res=h2d, name="gdn_out")
    return y, s_new, new_prefix


def _compress_kernel(*refs, paged, n_pages, pps, sp):
    if paged:
        refs = refs[1:]
    n_x = pps * NSA_KV_HEADS
    x_refs = refs[:n_x]
    w1_ref, pe_ref, w2_ref, nw_ref, o_ref, xs_ref, acc_ref = refs[n_x:]
    p = pl.program_id(2)
    s_all = n_pages * sp
    for u in range(pps):
        row0 = pl.multiple_of((p * pps + u) * sp, SUBLANE)
        for r in range(CMP_STRIDE):
            for g in range(NSA_KV_HEADS):
                xs_ref[r, g, pl.ds(row0, sp), :] = x_refs[u * NSA_KV_HEADS + g][pl.ds(r, sp, stride=CMP_STRIDE), :]

    @pl.when(p == n_pages // pps - 1)
    def _():
        is_k = pl.program_id(1) == 0
        acc_ref[s_all:s_all + SUBLANE, :] = jnp.zeros((SUBLANE, 2 * CMP_HIDDEN), F32)
        for g in range(NSA_KV_HEADS):
            acc_ref[0:s_all, :] = jnp.zeros((s_all, 2 * CMP_HIDDEN), F32)

            def body(r, carry):
                acc_ref[0:s_all, :] += jnp.dot(xs_ref[r, g].astype(BF16), w1_ref[r], preferred_element_type=F32)
                return carry

            lax.fori_loop(0, CMP_STRIDE, body, 0)
            hid = acc_ref[0:s_all, 0:CMP_HIDDEN] + acc_ref[1:s_all + 1, CMP_HIDDEN:2 * CMP_HIDDEN] + pe_ref[...]
            act = jax.nn.gelu(hid)
            y = jnp.dot(act.astype(BF16), w2_ref[...], preferred_element_type=F32)
            ms = jnp.mean(y * y, axis=-1, keepdims=True)
            normed = y * lax.rsqrt(ms + RMS_EPS) * nw_ref[...]
            o_ref[g] = jnp.where(is_k, normed, y)


def nsa_compress(x, page_table, col0, w1r, pe_hid, w2, k_norm0):
    paged = page_table is not None
    rows = x.shape[1]
    sp = rows // CMP_STRIDE
    if paged:
        b, n_pages = page_table.shape
    else:
        b, n_pages = x.shape[0], 1
    pps = math.gcd(n_pages, PAGES_PER_STEP)
    s_all = n_pages * sp

    def x_spec(u, g):
        def col(c):
            return (col0 + c) * NSA_KV_HEADS + g
        if paged:
            return pl.BlockSpec((None, rows, NSA_HD), lambda i, c, p, pt: (pt[i, p * pps + u], 0, col(c)))
        return pl.BlockSpec((None, rows, NSA_HD), lambda i, c, p: (i, 0, col(c)))

    def fixed(shape, imap):
        if paged:
            return pl.BlockSpec(shape, lambda i, c, p, pt: imap(i, c, p))
        return pl.BlockSpec(shape, imap)

    in_specs = [x_spec(u, g) for u in range(pps) for g in range(NSA_KV_HEADS)] + [
        fixed((None, CMP_STRIDE, NSA_HD, 2 * CMP_HIDDEN), lambda i, c, p: (c, 0, 0, 0)),
        fixed((None, 1, CMP_HIDDEN), lambda i, c, p: (c, 0, 0)),
        fixed((None, CMP_HIDDEN, NSA_HD), lambda i, c, p: (c, 0, 0)),
        fixed((1, NSA_HD), lambda i, c, p: (0, 0)),
    ]
    out_spec = fixed((None, None, NSA_KV_HEADS, s_all, NSA_HD), lambda i, c, p: (i, c, 0, 0, 0))
    scratch = [pltpu.VMEM((CMP_STRIDE, NSA_KV_HEADS, s_all, NSA_HD), F32),
               pltpu.VMEM((s_all + SUBLANE, 2 * CMP_HIDDEN), F32)]
    kern = functools.partial(_compress_kernel, paged=paged, n_pages=n_pages, pps=pps, sp=sp)
    out_shape = jax.ShapeDtypeStruct((b, 2, NSA_KV_HEADS, s_all, NSA_HD), F32)
    params = pltpu.CompilerParams(dimension_semantics=("parallel", "arbitrary", "arbitrary"),
                                  vmem_limit_bytes=VMEM_LIMIT)
    args = (x,) * (pps * NSA_KV_HEADS) + (w1r, pe_hid, w2, k_norm0.reshape(1, NSA_HD).astype(F32))
    if paged:
        grid_spec = pltpu.PrefetchScalarGridSpec(num_scalar_prefetch=1, grid=(b, 2, n_pages // pps), in_specs=in_specs,
                                                 out_specs=out_spec, scratch_shapes=scratch)
        return pl.pallas_call(kern, grid_spec=grid_spec, out_shape=out_shape, compiler_params=params,
                              name="nsa_compress_paged")(page_table, *args)
    return pl.pallas_call(kern, grid=(b, 2, n_pages // pps), in_specs=in_specs, out_specs=out_spec, out_shape=out_shape,
                          scratch_shapes=scratch, compiler_params=params, name="nsa_compress")(*args)


def _stack_heads(qb):
    return jnp.concatenate([qb[:, j * LANE:(j + 1) * LANE] for j in range(NSA_GROUP)], axis=0)


def _cmp_attn_kernel(q_ref, kc_ref, vc_ref, cb_ref, gate_ref, ovl_ref, o_ref, sel_ref, *, tq, q0, n_sb):
    i = pl.program_id(2)
    s_all = kc_ref.shape[0]
    nsbp = sel_ref.shape[-1]
    q4 = _stack_heads(q_ref[...]).astype(BF16)
    s = lax.dot_general(q4, kc_ref[...].astype(BF16), NT_DIMS, preferred_element_type=F32)
    bias = cb_ref[...].reshape(NSA_GROUP * tq, s_all)
    s = s + bias
    ok = bias > 0.5 * NEG_INF
    m = jnp.max(s, axis=-1, keepdims=True)
    e = jnp.exp(s - m)
    p = jnp.where(ok, e / jnp.sum(e, axis=-1, keepdims=True), 0.0)
    o4 = jnp.dot(p.astype(BF16), vc_ref[...].astype(BF16), preferred_element_type=F32)
    for j in range(NSA_GROUP):
        o_ref[:, j * LANE:(j + 1) * LANE] = o4[j * tq:(j + 1) * tq] * gate_ref[:, j:j + 1]

    psum = p[0:tq] + p[tq:2 * tq] + p[2 * tq:3 * tq] + p[3 * tq:4 * tq]
    ovl = ovl_ref[...]
    p_hi = psum.astype(BF16)
    r1 = psum - p_hi.astype(F32)
    p_mid = r1.astype(BF16)
    p_lo = (r1 - p_mid.astype(F32)).astype(BF16)
    imp = (jnp.dot(p_hi, ovl, preferred_element_type=F32) + jnp.dot(p_mid, ovl, preferred_element_type=F32)
           + jnp.dot(p_lo, ovl, preferred_element_type=F32))

    lane = lax.broadcasted_iota(jnp.int32, (tq, nsbp), 1)
    qpos = q0 + i * tq + lax.broadcasted_iota(jnp.int32, (tq, nsbp), 0)
    cur = qpos >> SEL_SHIFT
    forced = (lane == 0) | (lane == cur) | (lane == cur - 1)
    sb_ok = (lane << SEL_SHIFT) <= qpos
    score = jnp.where(sb_ok, imp + jnp.where(forced, FORCE_BONUS, 0.0), NEG_INF)
    work = jnp.where(lane < n_sb, score, -jnp.inf)
    lane_f = lane.astype(F32)
    selneg = jnp.full((tq, nsbp), NEG_INF, F32)
    for _ in range(N_SEL):
        mx = jnp.max(work, axis=-1, keepdims=True)
        first = jnp.min(jnp.where(work == mx, lane_f, 1e9), axis=-1, keepdims=True)
        hit = lane_f == first
        selneg = jnp.where(hit & (mx > 0.5 * NEG_INF), 0.0, selneg)
        work = jnp.where(hit, -jnp.inf, work)
    sel_ref[...] = selneg


def nsa_cmp_attention(proj3, kvc, cmp_bias, gates_r, ovl, tq, q0, n_sb):
    b, t, _ = proj3.shape
    s_all = kvc.shape[3]
    nsbp = ovl.shape[1]
    kern = functools.partial(_cmp_attn_kernel, tq=tq, q0=q0, n_sb=n_sb)
    return pl.pallas_call(
        kern,
        grid=(b, NSA_KV_HEADS, t // tq),
        in_specs=[
            pl.BlockSpec((None, tq, NSA_GROUP * NSA_HD), lambda i, g, q: (i, q, g)),
            pl.BlockSpec((None, None, None, s_all, NSA_HD), lambda i, g, q: (i, 0, g, 0, 0)),
            pl.BlockSpec((None, None, None, s_all, NSA_HD), lambda i, g, q: (i, 1, g, 0, 0)),
            pl.BlockSpec((NSA_GROUP, tq, s_all), lambda i, g, q: (g, q, 0)),
            pl.BlockSpec((None, None, tq, 3 * NSA_GROUP), lambda i, g, q: (i, g, q, 0)),
            pl.BlockSpec((s_all, nsbp), lambda i, g, q: (0, 0)),
        ],
        out_specs=[
            pl.BlockSpec((None, tq, NSA_GROUP * NSA_HD), lambda i, g, q: (i, q, g)),
            pl.BlockSpec((None, None, tq, nsbp), lambda i, g, q: (i, g, q, 0)),
        ],
        out_shape=[jax.ShapeDtypeStruct((b, t, NSA_QDIM), F32),
                   jax.ShapeDtypeStruct((b, NSA_KV_HEADS, t, nsbp), F32)],
        compiler_params=pltpu.CompilerParams(dimension_semantics=("parallel", "parallel", "parallel"),
                                             vmem_limit_bytes=VMEM_LIMIT),
        name="nsa_cmp_attn",
    )(proj3, kvc, kvc, cmp_bias, gates_r, ovl)


def _softmax_tile_update(s, v, m_prev, l_prev, acc_prev):
    m_new = jnp.maximum(m_prev, jnp.max(s, axis=-1, keepdims=True))
    alpha = jnp.exp(m_prev - m_new)
    p = jnp.exp(s - m_new)
    l_new = alpha * l_prev + jnp.sum(p, axis=-1, keepdims=True)
    acc_new = alpha * acc_prev + jnp.dot(p.astype(BF16), v, preferred_element_type=F32)
    return m_new, l_new, acc_new


def _tile_scores(q4, k, bias3, sel, tq, tk, qpos0, kpos0, window):
    s = lax.dot_general(q4, k, NT_DIMS, preferred_element_type=F32).reshape(NSA_GROUP, tq, tk) + bias3
    kpos = kpos0 + lax.broadcasted_iota(jnp.int32, (tq, tk), 1)
    qpos = qpos0 + lax.broadcasted_iota(jnp.int32, (tq, tk), 0)
    dist = qpos - kpos
    mask = dist >= 0
    if window is not None:
        mask = mask & (dist < window)
    if sel is not None:
        nsbp = sel.shape[1]
        blk = (kpos0 + lax.broadcasted_iota(jnp.int32, (nsbp, tk), 1)) >> SEL_SHIFT
        onehot = jnp.where(blk == lax.broadcasted_iota(jnp.int32, (nsbp, tk), 0), 1.0, 0.0).astype(BF16)
        s = s + jnp.dot(sel.astype(BF16), onehot, preferred_element_type=F32)[None]
    return jnp.where(mask[None], s, NEG_INF).reshape(NSA_GROUP * tq, tk)


def _flash_prompt_kernel(*refs, tq, tk, nkk, use_sel, window, gcol):
    if use_sel:
        q_ref, k_ref, v_ref, b_ref, gate_ref, sel_ref, o_ref, q4_ref, m_ref, l_ref, acc_ref = refs
    else:
        q_ref, k_ref, v_ref, b_ref, gate_ref, o_ref, q4_ref, m_ref, l_ref, acc_ref = refs
        sel_ref = None
    i = pl.program_id(2)
    jj = pl.program_id(3)
    if use_sel:
        j, valid = jj, jj <= i
    else:
        j = i - (nkk - 1) + jj
        valid = j >= 0

    @pl.when(jj == 0)
    def _():
        q4_ref[...] = _stack_heads(q_ref[...]).astype(BF16)
        m_ref[...] = jnp.full(m_ref.shape, NEG_INF, F32)
        l_ref[...] = jnp.zeros(l_ref.shape, F32)
        acc_ref[...] = jnp.zeros(acc_ref.shape, F32)

    @pl.when(valid)
    def _():
        sel = sel_ref[...] if use_sel else None
        s = _tile_scores(q4_ref[...], k_ref[...].astype(BF16), b_ref[...], sel, tq, tk, i * tq, j * tk, window)
        m_new, l_new, acc_new = _softmax_tile_update(s, v_ref[...].astype(BF16), m_ref[...], l_ref[...],
                                                     acc_ref[...])
        m_ref[...] = m_new
        l_ref[...] = l_new
        acc_ref[...] = acc_new

    @pl.when(jj == nkk - 1)
    def _():
        o4 = acc_ref[...] / l_ref[...]
        for h in range(NSA_GROUP):
            o_ref[:, h * LANE:(h + 1) * LANE] = o4[h * tq:(h + 1) * tq] * gate_ref[:, gcol + h:gcol + h + 1]


def nsa_flash_prompt(proj3, bias_tab, gates_r, sel, *, tile, kcol, vcol, window, gcol, name):
    b, t, _ = proj3.shape
    nq = t // tile
    n_dd = bias_tab.shape[1]
    use_sel = sel is not None
    nkk = nq if use_sel else (window + tile - 1) // tile + 1

    def kidx(q, jj):
        return jnp.minimum(jj, q) if use_sel else jnp.maximum(q - (nkk - 1) + jj, 0)

    in_specs = [
        pl.BlockSpec((None, tile, NSA_GROUP * NSA_HD), lambda i, g, q, jj: (i, q, g)),
        pl.BlockSpec((None, tile, NSA_HD), lambda i, g, q, jj: (i, kidx(q, jj), kcol + g)),
        pl.BlockSpec((None, tile, NSA_HD), lambda i, g, q, jj: (i, kidx(q, jj), vcol + g)),
        pl.BlockSpec((NSA_GROUP, None, tile, tile),
                     lambda i, g, q, jj: (g, jnp.minimum(q - kidx(q, jj), n_dd - 1), 0, 0)),
        pl.BlockSpec((None, None, tile, 3 * NSA_GROUP), lambda i, g, q, jj: (i, g, q, 0)),
    ]
    args = [proj3, proj3, proj3, bias_tab, gates_r]
    if use_sel:
        nsbp = sel.shape[-1]
        in_specs.append(pl.BlockSpec((None, None, tile, nsbp), lambda i, g, q, jj: (i, g, q, 0)))
        args.append(sel)
    kern = functools.partial(_flash_prompt_kernel, tq=tile, tk=tile, nkk=nkk, use_sel=use_sel, window=window,
                             gcol=gcol)
    return pl.pallas_call(
        kern,
        grid=(b, NSA_KV_HEADS, nq, nkk),
        in_specs=in_specs,
        out_specs=pl.BlockSpec((None, tile, NSA_GROUP * NSA_HD), lambda i, g, q, jj: (i, q, g)),
        out_shape=jax.ShapeDtypeStruct((b, t, NSA_QDIM), F32),
        scratch_shapes=[pltpu.VMEM((NSA_GROUP * tile, NSA_HD), BF16),
                        pltpu.VMEM((NSA_GROUP * tile, 1), F32),
                        pltpu.VMEM((NSA_GROUP * tile, 1), F32),
                        pltpu.VMEM((NSA_GROUP * tile, NSA_HD), F32)],
        compiler_params=pltpu.CompilerParams(
            dimension_semantics=("parallel", "parallel", "parallel", "arbitrary"), vmem_limit_bytes=VMEM_LIMIT),
        name=name,
    )(*args)


def _flash_sample_kernel(*refs, tq, pps, n_steps, kbase, q0, use_sel, window, gcol):
    q_ref = refs[1]
    k_refs = refs[2:2 + pps]
    v_refs = refs[2 + pps:2 + 2 * pps]
    rest = refs[2 + 2 * pps:]
    if use_sel:
        kn_ref, vn_ref, b_ref, bn_ref, gate_ref, sel_ref, o_ref, q4_ref, m_ref, l_ref, acc_ref = rest
    else:
        kn_ref, vn_ref, b_ref, bn_ref, gate_ref, o_ref, q4_ref, m_ref, l_ref, acc_ref = rest
        sel_ref = None
    p = pl.program_id(1)
    page = k_refs[0].shape[0]
    tk = pps * page

    @pl.when(p == 0)
    def _():
        for g in range(NSA_KV_HEADS):
            q4_ref[g] = _stack_heads(q_ref[:, g * NSA_GROUP * LANE:(g + 1) * NSA_GROUP * LANE]).astype(BF16)
        m_ref[...] = jnp.full(m_ref.shape, NEG_INF, F32)
        l_ref[...] = jnp.zeros(l_ref.shape, F32)
        acc_ref[...] = jnp.zeros(acc_ref.shape, F32)

    def update(g, k, v, bias3, kpos0):
        sel = sel_ref[g] if use_sel else None
        s = _tile_scores(q4_ref[g], k, bias3, sel, tq, k.shape[0], q0, kpos0, window)
        m_new, l_new, acc_new = _softmax_tile_update(s, v, m_ref[g], l_ref[g], acc_ref[g])
        m_ref[g] = m_new
        l_ref[g] = l_new
        acc_ref[g] = acc_new

    for g in range(NSA_KV_HEADS):
        cols = slice(g * LANE, (g + 1) * LANE)
        k = jnp.concatenate([r[:, cols].astype(BF16) for r in k_refs], axis=0)
        v = jnp.concatenate([r[:, cols].astype(BF16) for r in v_refs], axis=0)
        update(g, k, v, b_ref[g * NSA_GROUP:(g + 1) * NSA_GROUP], kbase + p * tk)

    @pl.when(p == n_steps - 1)
    def _():
        pad = jnp.zeros((page - tq, LANE), F32)
        for g in range(NSA_KV_HEADS):
            cols = slice(g * LANE, (g + 1) * LANE)
            kn = jnp.concatenate([kn_ref[:, cols], pad], axis=0).astype(BF16)
            vn = jnp.concatenate([vn_ref[:, cols], pad], axis=0).astype(BF16)
            update(g, kn, vn, bn_ref[g * NSA_GROUP:(g + 1) * NSA_GROUP, :, 0:page], q0)
            o4 = acc_ref[g] / l_ref[g]
            for j in range(NSA_GROUP):
                h = g * NSA_GROUP + j
                o_ref[:, h * LANE:(h + 1) * LANE] = o4[j * tq:(j + 1) * tq] * gate_ref[g, :, gcol + j:gcol + j + 1]


def nsa_flash_sample(proj3, pool, table, bias_tab, gates_r, sel, *, kcol, vcol, newk, newv, kbase, q0, window,
                     gcol, name):
    b, tq, _ = proj3.shape
    page = pool.shape[1]
    n_dd, tk = bias_tab.shape[1], bias_tab.shape[3]
    pps = tk // page
    n_steps = table.shape[1] // pps
    qtile = (q0 - kbase) // tk
    use_sel = sel is not None

    def page_spec(u, col):
        return pl.BlockSpec((None, page, NSA_KVDIM), lambda i, p, pt: (pt[i, p * pps + u], 0, col))

    in_specs = [pl.BlockSpec((None, tq, NSA_QDIM), lambda i, p, pt: (i, 0, 0))]
    in_specs += [page_spec(u, kcol) for u in range(pps)]
    in_specs += [page_spec(u, vcol) for u in range(pps)]
    in_specs += [
        pl.BlockSpec((None, tq, NSA_KVDIM), lambda i, p, pt: (i, 0, newk)),
        pl.BlockSpec((None, tq, NSA_KVDIM), lambda i, p, pt: (i, 0, newv)),
        pl.BlockSpec((NSA_HEADS, None, tq, tk), lambda i, p, pt: (0, jnp.clip(qtile - p, 0, n_dd - 1), 0, 0)),
        pl.BlockSpec((NSA_HEADS, None, tq, tk), lambda i, p, pt: (0, 0, 0, 0)),
        pl.BlockSpec((None, NSA_KV_HEADS, tq, 3 * NSA_GROUP), lambda i, p, pt: (i, 0, 0, 0)),
    ]
    args = [proj3] + [pool] * (2 * pps) + [proj3, proj3, bias_tab, bias_tab, gates_r]
    if use_sel:
        nsbp = sel.shape[-1]
        in_specs.append(pl.BlockSpec((None, NSA_KV_HEADS, tq, nsbp), lambda i, p, pt: (i, 0, 0, 0)))
        args.append(sel)
    kern = functools.partial(_flash_sample_kernel, tq=tq, pps=pps, n_steps=n_steps, kbase=kbase, q0=q0,
                             use_sel=use_sel, window=window, gcol=gcol)
    rows4 = NSA_GROUP * tq
    grid_spec = pltpu.PrefetchScalarGridSpec(
        num_scalar_prefetch=1, grid=(b, n_steps), in_specs=in_specs,
        out_specs=pl.BlockSpec((None, tq, NSA_QDIM), lambda i, p, pt: (i, 0, 0)),
        scratch_shapes=[pltpu.VMEM((NSA_KV_HEADS, rows4, NSA_HD), BF16),
                        pltpu.VMEM((NSA_KV_HEADS, rows4, 1), F32),
                        pltpu.VMEM((NSA_KV_HEADS, rows4, 1), F32),
                        pltpu.VMEM((NSA_KV_HEADS, rows4, NSA_HD), F32)])
    return pl.pallas_call(
        kern, grid_spec=grid_spec, out_shape=jax.ShapeDtypeStruct((b, tq, NSA_QDIM), F32),
        compiler_params=pltpu.CompilerParams(dimension_semantics=("parallel", "arbitrary"),
                                             vmem_limit_bytes=VMEM_LIMIT),
        name=name,
    )(table, *args)


def _bias_table(rel_bias, unit, tq, tk):
    n_dd = -(-(T5_SATURATION + tk - 1) // unit) + 1
    dist = (jnp.arange(n_dd)[:, None, None] * unit + jnp.arange(tq)[None, :, None]
            - jnp.arange(tk)[None, None, :])
    return _bucket_lookup(rel_bias, dist)


def _bucket_lookup(rel_bias, dist):
    onehot = jax.nn.one_hot(t5_bucket(dist), N_BUCKETS, dtype=F32)
    out = jnp.einsum('...b,bh->h...', onehot, rel_bias.astype(F32), precision=lax.Precision.HIGHEST)
    return out


def _cmp_tables(rel_bias, q0, t, s_all, nc, n_sb, nsbp):
    c = jnp.arange(s_all)
    c_end = c * CMP_STRIDE + (CMP_BLOCK - 1)
    dist = (q0 + jnp.arange(t))[:, None] - c_end[None, :]
    ok = (dist >= 0) & (c < nc)[None, :]
    bias = jnp.where(ok[None], _bucket_lookup(rel_bias, dist), NEG_INF)
    sb_start = jnp.arange(nsbp) * SEL_BLOCK
    c_start = c * CMP_STRIDE
    ovl = jnp.maximum(jnp.minimum(c_end[:, None], sb_start[None, :] + SEL_BLOCK - 1)
                      - jnp.maximum(c_start[:, None], sb_start[None, :]) + 1, 0).astype(F32) / CMP_BLOCK
    ovl = jnp.where((c < nc)[:, None] & (jnp.arange(nsbp) < n_sb)[None, :], ovl, 0.0)
    return bias, ovl.astype(BF16)


def _nsa_weights(w_in, q_norm, k_norm, cmp_pe, cmp_w1, cmp_w2, rel_bias, w_out):
    n_main = NSA_QDIM + 6 * NSA_KVDIM
    nsub = CMP_BLOCK // CMP_STRIDE
    w1r = cmp_w1.reshape(2, nsub, CMP_STRIDE, NSA_HD, CMP_HIDDEN)
    w1r = jnp.moveaxis(w1r, 1, 3).reshape(2, CMP_STRIDE, NSA_HD, nsub * CMP_HIDDEN).astype(BF16)
    pe_hid = jnp.einsum('ck,cke->ce', cmp_pe.reshape(2, -1), cmp_w1, precision=lax.Precision.HIGHEST)
    return dict(
        w_main=w_in[:, :n_main].astype(BF16),
        w_gate=_pad_cols(w_in[:, n_main:], LANE).astype(BF16),
        post=_nsa_in_post(q_norm, k_norm),
        w1r=w1r, pe_hid=pe_hid.reshape(2, 1, CMP_HIDDEN), w2=cmp_w2.astype(BF16), k_norm0=k_norm[0],
        rel_bias=rel_bias, w_out=w_out.astype(BF16))


def nsa_mixer(h2d, b, t, norm_in, nw, paged):
    post_main, post_gate = nw["post"]
    proj2 = proj(h2d, nw["w_main"], norm_w=norm_in, post=post_main, name="nsa_in")
    gate_out = proj(h2d, nw["w_gate"], norm_w=norm_in, post=post_gate, name="nsa_in_gate")
    proj3 = proj2.reshape(b, t, -1)
    gates_r = gate_out[:, :3 * NSA_HEADS].reshape(b, t, 3, NSA_KV_HEADS, NSA_GROUP)
    gates_r = gates_r.transpose(0, 3, 1, 2, 4).reshape(b, NSA_KV_HEADS, t, 3 * NSA_GROUP)
    kv_blk0 = NSA_QDIM // NSA_KVDIM
    kv_col0 = NSA_QDIM // NSA_HD
    rel_bias = nw["rel_bias"]
    if paged is None:
        p_len, tq = 0, 256
        kvc = nsa_compress(proj3, None, kv_blk0, nw["w1r"], nw["pe_hid"], nw["w2"], nw["k_norm0"])
    else:
        pool, page_table, win_state = paged
        p_len, tq = page_table.shape[1] * PAGE_SIZE, t
        kvc = nsa_compress(pool, page_table, 0, nw["w1r"], nw["pe_hid"], nw["w2"], nw["k_norm0"])
    n_all = p_len + t
    s_all = kvc.shape[3]
    nc = n_all // CMP_STRIDE - CMP_BLOCK // CMP_STRIDE + 1
    n_sb = -(-n_all // SEL_BLOCK)
    nsbp = -(-n_sb // LANE) * LANE
    cmp_bias, ovl = _cmp_tables(rel_bias, p_len, t, s_all, nc, n_sb, nsbp)
    o_cmp, sel = nsa_cmp_attention(proj3, kvc, cmp_bias, gates_r, ovl, tq, p_len, n_sb)
    if paged is None:
        tab = _bias_table(rel_bias, tq, tq, tq)
        o_sel = nsa_flash_prompt(proj3, tab, gates_r, sel, tile=tq, kcol=kv_col0 + 2 * NSA_KV_HEADS,
                                 vcol=kv_col0 + 3 * NSA_KV_HEADS, window=None, gcol=NSA_GROUP, name="nsa_sel")
        o_win = nsa_flash_prompt(proj3, tab, gates_r, None, tile=tq, kcol=kv_col0 + 4 * NSA_KV_HEADS,
                                 vcol=kv_col0 + 5 * NSA_KV_HEADS, window=WINDOW, gcol=2 * NSA_GROUP, name="nsa_win")
    else:
        tk = PAGES_PER_STEP * PAGE_SIZE
        assert WINDOW % tk == 0 and p_len % tk == 0
        tab = _bias_table(rel_bias, tk, t, tk)
        o_sel = nsa_flash_sample(proj3, pool, page_table, tab, gates_r, sel, kcol=2, vcol=3, newk=kv_blk0 + 2,
                                 newv=kv_blk0 + 3, kbase=0, q0=p_len, window=None, gcol=NSA_GROUP,
                                 name="nsa_sel_paged")
        n_wt = WINDOW // PAGE_SIZE
        win_pool = win_state.reshape(b * n_wt, PAGE_SIZE, 2 * NSA_KVDIM)
        win_table = jnp.arange(b * n_wt, dtype=jnp.int32).reshape(b, n_wt)
        o_win = nsa_flash_sample(proj3, win_pool, win_table, tab, gates_r, None, kcol=0, vcol=1, newk=kv_blk0 + 4,
                                 newv=kv_blk0 + 5, kbase=p_len - WINDOW, q0=p_len, window=WINDOW,
                                 gcol=2 * NSA_GROUP, name="nsa_win_paged")
    y = proj((o_cmp.reshape(b * t, NSA_QDIM), o_sel.reshape(b * t, NSA_QDIM), o_win.reshape(b * t, NSA_QDIM)),
             nw["w_out"], res=h2d, name="nsa_out")
    new_rows = proj3[:, :, NSA_QDIM:NSA_QDIM + 4 * NSA_KVDIM].reshape(b, t, 4, NSA_KV_HEADS, NSA_HD)
    new_win = proj3[:, :, NSA_QDIM + 4 * NSA_KVDIM:].reshape(b, t, 2, NSA_KV_HEADS, NSA_HD)
    return y, new_rows, new_win


def _pad_cols(w, n_pad):
    return jnp.pad(w, ((0, 0), (0, n_pad - w.shape[1])))


def _nsa_in_post(q_norm, k_norm):
    ones_kv = jnp.ones((NSA_KVDIM,), F32)
    zeros_kv = jnp.zeros((NSA_KVDIM,), F32)
    pw = jnp.concatenate([
        jnp.tile(q_norm.astype(F32) * (NSA_HD ** -0.5), NSA_HEADS),
        ones_kv, ones_kv, jnp.tile(k_norm[1].astype(F32), NSA_KV_HEADS), ones_kv,
        jnp.tile(k_norm[2].astype(F32), NSA_KV_HEADS), ones_kv])
    pm = jnp.concatenate([
        jnp.ones((NSA_QDIM,), F32),
        zeros_kv, zeros_kv, ones_kv, zeros_kv, ones_kv, zeros_kv])
    return (pw, pm), (jnp.ones((LANE,), F32), jnp.full((LANE,), 2.0, F32))


def kernel(x_prompt, x_sample, state_gdn, state_gdn_conv, cache_nsa_kv, state_nsa_win, state_ffn_conv, page_table,
           norm_mix, norm_ffn, gdn_w_in, gdn_conv_w, gdn_A_log, gdn_dt_bias, gdn_norm, gdn_w_out,
           nsa_w_in, nsa_q_norm, nsa_k_norm, nsa_cmp_pe, nsa_cmp_w1, nsa_cmp_w2, rel_bias, nsa_w_out,
           ffn_w_up, ffn_conv_w, ffn_conv_b, ffn_w_down):
    depth = norm_mix.shape[0]
    bp, tp, d = x_prompt.shape
    bs, ts, _ = x_sample.shape
    win_buf = state_nsa_win.shape[2]
    assert win_buf == WINDOW and tp >= WINDOW and cache_nsa_kv.shape[2] == PAGE_SIZE
    d_ff = ffn_conv_w.shape[1]
    conv_keep = ffn_conv_w.shape[2] - 1
    hp = x_prompt.reshape(bp * tp, d)
    hs = x_sample.reshape(bs * ts, d)
    gdn_p, gdnc_p, kv_p, win_p, ffn_p = [], [], [], [], []
    gdn_s, gdnc_s, kv_s, win_s, ffn_s = [], [], [], [], []
    for i in range(depth):
        j = i // 2
        if i % 2 == 0:
            n_main = GDN_CONV_DIM + GDN_VDIM
            w_in = (gdn_w_in[j][:, :n_main].astype(BF16), _pad_cols(gdn_w_in[j][:, n_main:], LANE).astype(BF16))
            gw = (norm_mix[i], w_in, gdn_conv_w[j], gdn_A_log[j], gdn_dt_bias[j], gdn_norm[j],
                  gdn_w_out[j].astype(BF16))
            hp, st_p, cv_p = gdn_mixer(hp, bp, tp, jnp.zeros((bp, GDN_V_HEADS, GDN_DK, GDN_DV), F32),
                                       jnp.zeros((bp, gdn_conv_w.shape[2] - 1, GDN_CONV_DIM), F32), *gw)
            hs, st_s, cv_s = gdn_mixer(hs, bs, ts, state_gdn[j], state_gdn_conv[j], *gw)
            gdn_p.append(st_p)
            gdnc_p.append(cv_p)
            gdn_s.append(st_s)
            gdnc_s.append(cv_s)
        else:
            nw = _nsa_weights(nsa_w_in[j], nsa_q_norm[j], nsa_k_norm[j], nsa_cmp_pe[j], nsa_cmp_w1[j],
                              nsa_cmp_w2[j], rel_bias, nsa_w_out[j])
            hp, rows_p, nwin_p = nsa_mixer(hp, bp, tp, norm_mix[i], nw, None)
            n_pool = cache_nsa_kv.shape[1]
            pool = cache_nsa_kv.reshape(cache_nsa_kv.shape[0] * n_pool, PAGE_SIZE, 4 * NSA_KVDIM)
            win_state = state_nsa_win[j].reshape(bs, win_buf, 2 * NSA_KVDIM)
            hs, rows_s, nwin_s = nsa_mixer(hs, bs, ts, norm_mix[i], nw, (pool, page_table + j * n_pool, win_state))
            kv_p.append(rows_p)
            win_p.append(nwin_p[:, tp - WINDOW:])
            kv_s.append(rows_s)
            win_s.append(jnp.concatenate([state_nsa_win[j][:, ts:], nwin_s], axis=1))
        n_up = -(-ffn_w_up.shape[2] // 512) * 512
        fw = (norm_ffn[i], _pad_cols(ffn_w_up[i], n_up).astype(BF16), ffn_conv_w[i], ffn_conv_b[i], ffn_w_down[i].astype(BF16))
        hp, cp = conv_ffn(hp, bp, tp, jnp.zeros((bp, conv_keep, d_ff), F32), *fw)
        hs, cs = conv_ffn(hs, bs, ts, state_ffn_conv[i], *fw)
        ffn_p.append(cp)
        ffn_s.append(cs)
    return (hp.reshape(bp, tp, d), hs.reshape(bs, ts, d),
            jnp.stack(gdn_p), jnp.stack(gdnc_p), jnp.stack(kv_p), jnp.stack(win_p), jnp.stack(ffn_p),
            jnp.stack(gdn_s), jnp.stack(gdnc_s), jnp.stack(kv_s), jnp.stack(win_s), jnp.stack(ffn_s))
```

```python
import functools
import math

import jax
import jax.numpy as jnp
from jax import lax
from jax.experimental import pallas as pl
from jax.experimental.pallas import tpu as pltpu

F32 = jnp.float32
BF16 = jnp.bfloat16
RMS_EPS = 1e-6
NEG_INF = -1e30

LANE = 128
SUBLANE = 8
VMEM_LIMIT = 56 * 1024 * 1024
PROJ_VMEM_BUDGET = 40 * 1024 * 1024
FFN_DOWN_ROWS = 256

D_MODEL = 2048
GDN_K_HEADS = 16
GDN_V_HEADS = 32
GDN_DK = 128
GDN_DV = 128
GDN_KDIM = GDN_K_HEADS * GDN_DK
GDN_VDIM = GDN_V_HEADS * GDN_DV
GDN_CONV_DIM = 2 * GDN_KDIM + GDN_VDIM
GDN_CHUNK = 64

NSA_HEADS = 16
NSA_KV_HEADS = 4
NSA_HD = 128
NSA_GROUP = NSA_HEADS // NSA_KV_HEADS
NSA_QDIM = NSA_HEADS * NSA_HD
NSA_KVDIM = NSA_KV_HEADS * NSA_HD
CMP_BLOCK = 32
CMP_STRIDE = 16
CMP_HIDDEN = 2 * NSA_HD
SEL_BLOCK = 64
SEL_SHIFT = 6
N_SEL = 16
WINDOW = 512
FORCE_BONUS = 1e3
N_BUCKETS = 32
REL_MAX_DIST = 1024
PAGE_SIZE = 128
PAGES_PER_STEP = 4
T5_SATURATION = 790

NT_DIMS = (((1,), (1,)), ((), ()))


def _proj_kernel(*refs, n_x, has_norm, has_res, has_post):
    it = iter(refs)
    x_refs = [next(it) for _ in range(n_x)]
    nw_ref = next(it) if has_norm else None
    w_ref = next(it)
    res_ref = next(it) if has_res else None
    pw_ref = next(it) if has_post else None
    pm_ref = next(it) if has_post else None
    o_ref = next(it)
    xs_ref = next(it)

    @pl.when(pl.program_id(1) == 0)
    def _():
        x = x_refs[0][...].astype(F32)
        for r in x_refs[1:]:
            x = x + r[...].astype(F32)
        if has_norm:
            ms = jnp.mean(x * x, axis=-1, keepdims=True)
            x = x * lax.rsqrt(ms + RMS_EPS) * nw_ref[...]
        xs_ref[...] = x.astype(BF16)

    y = jnp.dot(xs_ref[...], w_ref[...], preferred_element_type=F32)
    if has_res:
        y = y + res_ref[...]
    if has_post:
        tn = y.shape[1]
        for g in range(tn // LANE):
            sl = slice(g * LANE, (g + 1) * LANE)
            yg = y[:, sl]
            mode = pm_ref[:, sl]
            ms = jnp.mean(yg * yg, axis=-1, keepdims=True)
            normed = yg * lax.rsqrt(ms + RMS_EPS) * pw_ref[:, sl]
            sig = jax.nn.sigmoid(yg)
            o_ref[:, sl] = jnp.where(mode == 1.0, normed, jnp.where(mode == 2.0, sig, yg))
    else:
        o_ref[...] = y.astype(o_ref.dtype)


def _pick_tile(n, candidates):
    for c in candidates:
        if n % c == 0:
            return c
    return n


def proj(xs, w, *, norm_w=None, res=None, post=None, out_dtype=F32, name="proj"):
    if not isinstance(xs, (tuple, list)):
        xs = (xs,)
    m, k = xs[0].shape
    n = w.shape[1]
    tn = _pick_tile(n, (512, 384, 256, 128))
    x_row_bytes = sum(k * x.dtype.itemsize for x in xs)

    def vmem_bytes(tm):
        return 2 * (tm * x_row_bytes + k * tn * 2 + tm * tn * 4 * (2 if res is not None else 1)) + tm * k * 2

    tm = next((c for c in (1024, 512, 256) if m % c == 0 and vmem_bytes(c) <= PROJ_VMEM_BUDGET), m)
    in_specs = [pl.BlockSpec((tm, k), lambda i, j: (i, 0)) for _ in xs]
    args = list(xs)
    if norm_w is not None:
        in_specs.append(pl.BlockSpec((1, k), lambda i, j: (0, 0)))
        args.append(norm_w.reshape(1, k).astype(F32))
    in_specs.append(pl.BlockSpec((k, tn), lambda i, j: (0, j)))
    args.append(w)
    if res is not None:
        in_specs.append(pl.BlockSpec((tm, tn), lambda i, j: (i, j)))
        args.append(res)
    if post is not None:
        for a in post:
            in_specs.append(pl.BlockSpec((1, tn), lambda i, j: (0, j)))
            args.append(a.reshape(1, n).astype(F32))
    kern = functools.partial(_proj_kernel, n_x=len(xs), has_norm=norm_w is not None, has_res=res is not None,
                             has_post=post is not None)
    return pl.pallas_call(
        kern,
        grid=(m // tm, n // tn),
        in_specs=in_specs,
        out_specs=pl.BlockSpec((tm, tn), lambda i, j: (i, j)),
        out_shape=jax.ShapeDtypeStruct((m, n), out_dtype),
        scratch_shapes=[pltpu.VMEM((tm, k), BF16)],
        compiler_params=pltpu.CompilerParams(dimension_semantics=("parallel", "arbitrary"),
                                             vmem_limit_bytes=VMEM_LIMIT),
        name=name,
    )(*args)


def _conv_gate(pad_ref, prev, gate, val, cw, cb, width):
    t = gate.shape[0]
    halo = SUBLANE
    pad_ref[halo - (width - 1):halo, :] = prev
    pad_ref[halo:halo + t, :] = gate
    acc = cb + pad_ref[halo - (width - 1):halo - (width - 1) + t, :] * cw[0:1, :]
    for i in range(1, width):
        off = halo - (width - 1) + i
        acc = acc + pad_ref[off:off + t, :] * cw[i:i + 1, :]
    return jax.nn.silu(acc) * val


def _ffn_gate_kernel(gate_ref, val_ref, pre_ref, cw_ref, cb_ref, o_ref, pad_ref, *, width):
    for i in range(gate_ref.shape[0]):
        hid = _conv_gate(pad_ref, pre_ref[i], gate_ref[i], val_ref[i], cw_ref[...], cb_ref[...], width)
        o_ref[i] = hid.astype(o_ref.dtype)


def ffn_gate(up, prefix, conv_w, conv_b, d_ff):
    b, t, _ = up.shape
    width = conv_w.shape[1]
    nblk = d_ff // LANE
    kern = functools.partial(_ffn_gate_kernel, width=width)
    return pl.pallas_call(
        kern,
        grid=(nblk,),
        in_specs=[
            pl.BlockSpec((b, t, LANE), lambda j: (0, 0, j)),
            pl.BlockSpec((b, t, LANE), lambda j: (0, 0, j + nblk)),
            pl.BlockSpec((b, width - 1, LANE), lambda j: (0, 0, j)),
            pl.BlockSpec((width, LANE), lambda j: (0, j)),
            pl.BlockSpec((1, LANE), lambda j: (0, j)),
        ],
        out_specs=pl.BlockSpec((b, t, LANE), lambda j: (0, 0, j)),
        out_shape=jax.ShapeDtypeStruct((b, t, d_ff), BF16),
        scratch_shapes=[pltpu.VMEM((t + SUBLANE, LANE), F32)],
        compiler_params=pltpu.CompilerParams(dimension_semantics=("parallel",)),
        name="ffn_gate",
    )(up, up, prefix, conv_w.T, conv_b.reshape(1, d_ff))


def _ffn_down_kernel(gate_ref, val_ref, halo_ref, pre_ref, cw_ref, cb_ref, w_ref, res_ref, o_ref, hid_ref, pad_ref, *,
                     width, tiles_per_batch):
    @pl.when(pl.program_id(1) == 0)
    def _():
        first = (pl.program_id(0) % tiles_per_batch) == 0

        def column_block(c, carry):
            cs = pl.ds(pl.multiple_of(c * LANE, LANE), LANE)
            prev = jnp.where(first, pre_ref[:, cs], halo_ref[SUBLANE - (width - 1):SUBLANE, cs])
            hid = _conv_gate(pad_ref, prev, gate_ref[:, cs], val_ref[:, cs], cw_ref[:, cs], cb_ref[:, cs], width)
            hid_ref[:, cs] = hid.astype(BF16)
            return carry

        lax.fori_loop(0, gate_ref.shape[1] // LANE, column_block, 0)

    o_ref[...] = jnp.dot(hid_ref[...], w_ref[...], preferred_element_type=F32) + res_ref[...]


def ffn_down_fused(up2d, b, t, prefix, conv_w, conv_b, w_down, res):
    d_ff, n = w_down.shape
    width = conv_w.shape[1]
    tm, tn = FFN_DOWN_ROWS, 512
    assert t % tm == 0 and n % tn == 0
    tiles_per_batch = t // tm
    kern = functools.partial(_ffn_down_kernel, width=width, tiles_per_batch=tiles_per_batch)
    return pl.pallas_call(
        kern,
        grid=(b * tiles_per_batch, n // tn),
        in_specs=[
            pl.BlockSpec((tm, d_ff), lambda i, j: (i, 0)),
            pl.BlockSpec((tm, d_ff), lambda i, j: (i, 1)),
            pl.BlockSpec((SUBLANE, d_ff), lambda i, j: (jnp.maximum(i * (tm // SUBLANE) - 1, 0), 0)),
            pl.BlockSpec((None, width - 1, d_ff), lambda i, j: (i // tiles_per_batch, 0, 0)),
            pl.BlockSpec((width, d_ff), lambda i, j: (0, 0)),
            pl.BlockSpec((1, d_ff), lambda i, j: (0, 0)),
            pl.BlockSpec((d_ff, tn), lambda i, j: (0, j)),
            pl.BlockSpec((tm, tn), lambda i, j: (i, j)),
        ],
        out_specs=pl.BlockSpec((tm, tn), lambda i, j: (i, j)),
        out_shape=jax.ShapeDtypeStruct((b * t, n), F32),
        scratch_shapes=[pltpu.VMEM((tm, d_ff), BF16), pltpu.VMEM((tm + SUBLANE, LANE), F32)],
        compiler_params=pltpu.CompilerParams(dimension_semantics=("parallel", "arbitrary"),
                                             vmem_limit_bytes=VMEM_LIMIT),
        name="ffn_down_fused",
    )(up2d, up2d, up2d, prefix, conv_w.T, conv_b.reshape(1, d_ff), w_down, res)


def conv_ffn(h2d, b, t, prefix, norm_w, w_up, conv_w, conv_b, w_down):
    d_ff = conv_w.shape[0]
    keep = conv_w.shape[1] - 1
    assert t >= keep
    up2d = proj(h2d, w_up, norm_w=norm_w, name="ffn_up")
    up = up2d.reshape(b, t, -1)
    new_prefix = up[:, t - keep:, :d_ff]
    if t % FFN_DOWN_ROWS == 0:
        return ffn_down_fused(up2d, b, t, prefix, conv_w, conv_b, w_down, h2d), new_prefix
    hidden = ffn_gate(up, prefix, conv_w, conv_b, d_ff)
    out = proj(hidden.reshape(b * t, d_ff), w_down, res=h2d, name="ffn_down")
    return out, new_prefix


def t5_bucket(dist):
    d = jnp.maximum(dist, 0)
    max_exact = N_BUCKETS // 2
    scale = (N_BUCKETS - max_exact) / math.log(REL_MAX_DIST / max_exact)
    large = max_exact + (jnp.log(jnp.maximum(d, 1).astype(F32) / max_exact) * scale).astype(jnp.int32)
    return jnp.where(d < max_exact, d, jnp.minimum(large, N_BUCKETS - 1))


GDN_CONV_COLS = 512
GDN_HEADS_PER_STEP = 8
GDN_GROUP = 4


def _gdn_conv_kernel(x_ref, pre_ref, cw_ref, o_ref, pad_ref, *, width, n_q_blocks, n_qk_blocks):
    j = pl.program_id(1)
    t = x_ref.shape[0]
    halo = SUBLANE
    pad_ref[0:halo, :] = jnp.zeros((halo, pad_ref.shape[1]), F32)
    pad_ref[halo - (width - 1):halo, :] = pre_ref[...]
    pad_ref[halo:halo + t, :] = x_ref[...]
    acc = pad_ref[halo - (width - 1):halo - (width - 1) + t, :] * cw_ref[0:1, :]
    for i in range(1, width):
        off = halo - (width - 1) + i
        acc = acc + pad_ref[off:off + t, :] * cw_ref[i:i + 1, :]
    y = jax.nn.silu(acc)
    scale = jnp.where(j < n_q_blocks, GDN_DK ** -0.5, 1.0)
    is_qk = j < n_qk_blocks
    for h in range(x_ref.shape[1] // LANE):
        seg = y[:, h * LANE:(h + 1) * LANE]
        nrm = seg * lax.rsqrt(jnp.sum(seg * seg, axis=-1, keepdims=True) + RMS_EPS) * scale
        o_ref[:, h * LANE:(h + 1) * LANE] = jnp.where(is_qk, nrm, seg)


def gdn_conv(proj3, prefix, conv_w):
    b, t, _ = proj3.shape
    width = conv_w.shape[1]
    cols = GDN_CONV_COLS
    kern = functools.partial(_gdn_conv_kernel, width=width, n_q_blocks=GDN_KDIM // cols,
                             n_qk_blocks=2 * GDN_KDIM // cols)
    return pl.pallas_call(
        kern,
        grid=(b, GDN_CONV_DIM // cols),
        in_specs=[
            pl.BlockSpec((None, t, cols), lambda i, j: (i, 0, j)),
            pl.BlockSpec((None, width - 1, cols), lambda i, j: (i, 0, j)),
            pl.BlockSpec((width, cols), lambda i, j: (0, j)),
        ],
        out_specs=pl.BlockSpec((None, t, cols), lambda i, j: (i, 0, j)),
        out_shape=jax.ShapeDtypeStruct((b, t, GDN_CONV_DIM), F32),
        scratch_shapes=[pltpu.VMEM((t + SUBLANE, cols), F32)],
        compiler_params=pltpu.CompilerParams(dimension_semantics=("parallel", "parallel"),
                                             vmem_limit_bytes=VMEM_LIMIT),
        name="gdn_conv",
    )(proj3, prefix, conv_w.T)


def _split_bf16(x):
    hi = x.astype(BF16)
    return hi, (x - hi.astype(F32)).astype(BF16)


def _dot_split(a_parts, b_parts):
    (ah, al), (bh, bl) = a_parts, b_parts
    return (jnp.dot(ah, bh, preferred_element_type=F32) + jnp.dot(ah, bl, preferred_element_type=F32)
            + jnp.dot(al, bh, preferred_element_type=F32))


def _gdn_gate_kernel(x_ref, a_ref, dt_ref, o_ref, *, t_real):
    L = GDN_CHUNK
    lane = lax.broadcasted_iota(jnp.int32, (L, LANE), 1)
    row = lax.broadcasted_iota(jnp.int32, (L, LANE), 0)
    ci = lax.broadcasted_iota(jnp.int32, (L, L), 0)
    cj = lax.broadcasted_iota(jnp.int32, (L, L), 1)
    tril = jnp.where(ci >= cj, 1.0, 0.0).astype(BF16)
    neg_a = -jnp.exp(a_ref[...])

    def chunk(c, carry):
        r0 = pl.multiple_of(c * L, L)
        x = x_ref[pl.ds(r0, L), :]
        live = (row + r0) < t_real
        beta = jnp.where(live, jax.nn.sigmoid(x), 0.0)
        z = x + dt_ref[...]
        softplus = jnp.maximum(z, 0.0) + jnp.log(1.0 + jnp.exp(-jnp.abs(z)))
        g = jnp.where(live, neg_a * softplus, 0.0)
        hi, mid = _split_bf16(g)
        lo = (g - hi.astype(F32) - mid.astype(F32)).astype(BF16)
        cum = (jnp.dot(tril, hi, preferred_element_type=F32) + jnp.dot(tril, mid, preferred_element_type=F32)
               + jnp.dot(tril, lo, preferred_element_type=F32))
        o_ref[pl.ds(r0, L), :] = jnp.where(lane < GDN_V_HEADS, beta, cum)
        return carry

    lax.fori_loop(0, x_ref.shape[0] // L, chunk, 0)


def gdn_gates(gate3, a_log, dt_bias, t_real):
    b, t, _ = gate3.shape
    pad = jnp.zeros((GDN_V_HEADS,), F32)
    a_vec = jnp.concatenate([pad, a_log.astype(F32), pad, pad]).reshape(1, LANE)
    dt_vec = jnp.concatenate([pad, dt_bias.astype(F32), pad, pad]).reshape(1, LANE)
    return pl.pallas_call(
        functools.partial(_gdn_gate_kernel, t_real=t_real),
        grid=(b,),
        in_specs=[pl.BlockSpec((None, t, LANE), lambda i: (i, 0, 0)),
                  pl.BlockSpec((1, LANE), lambda i: (0, 0)),
                  pl.BlockSpec((1, LANE), lambda i: (0, 0))],
        out_specs=pl.BlockSpec((None, t, LANE), lambda i: (i, 0, 0)),
        out_shape=jax.ShapeDtypeStruct((b, t, LANE), F32),
        compiler_params=pltpu.CompilerParams(dimension_semantics=("parallel",)),
        name="gdn_gates",
    )(gate3, a_vec, dt_vec)


def _pad_rows(x, rows):
    if x.shape[0] == rows:
        return x
    return jnp.concatenate([x, jnp.zeros((rows - x.shape[0], x.shape[1]), x.dtype)], axis=0)


def _gdn_delta_kernel(q_ref, k_ref, v_ref, z_ref, gb_ref, gt_ref, s0_ref, nw_ref, o_ref, s_out_ref, s_ref, *, nc, tl):
    hg = pl.program_id(1)
    c = pl.program_id(2)
    L = GDN_CHUNK
    hb = GDN_HEADS_PER_STEP

    @pl.when(c == 0)
    def _():
        s_ref[...] = s0_ref[...]

    gs = GDN_GROUP
    n_groups = hb // gs
    rows = gs * L
    shift = L.bit_length() - 1
    gb = gb_ref[...]
    lane = lax.broadcasted_iota(jnp.int32, (L, LANE), 1)
    ri = lax.broadcasted_iota(jnp.int32, (rows, rows), 0)
    cj = lax.broadcasted_iota(jnp.int32, (rows, rows), 1)
    same = (ri >> shift) == (cj >> shift)
    strict = same & (ri > cj)
    incl = same & (ri >= cj)

    def stack(ref, width_of):
        return [jnp.concatenate([_pad_rows(ref[:, width_of(hh) * LANE:(width_of(hh) + 1) * LANE], L)
                                 for hh in range(g * gs, (g + 1) * gs)], axis=0) for g in range(n_groups)]

    q = stack(q_ref, lambda hh: hh // 2)
    k = stack(k_ref, lambda hh: hh // 2)
    v = stack(v_ref, lambda hh: hh)
    bcol, gcol, grow, eg, a_qk, pw, x = [], [], [], [], [], [], []
    for g in range(n_groups):
        heads = [hg * hb + g * gs + hh for hh in range(gs)]
        bcol.append(jnp.concatenate(
            [jnp.sum(jnp.where(lane == h, gb, 0.0), axis=-1, keepdims=True) for h in heads], axis=0))
        gcol.append(jnp.concatenate(
            [jnp.sum(jnp.where(lane == h + GDN_V_HEADS, gb, 0.0), axis=-1, keepdims=True) for h in heads], axis=0))
        grow.append(gt_ref[pl.ds(c, 1), g * rows:(g + 1) * rows])
        decay = jnp.exp(jnp.where(same, gcol[g] - grow[g], 0.0))
        kb = k[g].astype(BF16)
        kk = lax.dot_general(kb, kb, NT_DIMS, preferred_element_type=F32)
        qk = lax.dot_general(q[g].astype(BF16), kb, NT_DIMS, preferred_element_type=F32)
        a_qk.append((qk * jnp.where(incl, decay, 0.0)).astype(BF16))
        eg.append(jnp.exp(gcol[g]))
        pw.append(_split_bf16(-(bcol[g] * kk * jnp.where(strict, decay, 0.0))))
        x.append(jnp.concatenate([bcol[g] * v[g], (bcol[g] * eg[g]) * k[g]], axis=1))
    for i in range(shift):
        for g in range(n_groups):
            x[g] = x[g] + _dot_split(pw[g], _split_bf16(x[g]))
        if i + 1 < shift:
            for g in range(n_groups):
                pw[g] = _split_bf16(_dot_split(pw[g], pw[g]))
    for g in range(n_groups):
        u_eff, w_kb = x[g][:, :GDN_DV], x[g][:, GDN_DV:].astype(BF16)
        sbs = [s_ref[g * gs + hh].astype(BF16) for hh in range(gs)]
        u = jnp.concatenate(
            [u_eff[hh * L:(hh + 1) * L] - jnp.dot(w_kb[hh * L:(hh + 1) * L], sbs[hh], preferred_element_type=F32)
             for hh in range(gs)], axis=0)
        ub = u.astype(BF16)
        o_intra = jnp.dot(a_qk[g], ub, preferred_element_type=F32)
        q_dec = (q[g] * eg[g]).astype(BF16)
        for hh in range(gs):
            hr = slice(hh * L, (hh + 1) * L)
            hi = g * gs + hh
            o = jnp.dot(q_dec[hr], sbs[hh], preferred_element_type=F32) + o_intra[hr]
            g_last = grow[g][:, hh * L + L - 1:hh * L + L]
            k_dec = (k[g][hr] * jnp.exp(g_last - gcol[g][hr])).astype(BF16)
            s_ref[hi] = jnp.exp(g_last) * s_ref[hi] + lax.dot_general(k_dec, ub[hr], (((0,), (0,)), ((), ())),
                                                                      preferred_element_type=F32)
            on = o * lax.rsqrt(jnp.mean(o * o, axis=-1, keepdims=True) + RMS_EPS) * nw_ref[...]
            gated = on[:tl] * jax.nn.silu(z_ref[:, hi * LANE:(hi + 1) * LANE])
            o_ref[:, hi * LANE:(hi + 1) * LANE] = gated.astype(o_ref.dtype)

    @pl.when(c == nc - 1)
    def _():
        s_out_ref[...] = s_ref[...]


def gdn_delta(qkv, proj3, gb, gt, s0, norm_w, tl):
    b, t, _ = qkv.shape
    nc = gb.shape[1] // GDN_CHUNK
    hb = GDN_HEADS_PER_STEP
    kw = (hb // 2) * GDN_DK
    vw = hb * GDN_DV
    kern = functools.partial(_gdn_delta_kernel, nc=nc, tl=tl)
    return pl.pallas_call(
        kern,
        grid=(b, GDN_V_HEADS // hb, nc),
        in_specs=[
            pl.BlockSpec((None, tl, kw), lambda i, h, c: (i, c, h)),
            pl.BlockSpec((None, tl, kw), lambda i, h, c: (i, c, GDN_KDIM // kw + h)),
            pl.BlockSpec((None, tl, vw), lambda i, h, c: (i, c, 2 * GDN_KDIM // vw + h)),
            pl.BlockSpec((None, tl, vw), lambda i, h, c: (i, c, GDN_CONV_DIM // vw + h)),
            pl.BlockSpec((None, GDN_CHUNK, LANE), lambda i, h, c: (i, c, 0)),
            pl.BlockSpec((None, None, nc, hb * GDN_CHUNK), lambda i, h, c: (i, h, 0, 0)),
            pl.BlockSpec((None, hb, GDN_DK, GDN_DV), lambda i, h, c: (i, h, 0, 0)),
            pl.BlockSpec((1, GDN_DV), lambda i, h, c: (0, 0)),
        ],
        out_specs=[
            pl.BlockSpec((None, tl, vw), lambda i, h, c: (i, c, h)),
            pl.BlockSpec((None, hb, GDN_DK, GDN_DV), lambda i, h, c: (i, h, 0, 0)),
        ],
        out_shape=[jax.ShapeDtypeStruct((b, t, GDN_VDIM), BF16),
                   jax.ShapeDtypeStruct(s0.shape, F32)],
        scratch_shapes=[pltpu.VMEM((hb, GDN_DK, GDN_DV), F32)],
        compiler_params=pltpu.CompilerParams(dimension_semantics=("parallel", "parallel", "arbitrary"),
                                             vmem_limit_bytes=VMEM_LIMIT),
        name="gdn_delta",
    )(qkv, qkv, qkv, proj3, gb, gt, s0, norm_w.reshape(1, GDN_DV).astype(F32))


def gdn_mixer(h2d, b, t, s0, conv_prefix, norm_in, w_in, conv_w, a_log, dt_bias, norm_w, w_out):
    w_main, w_gate = w_in
    proj3 = proj(h2d, w_main, norm_w=norm_in, name="gdn_in").reshape(b, t, -1)
    gate3 = proj(h2d, w_gate, norm_w=norm_in, name="gdn_in_gate").reshape(b, t, -1)
    keep = conv_w.shape[1] - 1
    assert t >= keep
    new_prefix = proj3[:, t - keep:, :GDN_CONV_DIM]
    qkv = gdn_conv(proj3, conv_prefix, conv_w)
    tl = min(t, GDN_CHUNK)
    t_pad = -(-t // GDN_CHUNK) * GDN_CHUNK
    gb = gdn_gates(jnp.pad(gate3, ((0, 0), (0, t_pad - t), (0, 0))), a_log, dt_bias, t)
    nc = t_pad // GDN_CHUNK
    hb = GDN_HEADS_PER_STEP
    gt = gb[:, :, GDN_V_HEADS:2 * GDN_V_HEADS].reshape(b, nc, GDN_CHUNK, GDN_V_HEADS // hb, hb)
    gt = gt.transpose(0, 3, 1, 4, 2).reshape(b, GDN_V_HEADS // hb, nc, hb * GDN_CHUNK)
    o, s_new = gdn_delta(qkv, proj3, gb, gt, s0.astype(F32), norm_w, tl)
    y = proj(o.reshape(b * t, GDN_VDIM), w_out, res=h2d, name="gdn_out")
    return y, s_new, new_prefix


def _compress_mlp(get_x, w1_at, pe, w2, k_norm, acc_ref, s_all):
    acc_ref[0:s_all, :] = jnp.zeros((s_all, 2 * CMP_HIDDEN), F32)

    def body(r, carry):
        acc_ref[0:s_all, :] += jnp.dot(get_x(r), w1_at(r), preferred_element_type=F32)
        return carry

    lax.fori_loop(0, CMP_STRIDE, body, 0)
    hid = acc_ref[0:s_all, 0:CMP_HIDDEN] + acc_ref[1:s_all + 1, CMP_HIDDEN:2 * CMP_HIDDEN] + pe
    y = jnp.dot(jax.nn.gelu(hid).astype(BF16), w2, preferred_element_type=F32)
    if k_norm is None:
        return y
    return y * lax.rsqrt(jnp.mean(y * y, axis=-1, keepdims=True) + RMS_EPS) * k_norm


def _compress_kernel(*refs, s_all):
    x_refs = refs[:NSA_KV_HEADS]
    w1_ref, pe_ref, w2_ref, nw_ref, o_ref, xs_ref, acc_ref = refs[NSA_KV_HEADS:]
    for r in range(CMP_STRIDE):
        for g in range(NSA_KV_HEADS):
            xs_ref[r, g] = x_refs[g][pl.ds(r, s_all, stride=CMP_STRIDE), :].astype(BF16)
    is_k = pl.program_id(1) == 0
    acc_ref[s_all:s_all + SUBLANE, :] = jnp.zeros((SUBLANE, 2 * CMP_HIDDEN), F32)
    for g in range(NSA_KV_HEADS):
        y = _compress_mlp(lambda r: xs_ref[r, g], lambda r: w1_ref[r], pe_ref[...], w2_ref[...], None, acc_ref,
                          s_all)
        normed = y * lax.rsqrt(jnp.mean(y * y, axis=-1, keepdims=True) + RMS_EPS) * nw_ref[...]
        o_ref[g] = jnp.where(is_k, normed, y)


def nsa_compress(x, col0, w1r, pe_hid, w2, k_norm0):
    b, rows, _ = x.shape
    s_all = rows // CMP_STRIDE

    def x_spec(g):
        return pl.BlockSpec((None, rows, NSA_HD), lambda i, c: (i, 0, (col0 + c) * NSA_KV_HEADS + g))

    in_specs = [x_spec(g) for g in range(NSA_KV_HEADS)] + [
        pl.BlockSpec((None, CMP_STRIDE, NSA_HD, 2 * CMP_HIDDEN), lambda i, c: (c, 0, 0, 0)),
        pl.BlockSpec((None, 1, CMP_HIDDEN), lambda i, c: (c, 0, 0)),
        pl.BlockSpec((None, CMP_HIDDEN, NSA_HD), lambda i, c: (c, 0, 0)),
        pl.BlockSpec((1, NSA_HD), lambda i, c: (0, 0)),
    ]
    return pl.pallas_call(
        functools.partial(_compress_kernel, s_all=s_all),
        grid=(b, 2),
        in_specs=in_specs,
        out_specs=pl.BlockSpec((None, None, NSA_KV_HEADS, s_all, NSA_HD), lambda i, c: (i, c, 0, 0, 0)),
        out_shape=jax.ShapeDtypeStruct((b, 2, NSA_KV_HEADS, s_all, NSA_HD), F32),
        scratch_shapes=[pltpu.VMEM((CMP_STRIDE, NSA_KV_HEADS, s_all, NSA_HD), BF16),
                        pltpu.VMEM((s_all + SUBLANE, 2 * CMP_HIDDEN), F32)],
        compiler_params=pltpu.CompilerParams(dimension_semantics=("parallel", "parallel"),
                                             vmem_limit_bytes=VMEM_LIMIT),
        name="nsa_compress",
    )(*((x,) * NSA_KV_HEADS), w1r, pe_hid, w2, k_norm0.reshape(1, NSA_HD).astype(F32))


def _compress_paged_kernel(*refs, n_steps, pps):
    x_refs = refs[1:1 + pps]
    w1_ref, pe_ref, w2_ref, nw_ref, o_ref, xs_ref, acc_ref = refs[1 + pps:]
    p = pl.program_id(1)
    sp = PAGE_SIZE // CMP_STRIDE
    s_all = n_steps * pps * sp
    for u in range(0, pps, 2):
        row0 = pl.multiple_of((p * pps + u) * sp, 2 * sp)
        for r in range(CMP_STRIDE):
            for j in range(2 * NSA_KV_HEADS):
                pair = [x_refs[u + d][pl.ds(r, sp, stride=CMP_STRIDE), j, :] for d in range(2)]
                xs_ref[r, j, pl.ds(row0, 2 * sp), :] = jnp.concatenate(pair, axis=0).astype(BF16)

    @pl.when(p == n_steps - 1)
    def _():
        acc_ref[s_all:s_all + SUBLANE, :] = jnp.zeros((SUBLANE, 2 * CMP_HIDDEN), F32)
        for c in range(2):
            for g in range(NSA_KV_HEADS):
                j = c * NSA_KV_HEADS + g
                o_ref[c, g] = _compress_mlp(lambda r: xs_ref[r, j], lambda r: w1_ref[c, r], pe_ref[c], w2_ref[c],
                                            nw_ref[...] if c == 0 else None, acc_ref, s_all)


def nsa_compress_paged(pool4, page_table, w1r, pe_hid, w2, k_norm0):
    b, n_pages = page_table.shape
    pps = PAGES_PER_STEP
    assert n_pages % pps == 0 and pps % 2 == 0
    n_steps = n_pages // pps
    s_all = n_pages * (PAGE_SIZE // CMP_STRIDE)
    planes = 2 * NSA_KV_HEADS

    def page_spec(u):
        return pl.BlockSpec((None, PAGE_SIZE, planes, NSA_HD), lambda i, p, pt: (pt[i, p * pps + u], 0, 0, 0))

    in_specs = [page_spec(u) for u in range(pps)] + [
        pl.BlockSpec((2, CMP_STRIDE, NSA_HD, 2 * CMP_HIDDEN), lambda i, p, pt: (0, 0, 0, 0)),
        pl.BlockSpec((2, 1, CMP_HIDDEN), lambda i, p, pt: (0, 0, 0)),
        pl.BlockSpec((2, CMP_HIDDEN, NSA_HD), lambda i, p, pt: (0, 0, 0)),
        pl.BlockSpec((1, NSA_HD), lambda i, p, pt: (0, 0)),
    ]
    grid_spec = pltpu.PrefetchScalarGridSpec(
        num_scalar_prefetch=1, grid=(b, n_steps), in_specs=in_specs,
        out_specs=pl.BlockSpec((None, 2, NSA_KV_HEADS, s_all, NSA_HD), lambda i, p, pt: (i, 0, 0, 0, 0)),
        scratch_shapes=[pltpu.VMEM((CMP_STRIDE, planes, s_all, NSA_HD), BF16),
                        pltpu.VMEM((s_all + SUBLANE, 2 * CMP_HIDDEN), F32)])
    return pl.pallas_call(
        functools.partial(_compress_paged_kernel, n_steps=n_steps, pps=pps),
        grid_spec=grid_spec,
        out_shape=jax.ShapeDtypeStruct((b, 2, NSA_KV_HEADS, s_all, NSA_HD), F32),
        compiler_params=pltpu.CompilerParams(dimension_semantics=("parallel", "arbitrary"),
                                             vmem_limit_bytes=VMEM_LIMIT),
        name="nsa_compress_paged",
    )(page_table, *((pool4,) * pps), w1r, pe_hid, w2, k_norm0.reshape(1, NSA_HD).astype(F32))


def _stack_heads(qb):
    return jnp.concatenate([qb[:, j * LANE:(j + 1) * LANE] for j in range(NSA_GROUP)], axis=0)


def _cmp_attn_kernel(q_ref, kc_ref, vc_ref, cb_ref, gate_ref, ovl_ref, o_ref, sel_ref, *, tq, q0, n_sb):
    i = pl.program_id(2)
    s_all = kc_ref.shape[0]
    nsbp = sel_ref.shape[-1]
    q4 = _stack_heads(q_ref[...]).astype(BF16)
    s = lax.dot_general(q4, kc_ref[...].astype(BF16), NT_DIMS, preferred_element_type=F32)
    bias = cb_ref[...].reshape(NSA_GROUP * tq, s_all)
    s = s + bias
    ok = bias > 0.5 * NEG_INF
    m = jnp.max(s, axis=-1, keepdims=True)
    e = jnp.exp(s - m)
    p = jnp.where(ok, e / jnp.sum(e, axis=-1, keepdims=True), 0.0)
    o4 = jnp.dot(p.astype(BF16), vc_ref[...].astype(BF16), preferred_element_type=F32)
    for j in range(NSA_GROUP):
        o_ref[:, j * LANE:(j + 1) * LANE] = o4[j * tq:(j + 1) * tq] * gate_ref[:, j:j + 1]

    psum = p[0:tq] + p[tq:2 * tq] + p[2 * tq:3 * tq] + p[3 * tq:4 * tq]
    ovl = ovl_ref[...]
    p_hi = psum.astype(BF16)
    r1 = psum - p_hi.astype(F32)
    p_mid = r1.astype(BF16)
    p_lo = (r1 - p_mid.astype(F32)).astype(BF16)
    imp = (jnp.dot(p_hi, ovl, preferred_element_type=F32) + jnp.dot(p_mid, ovl, preferred_element_type=F32)
           + jnp.dot(p_lo, ovl, preferred_element_type=F32))

    lane = lax.broadcasted_iota(jnp.int32, (tq, nsbp), 1)
    qpos = q0 + i * tq + lax.broadcasted_iota(jnp.int32, (tq, nsbp), 0)
    cur = qpos >> SEL_SHIFT
    forced = (lane == 0) | (lane == cur) | (lane == cur - 1)
    sb_ok = (lane << SEL_SHIFT) <= qpos
    score = jnp.where(sb_ok, imp + jnp.where(forced, FORCE_BONUS, 0.0), NEG_INF)
    work = jnp.where(lane < n_sb, score, -jnp.inf)
    lane_f = lane.astype(F32)
    selneg = jnp.full((tq, nsbp), NEG_INF, F32)
    for _ in range(N_SEL):
        mx = jnp.max(work, axis=-1, keepdims=True)
        first = jnp.min(jnp.where(work == mx, lane_f, 1e9), axis=-1, keepdims=True)
        hit = lane_f == first
        selneg = jnp.where(hit & (mx > 0.5 * NEG_INF), 0.0, selneg)
        work = jnp.where(hit, -jnp.inf, work)
    sel_ref[...] = selneg


def nsa_cmp_attention(proj3, kvc, cmp_bias, gates_r, ovl, tq, q0, n_sb):
    b, t, _ = proj3.shape
    s_all = kvc.shape[3]
    nsbp = ovl.shape[1]
    kern = functools.partial(_cmp_attn_kernel, tq=tq, q0=q0, n_sb=n_sb)
    return pl.pallas_call(
        kern,
        grid=(b, NSA_KV_HEADS, t // tq),
        in_specs=[
            pl.BlockSpec((None, tq, NSA_GROUP * NSA_HD), lambda i, g, q: (i, q, g)),
            pl.BlockSpec((None, None, None, s_all, NSA_HD), lambda i, g, q: (i, 0, g, 0, 0)),
            pl.BlockSpec((None, None, None, s_all, NSA_HD), lambda i, g, q: (i, 1, g, 0, 0)),
            pl.BlockSpec((NSA_GROUP, tq, s_all), lambda i, g, q: (g, q, 0)),
            pl.BlockSpec((None, None, tq, 3 * NSA_GROUP), lambda i, g, q: (i, g, q, 0)),
            pl.BlockSpec((s_all, nsbp), lambda i, g, q: (0, 0)),
        ],
        out_specs=[
            pl.BlockSpec((None, tq, NSA_GROUP * NSA_HD), lambda i, g, q: (i, q, g)),
            pl.BlockSpec((None, None, tq, nsbp), lambda i, g, q: (i, g, q, 0)),
        ],
        out_shape=[jax.ShapeDtypeStruct((b, t, NSA_QDIM), F32),
                   jax.ShapeDtypeStruct((b, NSA_KV_HEADS, t, nsbp), F32)],
        compiler_params=pltpu.CompilerParams(dimension_semantics=("parallel", "parallel", "parallel"),
                                             vmem_limit_bytes=VMEM_LIMIT),
        name="nsa_cmp_attn",
    )(proj3, kvc, kvc, cmp_bias, gates_r, ovl)


def _softmax_tile_update(s, v, m_prev, l_prev, acc_prev):
    m_new = jnp.maximum(m_prev, jnp.max(s, axis=-1, keepdims=True))
    alpha = jnp.exp(m_prev - m_new)
    p = jnp.exp(s - m_new)
    l_new = alpha * l_prev + jnp.sum(p, axis=-1, keepdims=True)
    acc_new = alpha * acc_prev + jnp.dot(p.astype(BF16), v, preferred_element_type=F32)
    return m_new, l_new, acc_new


def _tile_scores(q4, k, bias3, sel, tq, tk, qpos0, kpos0, window):
    s = lax.dot_general(q4, k, NT_DIMS, preferred_element_type=F32).reshape(NSA_GROUP, tq, tk) + bias3
    kpos = kpos0 + lax.broadcasted_iota(jnp.int32, (tq, tk), 1)
    qpos = qpos0 + lax.broadcasted_iota(jnp.int32, (tq, tk), 0)
    dist = qpos - kpos
    mask = dist >= 0
    if window is not None:
        mask = mask & (dist < window)
    if sel is not None:
        nsbp = sel.shape[1]
        blk = (kpos0 + lax.broadcasted_iota(jnp.int32, (nsbp, tk), 1)) >> SEL_SHIFT
        onehot = jnp.where(blk == lax.broadcasted_iota(jnp.int32, (nsbp, tk), 0), 1.0, 0.0).astype(BF16)
        s = s + jnp.dot(sel.astype(BF16), onehot, preferred_element_type=F32)[None]
    return jnp.where(mask[None], s, NEG_INF).reshape(NSA_GROUP * tq, tk)


def _flash_prompt_kernel(*refs, tq, tk, nkk, use_sel, window, gcol):
    if use_sel:
        q_ref, k_ref, v_ref, b_ref, gate_ref, sel_ref, o_ref, q4_ref, m_ref, l_ref, acc_ref = refs
    else:
        q_ref, k_ref, v_ref, b_ref, gate_ref, o_ref, q4_ref, m_ref, l_ref, acc_ref = refs
        sel_ref = None
    i = pl.program_id(2)
    jj = pl.program_id(3)
    if use_sel:
        j, valid = jj, jj <= i
    else:
        j = i - (nkk - 1) + jj
        valid = j >= 0

    @pl.when(jj == 0)
    def _():
        q4_ref[...] = _stack_heads(q_ref[...]).astype(BF16)
        m_ref[...] = jnp.full(m_ref.shape, NEG_INF, F32)
        l_ref[...] = jnp.zeros(l_ref.shape, F32)
        acc_ref[...] = jnp.zeros(acc_ref.shape, F32)

    @pl.when(valid)
    def _():
        sel = sel_ref[...] if use_sel else None
        s = _tile_scores(q4_ref[...], k_ref[...].astype(BF16), b_ref[...], sel, tq, tk, i * tq, j * tk, window)
        m_new, l_new, acc_new = _softmax_tile_update(s, v_ref[...].astype(BF16), m_ref[...], l_ref[...],
                                                     acc_ref[...])
        m_ref[...] = m_new
        l_ref[...] = l_new
        acc_ref[...] = acc_new

    @pl.when(jj == nkk - 1)
    def _():
        o4 = acc_ref[...] / l_ref[...]
        for h in range(NSA_GROUP):
            o_ref[:, h * LANE:(h + 1) * LANE] = o4[h * tq:(h + 1) * tq] * gate_ref[:, gcol + h:gcol + h + 1]


def nsa_flash_prompt(proj3, bias_tab, gates_r, sel, *, tile, kcol, vcol, window, gcol, name):
    b, t, _ = proj3.shape
    nq = t // tile
    n_dd = bias_tab.shape[1]
    use_sel = sel is not None
    nkk = nq if use_sel else (window + tile - 1) // tile + 1

    def kidx(q, jj):
        return jnp.minimum(jj, q) if use_sel else jnp.maximum(q - (nkk - 1) + jj, 0)

    in_specs = [
        pl.BlockSpec((None, tile, NSA_GROUP * NSA_HD), lambda i, g, q, jj: (i, q, g)),
        pl.BlockSpec((None, tile, NSA_HD), lambda i, g, q, jj: (i, kidx(q, jj), kcol + g)),
        pl.BlockSpec((None, tile, NSA_HD), lambda i, g, q, jj: (i, kidx(q, jj), vcol + g)),
        pl.BlockSpec((NSA_GROUP, None, tile, tile),
                     lambda i, g, q, jj: (g, jnp.minimum(q - kidx(q, jj), n_dd - 1), 0, 0)),
        pl.BlockSpec((None, None, tile, 3 * NSA_GROUP), lambda i, g, q, jj: (i, g, q, 0)),
    ]
    args = [proj3, proj3, proj3, bias_tab, gates_r]
    if use_sel:
        nsbp = sel.shape[-1]
        in_specs.append(pl.BlockSpec((None, None, tile, nsbp), lambda i, g, q, jj: (i, g, q, 0)))
        args.append(sel)
    kern = functools.partial(_flash_prompt_kernel, tq=tile, tk=tile, nkk=nkk, use_sel=use_sel, window=window,
                             gcol=gcol)
    return pl.pallas_call(
        kern,
        grid=(b, NSA_KV_HEADS, nq, nkk),
        in_specs=in_specs,
        out_specs=pl.BlockSpec((None, tile, NSA_GROUP * NSA_HD), lambda i, g, q, jj: (i, q, g)),
        out_shape=jax.ShapeDtypeStruct((b, t, NSA_QDIM), F32),
        scratch_shapes=[pltpu.VMEM((NSA_GROUP * tile, NSA_HD), BF16),
                        pltpu.VMEM((NSA_GROUP * tile, 1), F32),
                        pltpu.VMEM((NSA_GROUP * tile, 1), F32),
                        pltpu.VMEM((NSA_GROUP * tile, NSA_HD), F32)],
        compiler_params=pltpu.CompilerParams(
            dimension_semantics=("parallel", "parallel", "parallel", "arbitrary"), vmem_limit_bytes=VMEM_LIMIT),
        name=name,
    )(*args)


def _flash_sample_kernel(*refs, tq, pps, n_steps, kbase, q0, use_sel, window, gcol):
    q_ref = refs[1]
    kv_refs = refs[2:2 + pps]
    rest = refs[2 + pps:]
    if use_sel:
        kn_ref, vn_ref, b_ref, bn_ref, gate_ref, sel_ref, o_ref, q4_ref, m_ref, l_ref, acc_ref = rest
    else:
        kn_ref, vn_ref, b_ref, bn_ref, gate_ref, o_ref, q4_ref, m_ref, l_ref, acc_ref = rest
        sel_ref = None
    p = pl.program_id(1)
    page = kv_refs[0].shape[0]
    tk = pps * page

    @pl.when(p == 0)
    def _():
        for g in range(NSA_KV_HEADS):
            q4_ref[g] = _stack_heads(q_ref[:, g * NSA_GROUP * LANE:(g + 1) * NSA_GROUP * LANE]).astype(BF16)
        m_ref[...] = jnp.full(m_ref.shape, NEG_INF, F32)
        l_ref[...] = jnp.zeros(l_ref.shape, F32)
        acc_ref[...] = jnp.zeros(acc_ref.shape, F32)

    def update(g, k, v, bias3, kpos0):
        sel = sel_ref[g] if use_sel else None
        s = _tile_scores(q4_ref[g], k, bias3, sel, tq, k.shape[0], q0, kpos0, window)
        m_new, l_new, acc_new = _softmax_tile_update(s, v, m_ref[g], l_ref[g], acc_ref[g])
        m_ref[g] = m_new
        l_ref[g] = l_new
        acc_ref[g] = acc_new

    for g in range(NSA_KV_HEADS):
        k = jnp.concatenate([r[:, g, :].astype(BF16) for r in kv_refs], axis=0)
        v = jnp.concatenate([r[:, NSA_KV_HEADS + g, :].astype(BF16) for r in kv_refs], axis=0)
        update(g, k, v, b_ref[g * NSA_GROUP:(g + 1) * NSA_GROUP], kbase + p * tk)

    @pl.when(p == n_steps - 1)
    def _():
        pad = jnp.zeros((page - tq, LANE), F32)
        for g in range(NSA_KV_HEADS):
            cols = slice(g * LANE, (g + 1) * LANE)
            kn = jnp.concatenate([kn_ref[:, cols], pad], axis=0).astype(BF16)
            vn = jnp.concatenate([vn_ref[:, cols], pad], axis=0).astype(BF16)
            update(g, kn, vn, bn_ref[g * NSA_GROUP:(g + 1) * NSA_GROUP, :, 0:page], q0)
            o4 = acc_ref[g] / l_ref[g]
            for j in range(NSA_GROUP):
                h = g * NSA_GROUP + j
                o_ref[:, h * LANE:(h + 1) * LANE] = o4[j * tq:(j + 1) * tq] * gate_ref[g, :, gcol + j:gcol + j + 1]


def nsa_flash_sample(proj3, pool4, table, bias_tab, gates_r, sel, *, plane_blk, newk, newv, kbase, q0, window,
                     gcol, name):
    b, tq, _ = proj3.shape
    page = pool4.shape[1]
    n_dd, tk = bias_tab.shape[1], bias_tab.shape[3]
    pps = tk // page
    n_steps = table.shape[1] // pps
    qtile = (q0 - kbase) // tk
    use_sel = sel is not None

    def page_spec(u):
        return pl.BlockSpec((None, page, 2 * NSA_KV_HEADS, NSA_HD),
                            lambda i, p, pt: (pt[i, p * pps + u], 0, plane_blk, 0))

    in_specs = [pl.BlockSpec((None, tq, NSA_QDIM), lambda i, p, pt: (i, 0, 0))]
    in_specs += [page_spec(u) for u in range(pps)]
    in_specs += [
        pl.BlockSpec((None, tq, NSA_KVDIM), lambda i, p, pt: (i, 0, newk)),
        pl.BlockSpec((None, tq, NSA_KVDIM), lambda i, p, pt: (i, 0, newv)),
        pl.BlockSpec((NSA_HEADS, None, tq, tk), lambda i, p, pt: (0, jnp.clip(qtile - p, 0, n_dd - 1), 0, 0)),
        pl.BlockSpec((NSA_HEADS, None, tq, tk), lambda i, p, pt: (0, 0, 0, 0)),
        pl.BlockSpec((None, NSA_KV_HEADS, tq, 3 * NSA_GROUP), lambda i, p, pt: (i, 0, 0, 0)),
    ]
    args = [proj3] + [pool4] * pps + [proj3, proj3, bias_tab, bias_tab, gates_r]
    if use_sel:
        nsbp = sel.shape[-1]
        in_specs.append(pl.BlockSpec((None, NSA_KV_HEADS, tq, nsbp), lambda i, p, pt: (i, 0, 0, 0)))
        args.append(sel)
    kern = functools.partial(_flash_sample_kernel, tq=tq, pps=pps, n_steps=n_steps, kbase=kbase, q0=q0,
                             use_sel=use_sel, window=window, gcol=gcol)
    rows4 = NSA_GROUP * tq
    grid_spec = pltpu.PrefetchScalarGridSpec(
        num_scalar_prefetch=1, grid=(b, n_steps), in_specs=in_specs,
        out_specs=pl.BlockSpec((None, tq, NSA_QDIM), lambda i, p, pt: (i, 0, 0)),
        scratch_shapes=[pltpu.VMEM((NSA_KV_HEADS, rows4, NSA_HD), BF16),
                        pltpu.VMEM((NSA_KV_HEADS, rows4, 1), F32),
                        pltpu.VMEM((NSA_KV_HEADS, rows4, 1), F32),
                        pltpu.VMEM((NSA_KV_HEADS, rows4, NSA_HD), F32)])
    return pl.pallas_call(
        kern, grid_spec=grid_spec, out_shape=jax.ShapeDtypeStruct((b, tq, NSA_QDIM), F32),
        compiler_params=pltpu.CompilerParams(dimension_semantics=("parallel", "arbitrary"),
                                             vmem_limit_bytes=VMEM_LIMIT),
        name=name,
    )(table, *args)


def _bias_table(rel_bias, unit, tq, tk):
    n_dd = -(-(T5_SATURATION + tk - 1) // unit) + 1
    dist = (jnp.arange(n_dd)[:, None, None] * unit + jnp.arange(tq)[None, :, None]
            - jnp.arange(tk)[None, None, :])
    return _bucket_lookup(rel_bias, dist)


def _bucket_lookup(rel_bias, dist):
    onehot = jax.nn.one_hot(t5_bucket(dist), N_BUCKETS, dtype=F32)
    out = jnp.einsum('...b,bh->h...', onehot, rel_bias.astype(F32), precision=lax.Precision.HIGHEST)
    return out


def _cmp_tables(rel_bias, q0, t, s_all, nc, n_sb, nsbp):
    c = jnp.arange(s_all)
    c_end = c * CMP_STRIDE + (CMP_BLOCK - 1)
    dist = (q0 + jnp.arange(t))[:, None] - c_end[None, :]
    ok = (dist >= 0) & (c < nc)[None, :]
    bias = jnp.where(ok[None], _bucket_lookup(rel_bias, dist), NEG_INF)
    sb_start = jnp.arange(nsbp) * SEL_BLOCK
    c_start = c * CMP_STRIDE
    ovl = jnp.maximum(jnp.minimum(c_end[:, None], sb_start[None, :] + SEL_BLOCK - 1)
                      - jnp.maximum(c_start[:, None], sb_start[None, :]) + 1, 0).astype(F32) / CMP_BLOCK
    ovl = jnp.where((c < nc)[:, None] & (jnp.arange(nsbp) < n_sb)[None, :], ovl, 0.0)
    return bias, ovl.astype(BF16)


def _nsa_weights(w_in, q_norm, k_norm, cmp_pe, cmp_w1, cmp_w2, rel_bias, w_out):
    n_main = NSA_QDIM + 6 * NSA_KVDIM
    nsub = CMP_BLOCK // CMP_STRIDE
    w1r = cmp_w1.reshape(2, nsub, CMP_STRIDE, NSA_HD, CMP_HIDDEN)
    w1r = jnp.moveaxis(w1r, 1, 3).reshape(2, CMP_STRIDE, NSA_HD, nsub * CMP_HIDDEN).astype(BF16)
    pe_hid = jnp.einsum('ck,cke->ce', cmp_pe.reshape(2, -1), cmp_w1, precision=lax.Precision.HIGHEST)
    return dict(
        w_main=w_in[:, :n_main].astype(BF16),
        w_gate=_pad_cols(w_in[:, n_main:], LANE).astype(BF16),
        post=_nsa_in_post(q_norm, k_norm),
        w1r=w1r, pe_hid=pe_hid.reshape(2, 1, CMP_HIDDEN), w2=cmp_w2.astype(BF16), k_norm0=k_norm[0],
        rel_bias=rel_bias, w_out=w_out.astype(BF16))


def nsa_mixer(h2d, b, t, norm_in, nw, paged):
    post_main, post_gate = nw["post"]
    proj2 = proj(h2d, nw["w_main"], norm_w=norm_in, post=post_main, name="nsa_in")
    gate_out = proj(h2d, nw["w_gate"], norm_w=norm_in, post=post_gate, name="nsa_in_gate")
    proj3 = proj2.reshape(b, t, -1)
    gates_r = gate_out[:, :3 * NSA_HEADS].reshape(b, t, 3, NSA_KV_HEADS, NSA_GROUP)
    gates_r = gates_r.transpose(0, 3, 1, 2, 4).reshape(b, NSA_KV_HEADS, t, 3 * NSA_GROUP)
    kv_blk0 = NSA_QDIM // NSA_KVDIM
    kv_col0 = NSA_QDIM // NSA_HD
    rel_bias = nw["rel_bias"]
    if paged is None:
        p_len, tq = 0, 256
        kvc = nsa_compress(proj3, kv_blk0, nw["w1r"], nw["pe_hid"], nw["w2"], nw["k_norm0"])
    else:
        pool, page_table, win_pool = paged
        p_len, tq = page_table.shape[1] * PAGE_SIZE, t
        kvc = nsa_compress_paged(pool, page_table, nw["w1r"], nw["pe_hid"], nw["w2"], nw["k_norm0"])
    n_all = p_len + t
    s_all = kvc.shape[3]
    nc = n_all // CMP_STRIDE - CMP_BLOCK // CMP_STRIDE + 1
    n_sb = -(-n_all // SEL_BLOCK)
    nsbp = -(-n_sb // LANE) * LANE
    cmp_bias, ovl = _cmp_tables(rel_bias, p_len, t, s_all, nc, n_sb, nsbp)
    o_cmp, sel = nsa_cmp_attention(proj3, kvc, cmp_bias, gates_r, ovl, tq, p_len, n_sb)
    if paged is None:
        tab = _bias_table(rel_bias, tq, tq, tq)
        o_sel = nsa_flash_prompt(proj3, tab, gates_r, sel, tile=tq, kcol=kv_col0 + 2 * NSA_KV_HEADS,
                                 vcol=kv_col0 + 3 * NSA_KV_HEADS, window=None, gcol=NSA_GROUP, name="nsa_sel")
        o_win = nsa_flash_prompt(proj3, tab, gates_r, None, tile=tq, kcol=kv_col0 + 4 * NSA_KV_HEADS,
                                 vcol=kv_col0 + 5 * NSA_KV_HEADS, window=WINDOW, gcol=2 * NSA_GROUP, name="nsa_win")
    else:
        tk = PAGES_PER_STEP * PAGE_SIZE
        assert WINDOW % tk == 0 and p_len % tk == 0
        tab = _bias_table(rel_bias, tk, t, tk)
        o_sel = nsa_flash_sample(proj3, pool, page_table, tab, gates_r, sel, plane_blk=1, newk=kv_blk0 + 2,
                                 newv=kv_blk0 + 3, kbase=0, q0=p_len, window=None, gcol=NSA_GROUP,
                                 name="nsa_sel_paged")
        n_wt = WINDOW // PAGE_SIZE
        win_table = jnp.arange(b * n_wt, dtype=jnp.int32).reshape(b, n_wt)
        o_win = nsa_flash_sample(proj3, win_pool, win_table, tab, gates_r, None, plane_blk=0, newk=kv_blk0 + 4,
                                 newv=kv_blk0 + 5, kbase=p_len - WINDOW, q0=p_len, window=WINDOW,
                                 gcol=2 * NSA_GROUP, name="nsa_win_paged")
    y = proj((o_cmp.reshape(b * t, NSA_QDIM), o_sel.reshape(b * t, NSA_QDIM), o_win.reshape(b * t, NSA_QDIM)),
             nw["w_out"], res=h2d, name="nsa_out")
    new_rows = proj3[:, :, NSA_QDIM:NSA_QDIM + 4 * NSA_KVDIM].reshape(b, t, 4, NSA_KV_HEADS, NSA_HD)
    new_win = proj3[:, :, NSA_QDIM + 4 * NSA_KVDIM:].reshape(b, t, 2, NSA_KV_HEADS, NSA_HD)
    return y, new_rows, new_win


def _pad_cols(w, n_pad):
    return jnp.pad(w, ((0, 0), (0, n_pad - w.shape[1])))


def _nsa_in_post(q_norm, k_norm):
    ones_kv = jnp.ones((NSA_KVDIM,), F32)
    zeros_kv = jnp.zeros((NSA_KVDIM,), F32)
    pw = jnp.concatenate([
        jnp.tile(q_norm.astype(F32) * (NSA_HD ** -0.5), NSA_HEADS),
        ones_kv, ones_kv, jnp.tile(k_norm[1].astype(F32), NSA_KV_HEADS), ones_kv,
        jnp.tile(k_norm[2].astype(F32), NSA_KV_HEADS), ones_kv])
    pm = jnp.concatenate([
        jnp.ones((NSA_QDIM,), F32),
        zeros_kv, zeros_kv, ones_kv, zeros_kv, ones_kv, zeros_kv])
    return (pw, pm), (jnp.ones((LANE,), F32), jnp.full((LANE,), 2.0, F32))


def kernel(x_prompt, x_sample, state_gdn, state_gdn_conv, cache_nsa_kv, state_nsa_win, state_ffn_conv, page_table,
           norm_mix, norm_ffn, gdn_w_in, gdn_conv_w, gdn_A_log, gdn_dt_bias, gdn_norm, gdn_w_out,
           nsa_w_in, nsa_q_norm, nsa_k_norm, nsa_cmp_pe, nsa_cmp_w1, nsa_cmp_w2, rel_bias, nsa_w_out,
           ffn_w_up, ffn_conv_w, ffn_conv_b, ffn_w_down):
    depth = norm_mix.shape[0]
    bp, tp, d = x_prompt.shape
    bs, ts, _ = x_sample.shape
    win_buf = state_nsa_win.shape[2]
    assert win_buf == WINDOW and tp >= WINDOW and cache_nsa_kv.shape[2] == PAGE_SIZE
    d_ff = ffn_conv_w.shape[1]
    conv_keep = ffn_conv_w.shape[2] - 1
    hp = x_prompt.reshape(bp * tp, d)
    hs = x_sample.reshape(bs * ts, d)
    gdn_p, gdnc_p, kv_p, win_p, ffn_p = [], [], [], [], []
    gdn_s, gdnc_s, kv_s, win_s, ffn_s = [], [], [], [], []
    for i in range(depth):
        j = i // 2
        if i % 2 == 0:
            n_main = GDN_CONV_DIM + GDN_VDIM
            w_in = (gdn_w_in[j][:, :n_main].astype(BF16), _pad_cols(gdn_w_in[j][:, n_main:], LANE).astype(BF16))
            gw = (norm_mix[i], w_in, gdn_conv_w[j], gdn_A_log[j], gdn_dt_bias[j], gdn_norm[j],
                  gdn_w_out[j].astype(BF16))
            hp, st_p, cv_p = gdn_mixer(hp, bp, tp, jnp.zeros((bp, GDN_V_HEADS, GDN_DK, GDN_DV), F32),
                                       jnp.zeros((bp, gdn_conv_w.shape[2] - 1, GDN_CONV_DIM), F32), *gw)
            hs, st_s, cv_s = gdn_mixer(hs, bs, ts, state_gdn[j], state_gdn_conv[j], *gw)
            gdn_p.append(st_p)
            gdnc_p.append(cv_p)
            gdn_s.append(st_s)
            gdnc_s.append(cv_s)
        else:
            nw = _nsa_weights(nsa_w_in[j], nsa_q_norm[j], nsa_k_norm[j], nsa_cmp_pe[j], nsa_cmp_w1[j],
                              nsa_cmp_w2[j], rel_bias, nsa_w_out[j])
            hp, rows_p, nwin_p = nsa_mixer(hp, bp, tp, norm_mix[i], nw, None)
            n_pool = cache_nsa_kv.shape[1]
            pool = cache_nsa_kv.reshape(cache_nsa_kv.shape[0] * n_pool, PAGE_SIZE, 4 * NSA_KV_HEADS, NSA_HD)
            win_pool = state_nsa_win[j].reshape(bs * (win_buf // PAGE_SIZE), PAGE_SIZE, 2 * NSA_KV_HEADS, NSA_HD)
            hs, rows_s, nwin_s = nsa_mixer(hs, bs, ts, norm_mix[i], nw, (pool, page_table + j * n_pool, win_pool))
            kv_p.append(rows_p)
            win_p.append(nwin_p[:, tp - WINDOW:])
            kv_s.append(rows_s)
            win_s.append(jnp.concatenate([state_nsa_win[j][:, ts:], nwin_s], axis=1))
        n_up = -(-ffn_w_up.shape[2] // 512) * 512
        fw = (norm_ffn[i], _pad_cols(ffn_w_up[i], n_up).astype(BF16), ffn_conv_w[i], ffn_conv_b[i], ffn_w_down[i].astype(BF16))
        hp, cp = conv_ffn(hp, bp, tp, jnp.zeros((bp, conv_keep, d_ff), F32), *fw)
        hs, cs = conv_ffn(hs, bs, ts, state_ffn_conv[i], *fw)
        ffn_p.append(cp)
        ffn_s.append(cs)
    return (hp.reshape(bp, tp, d), hs.reshape(bs, ts, d),
            jnp.stack(gdn_p), jnp.stack(gdnc_p), jnp.stack(kv_p), jnp.stack(win_p), jnp.stack(ffn_p),
            jnp.stack(gdn_s), jnp.stack(gdnc_s), jnp.stack(kv_s), jnp.stack(win_s), jnp.stack(ffn_s))
```

```python
import functools
import math

import jax
import jax.numpy as jnp
from jax import lax
from jax.experimental import pallas as pl
from jax.experimental.pallas import tpu as pltpu

F32 = jnp.float32
BF16 = jnp.bfloat16
RMS_EPS = 1e-6
NEG_INF = -1e30

LANE = 128
SUBLANE = 8
VMEM_LIMIT = 56 * 1024 * 1024
PROJ_VMEM_BUDGET = 40 * 1024 * 1024
FFN_DOWN_ROWS = 256

D_MODEL = 2048
GDN_K_HEADS = 16
GDN_V_HEADS = 32
GDN_DK = 128
GDN_DV = 128
GDN_KDIM = GDN_K_HEADS * GDN_DK
GDN_VDIM = GDN_V_HEADS * GDN_DV
GDN_CONV_DIM = 2 * GDN_KDIM + GDN_VDIM
GDN_CHUNK = 64

NSA_HEADS = 16
NSA_KV_HEADS = 4
NSA_HD = 128
NSA_GROUP = NSA_HEADS // NSA_KV_HEADS
NSA_QDIM = NSA_HEADS * NSA_HD
NSA_KVDIM = NSA_KV_HEADS * NSA_HD
CMP_BLOCK = 32
CMP_STRIDE = 16
CMP_HIDDEN = 2 * NSA_HD
SEL_BLOCK = 64
SEL_SHIFT = 6
N_SEL = 16
WINDOW = 512
FORCE_BONUS = 1e3
N_BUCKETS = 32
REL_MAX_DIST = 1024
PAGE_SIZE = 128
PAGES_PER_STEP = 4
T5_SATURATION = 790
NT_DIMS = (((1,), (1,)), ((), ()))


def _proj_kernel(*refs, n_x, has_norm, has_res, has_post):
    it = iter(refs)
    x_refs = [next(it) for _ in range(n_x)]
    nw_ref = next(it) if has_norm else None
    w_ref = next(it)
    res_ref = next(it) if has_res else None
    pw_ref = next(it) if has_post else None
    pm_ref = next(it) if has_post else None
    o_ref = next(it)
    xs_ref = next(it)

    @pl.when(pl.program_id(1) == 0)
    def _():
        x = x_refs[0][...].astype(F32)
        for r in x_refs[1:]:
            x = x + r[...].astype(F32)
        if has_norm:
            ms = jnp.mean(x * x, axis=-1, keepdims=True)
            x = x * lax.rsqrt(ms + RMS_EPS) * nw_ref[...]
        xs_ref[...] = x.astype(BF16)

    y = jnp.dot(xs_ref[...], w_ref[...], preferred_element_type=F32)
    if has_res:
        y = y + res_ref[...]
    if has_post:
        tn = y.shape[1]
        for g in range(tn // LANE):
            sl = slice(g * LANE, (g + 1) * LANE)
            yg = y[:, sl]
            mode = pm_ref[:, sl]
            ms = jnp.mean(yg * yg, axis=-1, keepdims=True)
            normed = yg * lax.rsqrt(ms + RMS_EPS) * pw_ref[:, sl]
            sig = jax.nn.sigmoid(yg)
            o_ref[:, sl] = jnp.where(mode == 1.0, normed, jnp.where(mode == 2.0, sig, yg))
    else:
        o_ref[...] = y.astype(o_ref.dtype)


def _pick_tile(n, candidates):
    for c in candidates:
        if n % c == 0:
            return c
    return n


def proj(xs, w, *, norm_w=None, res=None, post=None, out_dtype=F32, name="proj"):
    if not isinstance(xs, (tuple, list)):
        xs = (xs,)
    m, k = xs[0].shape
    n = w.shape[1]
    tn = _pick_tile(n, (512, 384, 256, 128))
    x_row_bytes = sum(k * x.dtype.itemsize for x in xs)

    def vmem_bytes(tm):
        return 2 * (tm * x_row_bytes + k * tn * 2 + tm * tn * 4 * (2 if res is not None else 1)) + tm * k * 2

    tm = next((c for c in (1024, 512, 256) if m % c == 0 and vmem_bytes(c) <= PROJ_VMEM_BUDGET), m)
    in_specs = [pl.BlockSpec((tm, k), lambda i, j: (i, 0)) for _ in xs]
    args = list(xs)
    if norm_w is not None:
        in_specs.append(pl.BlockSpec((1, k), lambda i, j: (0, 0)))
        args.append(norm_w.reshape(1, k).astype(F32))
    in_specs.append(pl.BlockSpec((k, tn), lambda i, j: (0, j)))
    args.append(w)
    if res is not None:
        in_specs.append(pl.BlockSpec((tm, tn), lambda i, j: (i, j)))
        args.append(res)
    if post is not None:
        for a in post:
            in_specs.append(pl.BlockSpec((1, tn), lambda i, j: (0, j)))
            args.append(a.reshape(1, n).astype(F32))
    kern = functools.partial(_proj_kernel, n_x=len(xs), has_norm=norm_w is not None, has_res=res is not None,
                             has_post=post is not None)
    return pl.pallas_call(
        kern,
        grid=(m // tm, n // tn),
        in_specs=in_specs,
        out_specs=pl.BlockSpec((tm, tn), lambda i, j: (i, j)),
        out_shape=jax.ShapeDtypeStruct((m, n), out_dtype),
        scratch_shapes=[pltpu.VMEM((tm, k), BF16)],
        compiler_params=pltpu.CompilerParams(dimension_semantics=("parallel", "arbitrary"),
                                             vmem_limit_bytes=VMEM_LIMIT),
        name=name,
    )(*args)


def _conv_gate(pad_ref, prev, gate, val, cw, cb, width):
    t = gate.shape[0]
    halo = SUBLANE
    pad_ref[halo - (width - 1):halo, :] = prev
    pad_ref[halo:halo + t, :] = gate
    acc = cb + pad_ref[halo - (width - 1):halo - (width - 1) + t, :] * cw[0:1, :]
    for i in range(1, width):
        off = halo - (width - 1) + i
        acc = acc + pad_ref[off:off + t, :] * cw[i:i + 1, :]
    return jax.nn.silu(acc) * val


def _ffn_gate_kernel(gate_ref, val_ref, pre_ref, cw_ref, cb_ref, o_ref, pad_ref, *, width):
    for i in range(gate_ref.shape[0]):
        hid = _conv_gate(pad_ref, pre_ref[i], gate_ref[i], val_ref[i], cw_ref[...], cb_ref[...], width)
        o_ref[i] = hid.astype(o_ref.dtype)


def ffn_gate(up, prefix, conv_w, conv_b, d_ff):
    b, t, _ = up.shape
    width = conv_w.shape[1]
    nblk = d_ff // LANE
    kern = functools.partial(_ffn_gate_kernel, width=width)
    return pl.pallas_call(
        kern,
        grid=(nblk,),
        in_specs=[
            pl.BlockSpec((b, t, LANE), lambda j: (0, 0, j)),
            pl.BlockSpec((b, t, LANE), lambda j: (0, 0, j + nblk)),
            pl.BlockSpec((b, width - 1, LANE), lambda j: (0, 0, j)),
            pl.BlockSpec((width, LANE), lambda j: (0, j)),
            pl.BlockSpec((1, LANE), lambda j: (0, j)),
        ],
        out_specs=pl.BlockSpec((b, t, LANE), lambda j: (0, 0, j)),
        out_shape=jax.ShapeDtypeStruct((b, t, d_ff), BF16),
        scratch_shapes=[pltpu.VMEM((t + SUBLANE, LANE), F32)],
        compiler_params=pltpu.CompilerParams(dimension_semantics=("parallel",)),
        name="ffn_gate",
    )(up, up, prefix, conv_w.T, conv_b.reshape(1, d_ff))


def _ffn_down_kernel(gate_ref, val_ref, halo_ref, pre_ref, cw_ref, cb_ref, w_ref, res_ref, o_ref, hid_ref, pad_ref, *,
                     width, tiles_per_batch):
    @pl.when(pl.program_id(1) == 0)
    def _():
        first = (pl.program_id(0) % tiles_per_batch) == 0

        def column_block(c, carry):
            cs = pl.ds(pl.multiple_of(c * LANE, LANE), LANE)
            prev = jnp.where(first, pre_ref[:, cs], halo_ref[SUBLANE - (width - 1):SUBLANE, cs])
            hid = _conv_gate(pad_ref, prev, gate_ref[:, cs], val_ref[:, cs], cw_ref[:, cs], cb_ref[:, cs], width)
            hid_ref[:, cs] = hid.astype(BF16)
            return carry

        lax.fori_loop(0, gate_ref.shape[1] // LANE, column_block, 0)

    o_ref[...] = jnp.dot(hid_ref[...], w_ref[...], preferred_element_type=F32) + res_ref[...]


def ffn_down_fused(up2d, b, t, prefix, conv_w, conv_b, w_down, res):
    d_ff, n = w_down.shape
    width = conv_w.shape[1]
    tm, tn = FFN_DOWN_ROWS, 512
    assert t % tm == 0 and n % tn == 0
    tiles_per_batch = t // tm
    kern = functools.partial(_ffn_down_kernel, width=width, tiles_per_batch=tiles_per_batch)
    return pl.pallas_call(
        kern,
        grid=(b * tiles_per_batch, n // tn),
        in_specs=[
            pl.BlockSpec((tm, d_ff), lambda i, j: (i, 0)),
            pl.BlockSpec((tm, d_ff), lambda i, j: (i, 1)),
            pl.BlockSpec((SUBLANE, d_ff), lambda i, j: (jnp.maximum(i * (tm // SUBLANE) - 1, 0), 0)),
            pl.BlockSpec((None, width - 1, d_ff), lambda i, j: (i // tiles_per_batch, 0, 0)),
            pl.BlockSpec((width, d_ff), lambda i, j: (0, 0)),
            pl.BlockSpec((1, d_ff), lambda i, j: (0, 0)),
            pl.BlockSpec((d_ff, tn), lambda i, j: (0, j)),
            pl.BlockSpec((tm, tn), lambda i, j: (i, j)),
        ],
        out_specs=pl.BlockSpec((tm, tn), lambda i, j: (i, j)),
        out_shape=jax.ShapeDtypeStruct((b * t, n), F32),
        scratch_shapes=[pltpu.VMEM((tm, d_ff), BF16), pltpu.VMEM((tm + SUBLANE, LANE), F32)],
        compiler_params=pltpu.CompilerParams(dimension_semantics=("parallel", "arbitrary"),
                                             vmem_limit_bytes=VMEM_LIMIT),
        name="ffn_down_fused",
    )(up2d, up2d, up2d, prefix, conv_w.T, conv_b.reshape(1, d_ff), w_down, res)


def conv_ffn(h2d, b, t, prefix, norm_w, w_up, conv_w, conv_b, w_down):
    d_ff = conv_w.shape[0]
    keep = conv_w.shape[1] - 1
    assert t >= keep
    up2d = proj(h2d, w_up, norm_w=norm_w, name="ffn_up")
    up = up2d.reshape(b, t, -1)
    new_prefix = up[:, t - keep:, :d_ff]
    if t % FFN_DOWN_ROWS == 0:
        return ffn_down_fused(up2d, b, t, prefix, conv_w, conv_b, w_down, h2d), new_prefix
    hidden = ffn_gate(up, prefix, conv_w, conv_b, d_ff)
    out = proj(hidden.reshape(b * t, d_ff), w_down, res=h2d, name="ffn_down")
    return out, new_prefix


def t5_bucket(dist):
    d = jnp.maximum(dist, 0)
    max_exact = N_BUCKETS // 2
    scale = (N_BUCKETS - max_exact) / math.log(REL_MAX_DIST / max_exact)
    large = max_exact + (jnp.log(jnp.maximum(d, 1).astype(F32) / max_exact) * scale).astype(jnp.int32)
    return jnp.where(d < max_exact, d, jnp.minimum(large, N_BUCKETS - 1))


GDN_CONV_COLS = 512
GDN_HEADS_PER_STEP = 8
GDN_GROUP = 4


def _gdn_conv_kernel(x_ref, pre_ref, cw_ref, o_ref, pad_ref, *, width, n_q_blocks, n_qk_blocks):
    j = pl.program_id(1)
    t = x_ref.shape[0]
    halo = SUBLANE
    pad_ref[0:halo, :] = jnp.zeros((halo, pad_ref.shape[1]), F32)
    pad_ref[halo - (width - 1):halo, :] = pre_ref[...]
    pad_ref[halo:halo + t, :] = x_ref[...]
    acc = pad_ref[halo - (width - 1):halo - (width - 1) + t, :] * cw_ref[0:1, :]
    for i in range(1, width):
        off = halo - (width - 1) + i
        acc = acc + pad_ref[off:off + t, :] * cw_ref[i:i + 1, :]
    y = jax.nn.silu(acc)
    scale = jnp.where(j < n_q_blocks, GDN_DK ** -0.5, 1.0)
    is_qk = j < n_qk_blocks
    for h in range(x_ref.shape[1] // LANE):
        seg = y[:, h * LANE:(h + 1) * LANE]
        nrm = seg * lax.rsqrt(jnp.sum(seg * seg, axis=-1, keepdims=True) + RMS_EPS) * scale
        o_ref[:, h * LANE:(h + 1) * LANE] = jnp.where(is_qk, nrm, seg)


def gdn_conv(proj3, prefix, conv_w):
    b, t, _ = proj3.shape
    width = conv_w.shape[1]
    cols = GDN_CONV_COLS
    kern = functools.partial(_gdn_conv_kernel, width=width, n_q_blocks=GDN_KDIM // cols,
                             n_qk_blocks=2 * GDN_KDIM // cols)
    return pl.pallas_call(
        kern,
        grid=(b, GDN_CONV_DIM // cols),
        in_specs=[
            pl.BlockSpec((None, t, cols), lambda i, j: (i, 0, j)),
            pl.BlockSpec((None, width - 1, cols), lambda i, j: (i, 0, j)),
            pl.BlockSpec((width, cols), lambda i, j: (0, j)),
        ],
        out_specs=pl.BlockSpec((None, t, cols), lambda i, j: (i, 0, j)),
        out_shape=jax.ShapeDtypeStruct((b, t, GDN_CONV_DIM), F32),
        scratch_shapes=[pltpu.VMEM((t + SUBLANE, cols), F32)],
        compiler_params=pltpu.CompilerParams(dimension_semantics=("parallel", "parallel"),
                                             vmem_limit_bytes=VMEM_LIMIT),
        name="gdn_conv",
    )(proj3, prefix, conv_w.T)


def _split_bf16(x):
    hi = x.astype(BF16)
    return hi, (x - hi.astype(F32)).astype(BF16)


def _dot_split(a_parts, b_parts):
    (ah, al), (bh, bl) = a_parts, b_parts
    return (jnp.dot(ah, bh, preferred_element_type=F32) + jnp.dot(ah, bl, preferred_element_type=F32)
            + jnp.dot(al, bh, preferred_element_type=F32))


def _gdn_gate_kernel(x_ref, a_ref, dt_ref, o_ref, *, t_real):
    L = GDN_CHUNK
    lane = lax.broadcasted_iota(jnp.int32, (L, LANE), 1)
    row = lax.broadcasted_iota(jnp.int32, (L, LANE), 0)
    ci = lax.broadcasted_iota(jnp.int32, (L, L), 0)
    cj = lax.broadcasted_iota(jnp.int32, (L, L), 1)
    tril = jnp.where(ci >= cj, 1.0, 0.0).astype(BF16)
    neg_a = -jnp.exp(a_ref[...])

    def chunk(c, carry):
        r0 = pl.multiple_of(c * L, L)
        x = x_ref[pl.ds(r0, L), :]
        live = (row + r0) < t_real
        beta = jnp.where(live, jax.nn.sigmoid(x), 0.0)
        z = x + dt_ref[...]
        softplus = jnp.maximum(z, 0.0) + jnp.log(1.0 + jnp.exp(-jnp.abs(z)))
        g = jnp.where(live, neg_a * softplus, 0.0)
        hi, mid = _split_bf16(g)
        lo = (g - hi.astype(F32) - mid.astype(F32)).astype(BF16)
        cum = (jnp.dot(tril, hi, preferred_element_type=F32) + jnp.dot(tril, mid, preferred_element_type=F32)
               + jnp.dot(tril, lo, preferred_element_type=F32))
        o_ref[pl.ds(r0, L), :] = jnp.where(lane < GDN_V_HEADS, beta, cum)
        return carry

    lax.fori_loop(0, x_ref.shape[0] // L, chunk, 0)


def gdn_gates(gate3, a_log, dt_bias, t_real):
    b, t, _ = gate3.shape
    pad = jnp.zeros((GDN_V_HEADS,), F32)
    a_vec = jnp.concatenate([pad, a_log.astype(F32), pad, pad]).reshape(1, LANE)
    dt_vec = jnp.concatenate([pad, dt_bias.astype(F32), pad, pad]).reshape(1, LANE)
    return pl.pallas_call(
        functools.partial(_gdn_gate_kernel, t_real=t_real),
        grid=(b,),
        in_specs=[pl.BlockSpec((None, t, LANE), lambda i: (i, 0, 0)),
                  pl.BlockSpec((1, LANE), lambda i: (0, 0)),
                  pl.BlockSpec((1, LANE), lambda i: (0, 0))],
        out_specs=pl.BlockSpec((None, t, LANE), lambda i: (i, 0, 0)),
        out_shape=jax.ShapeDtypeStruct((b, t, LANE), F32),
        compiler_params=pltpu.CompilerParams(dimension_semantics=("parallel",)),
        name="gdn_gates",
    )(gate3, a_vec, dt_vec)


def _pad_rows(x, rows):
    if x.shape[0] == rows:
        return x
    return jnp.concatenate([x, jnp.zeros((rows - x.shape[0], x.shape[1]), x.dtype)], axis=0)


def _gdn_delta_kernel(q_ref, k_ref, v_ref, z_ref, gb_ref, gt_ref, s0_ref, nw_ref, o_ref, s_out_ref, s_ref, *, nc, tl):
    hg = pl.program_id(1)
    c = pl.program_id(2)
    L = GDN_CHUNK
    hb = GDN_HEADS_PER_STEP

    @pl.when(c == 0)
    def _():
        s_ref[...] = s0_ref[...]

    gs = GDN_GROUP
    n_groups = hb // gs
    rows = gs * L
    shift = L.bit_length() - 1
    gb = gb_ref[...]
    lane = lax.broadcasted_iota(jnp.int32, (L, LANE), 1)
    ri = lax.broadcasted_iota(jnp.int32, (rows, rows), 0)
    cj = lax.broadcasted_iota(jnp.int32, (rows, rows), 1)
    same = (ri >> shift) == (cj >> shift)
    strict = same & (ri > cj)
    incl = same & (ri >= cj)

    def stack(ref, width_of):
        return [jnp.concatenate([_pad_rows(ref[:, width_of(hh) * LANE:(width_of(hh) + 1) * LANE], L)
                                 for hh in range(g * gs, (g + 1) * gs)], axis=0) for g in range(n_groups)]

    q = stack(q_ref, lambda hh: hh // 2)
    k = stack(k_ref, lambda hh: hh // 2)
    v = stack(v_ref, lambda hh: hh)
    bcol, gcol, grow, eg, a_qk, pw, x = [], [], [], [], [], [], []
    for g in range(n_groups):
        heads = [hg * hb + g * gs + hh for hh in range(gs)]
        bcol.append(jnp.concatenate(
            [jnp.sum(jnp.where(lane == h, gb, 0.0), axis=-1, keepdims=True) for h in heads], axis=0))
        gcol.append(jnp.concatenate(
            [jnp.sum(jnp.where(lane == h + GDN_V_HEADS, gb, 0.0), axis=-1, keepdims=True) for h in heads], axis=0))
        grow.append(gt_ref[pl.ds(c, 1), g * rows:(g + 1) * rows])
        decay = jnp.exp(jnp.where(same, gcol[g] - grow[g], 0.0))
        kb = k[g].astype(BF16)
        kk = lax.dot_general(kb, kb, NT_DIMS, preferred_element_type=F32)
        qk = lax.dot_general(q[g].astype(BF16), kb, NT_DIMS, preferred_element_type=F32)
        a_qk.append((qk * jnp.where(incl, decay, 0.0)).astype(BF16))
        eg.append(jnp.exp(gcol[g]))
        pw.append(_split_bf16(-(bcol[g] * kk * jnp.where(strict, decay, 0.0))))
        x.append(jnp.concatenate([bcol[g] * v[g], (bcol[g] * eg[g]) * k[g]], axis=1))
    for i in range(shift):
        for g in range(n_groups):
            x[g] = x[g] + _dot_split(pw[g], _split_bf16(x[g]))
        if i + 1 < shift:
            for g in range(n_groups):
                pw[g] = _split_bf16(_dot_split(pw[g], pw[g]))
    for g in range(n_groups):
        u_eff, w_kb = x[g][:, :GDN_DV], x[g][:, GDN_DV:].astype(BF16)
        sbs = [s_ref[g * gs + hh].astype(BF16) for hh in range(gs)]
        u = jnp.concatenate(
            [u_eff[hh * L:(hh + 1) * L] - jnp.dot(w_kb[hh * L:(hh + 1) * L], sbs[hh], preferred_element_type=F32)
             for hh in range(gs)], axis=0)
        ub = u.astype(BF16)
        o_intra = jnp.dot(a_qk[g], ub, preferred_element_type=F32)
        q_dec = (q[g] * eg[g]).astype(BF16)
        for hh in range(gs):
            hr = slice(hh * L, (hh + 1) * L)
            hi = g * gs + hh
            o = jnp.dot(q_dec[hr], sbs[hh], preferred_element_type=F32) + o_intra[hr]
            g_last = grow[g][:, hh * L + L - 1:hh * L + L]
            k_dec = (k[g][hr] * jnp.exp(g_last - gcol[g][hr])).astype(BF16)
            s_ref[hi] = jnp.exp(g_last) * s_ref[hi] + lax.dot_general(k_dec, ub[hr], (((0,), (0,)), ((), ())),
                                                                      preferred_element_type=F32)
            on = o * lax.rsqrt(jnp.mean(o * o, axis=-1, keepdims=True) + RMS_EPS) * nw_ref[...]
            gated = on[:tl] * jax.nn.silu(z_ref[:, hi * LANE:(hi + 1) * LANE])
            o_ref[:, hi * LANE:(hi + 1) * LANE] = gated.astype(o_ref.dtype)

    @pl.when(c == nc - 1)
    def _():
        s_out_ref[...] = s_ref[...]


def gdn_delta(qkv, proj3, gb, gt, s0, norm_w, tl):
    b, t, _ = qkv.shape
    nc = gb.shape[1] // GDN_CHUNK
    hb = GDN_HEADS_PER_STEP
    kw = (hb // 2) * GDN_DK
    vw = hb * GDN_DV
    kern = functools.partial(_gdn_delta_kernel, nc=nc, tl=tl)
    return pl.pallas_call(
        kern,
        grid=(b, GDN_V_HEADS // hb, nc),
        in_specs=[
            pl.BlockSpec((None, tl, kw), lambda i, h, c: (i, c, h)),
            pl.BlockSpec((None, tl, kw), lambda i, h, c: (i, c, GDN_KDIM // kw + h)),
            pl.BlockSpec((None, tl, vw), lambda i, h, c: (i, c, 2 * GDN_KDIM // vw + h)),
            pl.BlockSpec((None, tl, vw), lambda i, h, c: (i, c, GDN_CONV_DIM // vw + h)),
            pl.BlockSpec((None, GDN_CHUNK, LANE), lambda i, h, c: (i, c, 0)),
            pl.BlockSpec((None, None, nc, hb * GDN_CHUNK), lambda i, h, c: (i, h, 0, 0)),
            pl.BlockSpec((None, hb, GDN_DK, GDN_DV), lambda i, h, c: (i, h, 0, 0)),
            pl.BlockSpec((1, GDN_DV), lambda i, h, c: (0, 0)),
        ],
        out_specs=[
            pl.BlockSpec((None, tl, vw), lambda i, h, c: (i, c, h)),
            pl.BlockSpec((None, hb, GDN_DK, GDN_DV), lambda i, h, c: (i, h, 0, 0)),
        ],
        out_shape=[jax.ShapeDtypeStruct((b, t, GDN_VDIM), BF16),
                   jax.ShapeDtypeStruct(s0.shape, F32)],
        scratch_shapes=[pltpu.VMEM((hb, GDN_DK, GDN_DV), F32)],
        compiler_params=pltpu.CompilerParams(dimension_semantics=("parallel", "parallel", "arbitrary"),
                                             vmem_limit_bytes=VMEM_LIMIT),
        name="gdn_delta",
    )(qkv, qkv, qkv, proj3, gb, gt, s0, norm_w.reshape(1, GDN_DV).astype(F32))


def gdn_mixer(h2d, b, t, s0, conv_prefix, norm_in, w_in, conv_w, a_log, dt_bias, norm_w, w_out):
    w_main, w_gate = w_in
    proj3 = proj(h2d, w_main, norm_w=norm_in, name="gdn_in").reshape(b, t, -1)
    gate3 = proj(h2d, w_gate, norm_w=norm_in, name="gdn_in_gate").reshape(b, t, -1)
    keep = conv_w.shape[1] - 1
    assert t >= keep
    new_prefix = proj3[:, t - keep:, :GDN_CONV_DIM]
    qkv = gdn_conv(proj3, conv_prefix, conv_w)
    tl = min(t, GDN_CHUNK)
    t_pad = -(-t // GDN_CHUNK) * GDN_CHUNK
    gb = gdn_gates(jnp.pad(gate3, ((0, 0), (0, t_pad - t), (0, 0))), a_log, dt_bias, t)
    nc = t_pad // GDN_CHUNK
    hb = GDN_HEADS_PER_STEP
    gt = gb[:, :, GDN_V_HEADS:2 * GDN_V_HEADS].reshape(b, nc, GDN_CHUNK, GDN_V_HEADS // hb, hb)
    gt = gt.transpose(0, 3, 1, 4, 2).reshape(b, GDN_V_HEADS // hb, nc, hb * GDN_CHUNK)
    o, s_new = gdn_delta(qkv, proj3, gb, gt, s0.astype(F32), norm_w, tl)
    y = proj(o.reshape(b * t, GDN_VDIM), w_out, res=h2d, name="gdn_out")
    return y, s_new, new_prefix


def _compress_mlp(get_x, w1_at, pe, w2, k_norm, acc_ref, s_all):
    acc_ref[0:s_all, :] = jnp.zeros((s_all, 2 * CMP_HIDDEN), F32)

    def body(r, carry):
        acc_ref[0:s_all, :] += jnp.dot(get_x(r), w1_at(r), preferred_element_type=F32)
        return carry

    lax.fori_loop(0, CMP_STRIDE, body, 0)
    hid = acc_ref[0:s_all, 0:CMP_HIDDEN] + acc_ref[1:s_all + 1, CMP_HIDDEN:2 * CMP_HIDDEN] + pe
    y = jnp.dot(jax.nn.gelu(hid).astype(BF16), w2, preferred_element_type=F32)
    if k_norm is None:
        return y
    return y * lax.rsqrt(jnp.mean(y * y, axis=-1, keepdims=True) + RMS_EPS) * k_norm


def _compress_kernel(*refs, s_all):
    x_refs = refs[:NSA_KV_HEADS]
    w1_ref, pe_ref, w2_ref, nw_ref, o_ref, xs_ref, acc_ref = refs[NSA_KV_HEADS:]
    for r in range(CMP_STRIDE):
        for g in range(NSA_KV_HEADS):
            xs_ref[r, g] = x_refs[g][pl.ds(r, s_all, stride=CMP_STRIDE), :].astype(BF16)
    is_k = pl.program_id(1) == 0
    acc_ref[s_all:s_all + SUBLANE, :] = jnp.zeros((SUBLANE, 2 * CMP_HIDDEN), F32)
    for g in range(NSA_KV_HEADS):
        y = _compress_mlp(lambda r: xs_ref[r, g], lambda r: w1_ref[r], pe_ref[...], w2_ref[...], None, acc_ref,
                          s_all)
        normed = y * lax.rsqrt(jnp.mean(y * y, axis=-1, keepdims=True) + RMS_EPS) * nw_ref[...]
        o_ref[g] = jnp.where(is_k, normed, y)


def nsa_compress(x, col0, w1r, pe_hid, w2, k_norm0):
    b, rows, _ = x.shape
    s_all = rows // CMP_STRIDE

    def x_spec(g):
        return pl.BlockSpec((None, rows, NSA_HD), lambda i, c: (i, 0, (col0 + c) * NSA_KV_HEADS + g))

    in_specs = [x_spec(g) for g in range(NSA_KV_HEADS)] + [
        pl.BlockSpec((None, CMP_STRIDE, NSA_HD, 2 * CMP_HIDDEN), lambda i, c: (c, 0, 0, 0)),
        pl.BlockSpec((None, 1, CMP_HIDDEN), lambda i, c: (c, 0, 0)),
        pl.BlockSpec((None, CMP_HIDDEN, NSA_HD), lambda i, c: (c, 0, 0)),
        pl.BlockSpec((1, NSA_HD), lambda i, c: (0, 0)),
    ]
    return pl.pallas_call(
        functools.partial(_compress_kernel, s_all=s_all),
        grid=(b, 2),
        in_specs=in_specs,
        out_specs=pl.BlockSpec((None, None, NSA_KV_HEADS, s_all, NSA_HD), lambda i, c: (i, c, 0, 0, 0)),
        out_shape=jax.ShapeDtypeStruct((b, 2, NSA_KV_HEADS, s_all, NSA_HD), F32),
        scratch_shapes=[pltpu.VMEM((CMP_STRIDE, NSA_KV_HEADS, s_all, NSA_HD), BF16),
                        pltpu.VMEM((s_all + SUBLANE, 2 * CMP_HIDDEN), F32)],
        compiler_params=pltpu.CompilerParams(dimension_semantics=("parallel", "parallel"),
                                             vmem_limit_bytes=VMEM_LIMIT),
        name="nsa_compress",
    )(*((x,) * NSA_KV_HEADS), w1r, pe_hid, w2, k_norm0.reshape(1, NSA_HD).astype(F32))


CMP_FINAL_ROWS = 512


def _compress_paged_kernel(*refs, n_steps, pps):
    x_refs = refs[1:1 + pps]
    w1_ref, pe_ref, w2_ref, nw_ref, o_ref, acc_ref = refs[1 + pps:]
    p = pl.program_id(1)
    planes = 2 * NSA_KV_HEADS
    sp = PAGE_SIZE // CMP_STRIDE
    rows_pp = sp * planes
    n_rows = n_steps * pps * rows_pp
    hid_w = CMP_HIDDEN
    lhs = jnp.concatenate(
        [jnp.concatenate([x[pl.ds(r, sp, stride=CMP_STRIDE), :, :].reshape(rows_pp, NSA_HD).astype(BF16)
                          for r in range(CMP_STRIDE)], axis=1) for x in x_refs], axis=0)
    row0 = pl.multiple_of(p * (pps * rows_pp), pps * rows_pp)
    acc_ref[pl.ds(row0, pps * rows_pp), :] = jnp.dot(lhs, w1_ref[...], preferred_element_type=F32)

    @pl.when(p == n_steps - 1)
    def _():
        acc_ref[n_rows:n_rows + planes, :] = jnp.zeros((planes, 4 * hid_w), F32)
        cr = CMP_FINAL_ROWS
        is_k = (lax.broadcasted_iota(jnp.int32, (cr, 1), 0) % planes) < NSA_KV_HEADS

        def chunk(ci, carry):
            r0 = pl.multiple_of(ci * cr, cr)
            a = acc_ref[pl.ds(r0, cr), :]
            nxt = acc_ref[pl.ds(r0 + planes, cr), :]
            hid_k = a[:, 0:hid_w] + nxt[:, hid_w:2 * hid_w] + pe_ref[0]
            hid_v = a[:, 2 * hid_w:3 * hid_w] + nxt[:, 3 * hid_w:4 * hid_w] + pe_ref[1]
            act = jax.nn.gelu(jnp.where(is_k, hid_k, hid_v)).astype(BF16)
            y2 = jnp.dot(act, w2_ref[...], preferred_element_type=F32)
            y = jnp.where(is_k, y2[:, :NSA_HD], y2[:, NSA_HD:])
            normed = y * lax.rsqrt(jnp.mean(y * y, axis=-1, keepdims=True) + RMS_EPS) * nw_ref[...]
            o_ref[pl.ds(r0, cr), :] = jnp.where(is_k, normed, y)
            return carry

        lax.fori_loop(0, n_rows // cr, chunk, 0)


def nsa_compress_paged(pool4, page_table, w1r, pe_hid, w2, k_norm0):
    b, n_pages = page_table.shape
    pps = PAGES_PER_STEP
    assert n_pages % pps == 0
    n_steps = n_pages // pps
    planes = 2 * NSA_KV_HEADS
    s_all = n_pages * (PAGE_SIZE // CMP_STRIDE)
    n_rows = s_all * planes
    assert n_rows % CMP_FINAL_ROWS == 0
    w1cat = w1r.transpose(1, 2, 0, 3).reshape(CMP_STRIDE * NSA_HD, 4 * CMP_HIDDEN)
    w2cat = jnp.concatenate([w2[0], w2[1]], axis=1)

    def page_spec(u):
        return pl.BlockSpec((None, PAGE_SIZE, planes, NSA_HD), lambda i, p, pt: (pt[i, p * pps + u], 0, 0, 0))

    in_specs = [page_spec(u) for u in range(pps)] + [
        pl.BlockSpec(w1cat.shape, lambda i, p, pt: (0, 0)),
        pl.BlockSpec((2, 1, CMP_HIDDEN), lambda i, p, pt: (0, 0, 0)),
        pl.BlockSpec(w2cat.shape, lambda i, p, pt: (0, 0)),
        pl.BlockSpec((1, NSA_HD), lambda i, p, pt: (0, 0)),
    ]
    grid_spec = pltpu.PrefetchScalarGridSpec(
        num_scalar_prefetch=1, grid=(b, n_steps), in_specs=in_specs,
        out_specs=pl.BlockSpec((None, n_rows, NSA_HD), lambda i, p, pt: (i, 0, 0)),
        scratch_shapes=[pltpu.VMEM((n_rows + planes, 4 * CMP_HIDDEN), F32)])
    out = pl.pallas_call(
        functools.partial(_compress_paged_kernel, n_steps=n_steps, pps=pps),
        grid_spec=grid_spec,
        out_shape=jax.ShapeDtypeStruct((b, n_rows, NSA_HD), F32),
        compiler_params=pltpu.CompilerParams(dimension_semantics=("parallel", "arbitrary"),
                                             vmem_limit_bytes=VMEM_LIMIT),
        name="nsa_compress_paged",
    )(page_table, *((pool4,) * pps), w1cat, pe_hid, w2cat, k_norm0.reshape(1, NSA_HD).astype(F32))
    return out.reshape(b, s_all, 2, NSA_KV_HEADS, NSA_HD).transpose(0, 2, 3, 1, 4)


def _stack_heads(qb):
    return jnp.concatenate([qb[:, j * LANE:(j + 1) * LANE] for j in range(NSA_GROUP)], axis=0)


def _cmp_attn_kernel(q_ref, kc_ref, vc_ref, cb_ref, gate_ref, ovl_ref, o_ref, sel_ref, *, tq, q0, n_sb):
    i = pl.program_id(2)
    s_all = kc_ref.shape[0]
    nsbp = sel_ref.shape[-1]
    q4 = _stack_heads(q_ref[...]).astype(BF16)
    s = lax.dot_general(q4, kc_ref[...].astype(BF16), NT_DIMS, preferred_element_type=F32)
    bias = cb_ref[...].reshape(NSA_GROUP * tq, s_all)
    s = s + bias
    ok = bias > 0.5 * NEG_INF
    m = jnp.max(s, axis=-1, keepdims=True)
    e = jnp.exp(s - m)
    p = jnp.where(ok, e / jnp.sum(e, axis=-1, keepdims=True), 0.0)
    o4 = jnp.dot(p.astype(BF16), vc_ref[...].astype(BF16), preferred_element_type=F32)
    for j in range(NSA_GROUP):
        o_ref[:, j * LANE:(j + 1) * LANE] = o4[j * tq:(j + 1) * tq] * gate_ref[:, j:j + 1]

    psum = p[0:tq] + p[tq:2 * tq] + p[2 * tq:3 * tq] + p[3 * tq:4 * tq]
    ovl = ovl_ref[...]
    p_hi = psum.astype(BF16)
    r1 = psum - p_hi.astype(F32)
    p_mid = r1.astype(BF16)
    p_lo = (r1 - p_mid.astype(F32)).astype(BF16)
    imp = (jnp.dot(p_hi, ovl, preferred_element_type=F32) + jnp.dot(p_mid, ovl, preferred_element_type=F32)
           + jnp.dot(p_lo, ovl, preferred_element_type=F32))

    lane = lax.broadcasted_iota(jnp.int32, (tq, nsbp), 1)
    qpos = q0 + i * tq + lax.broadcasted_iota(jnp.int32, (tq, nsbp), 0)
    cur = qpos >> SEL_SHIFT
    forced = (lane == 0) | (lane == cur) | (lane == cur - 1)
    sb_ok = (lane << SEL_SHIFT) <= qpos
    score = jnp.where(sb_ok, imp + jnp.where(forced, FORCE_BONUS, 0.0), NEG_INF)
    work = jnp.where(lane < n_sb, score, -jnp.inf)
    lane_f = lane.astype(F32)
    selneg = jnp.full((tq, nsbp), NEG_INF, F32)
    for _ in range(N_SEL):
        mx = jnp.max(work, axis=-1, keepdims=True)
        first = jnp.min(jnp.where(work == mx, lane_f, 1e9), axis=-1, keepdims=True)
        hit = lane_f == first
        selneg = jnp.where(hit & (mx > 0.5 * NEG_INF), 0.0, selneg)
        work = jnp.where(hit, -jnp.inf, work)
    sel_ref[...] = selneg


def nsa_cmp_attention(proj3, kvc, cmp_bias, gates_r, ovl, tq, q0, n_sb):
    b, t, _ = proj3.shape
    s_all = kvc.shape[3]
    nsbp = ovl.shape[1]
    kern = functools.partial(_cmp_attn_kernel, tq=tq, q0=q0, n_sb=n_sb)
    return pl.pallas_call(
        kern,
        grid=(b, NSA_KV_HEADS, t // tq),
        in_specs=[
            pl.BlockSpec((None, tq, NSA_GROUP * NSA_HD), lambda i, g, q: (i, q, g)),
            pl.BlockSpec((None, None, None, s_all, NSA_HD), lambda i, g, q: (i, 0, g, 0, 0)),
            pl.BlockSpec((None, None, None, s_all, NSA_HD), lambda i, g, q: (i, 1, g, 0, 0)),
            pl.BlockSpec((NSA_GROUP, tq, s_all), lambda i, g, q: (g, q, 0)),
            pl.BlockSpec((None, None, tq, 3 * NSA_GROUP), lambda i, g, q: (i, g, q, 0)),
            pl.BlockSpec((s_all, nsbp), lambda i, g, q: (0, 0)),
        ],
        out_specs=[
            pl.BlockSpec((None, tq, NSA_GROUP * NSA_HD), lambda i, g, q: (i, q, g)),
            pl.BlockSpec((None, None, tq, nsbp), lambda i, g, q: (i, g, q, 0)),
        ],
        out_shape=[jax.ShapeDtypeStruct((b, t, NSA_QDIM), F32),
                   jax.ShapeDtypeStruct((b, NSA_KV_HEADS, t, nsbp), F32)],
        compiler_params=pltpu.CompilerParams(dimension_semantics=("parallel", "parallel", "parallel"),
                                             vmem_limit_bytes=VMEM_LIMIT),
        name="nsa_cmp_attn",
    )(proj3, kvc, kvc, cmp_bias, gates_r, ovl)


def _softmax_tile_update(s, v, m_prev, l_prev, acc_prev):
    m_new = jnp.maximum(m_prev, jnp.max(s, axis=-1, keepdims=True))
    alpha = jnp.exp(m_prev - m_new)
    p = jnp.exp(s - m_new)
    l_new = alpha * l_prev + jnp.sum(p, axis=-1, keepdims=True)
    acc_new = alpha * acc_prev + jnp.dot(p.astype(BF16), v, preferred_element_type=F32)
    return m_new, l_new, acc_new


def _tile_scores(q4, k, bias3, sel, tq, tk, qpos0, kpos0, window):
    s = lax.dot_general(q4, k, NT_DIMS, preferred_element_type=F32).reshape(NSA_GROUP, tq, tk) + bias3
    kpos = kpos0 + lax.broadcasted_iota(jnp.int32, (tq, tk), 1)
    qpos = qpos0 + lax.broadcasted_iota(jnp.int32, (tq, tk), 0)
    dist = qpos - kpos
    mask = dist >= 0
    if window is not None:
        mask = mask & (dist < window)
    if sel is not None:
        nsbp = sel.shape[1]
        blk = (kpos0 + lax.broadcasted_iota(jnp.int32, (nsbp, tk), 1)) >> SEL_SHIFT
        onehot = jnp.where(blk == lax.broadcasted_iota(jnp.int32, (nsbp, tk), 0), 1.0, 0.0).astype(BF16)
        s = s + jnp.dot(sel.astype(BF16), onehot, preferred_element_type=F32)[None]
    return jnp.where(mask[None], s, NEG_INF).reshape(NSA_GROUP * tq, tk)


def _flash_prompt_kernel(*refs, tq, tk, nkk, use_sel, window, gcol):
    if use_sel:
        q_ref, k_ref, v_ref, b_ref, gate_ref, sel_ref, o_ref, q4_ref, m_ref, l_ref, acc_ref = refs
    else:
        q_ref, k_ref, v_ref, b_ref, gate_ref, o_ref, q4_ref, m_ref, l_ref, acc_ref = refs
        sel_ref = None
    i = pl.program_id(2)
    jj = pl.program_id(3)
    if use_sel:
        j, valid = jj, jj <= i
    else:
        j = i - (nkk - 1) + jj
        valid = j >= 0

    @pl.when(jj == 0)
    def _():
        q4_ref[...] = _stack_heads(q_ref[...]).astype(BF16)
        m_ref[...] = jnp.full(m_ref.shape, NEG_INF, F32)
        l_ref[...] = jnp.zeros(l_ref.shape, F32)
        acc_ref[...] = jnp.zeros(acc_ref.shape, F32)

    @pl.when(valid)
    def _():
        sel = sel_ref[...] if use_sel else None
        s = _tile_scores(q4_ref[...], k_ref[...].astype(BF16), b_ref[...], sel, tq, tk, i * tq, j * tk, window)
        m_new, l_new, acc_new = _softmax_tile_update(s, v_ref[...].astype(BF16), m_ref[...], l_ref[...],
                                                     acc_ref[...])
        m_ref[...] = m_new
        l_ref[...] = l_new
        acc_ref[...] = acc_new

    @pl.when(jj == nkk - 1)
    def _():
        o4 = acc_ref[...] / l_ref[...]
        for h in range(NSA_GROUP):
            o_ref[:, h * LANE:(h + 1) * LANE] = o4[h * tq:(h + 1) * tq] * gate_ref[:, gcol + h:gcol + h + 1]


def nsa_flash_prompt(proj3, bias_tab, gates_r, sel, *, tile, kcol, vcol, window, gcol, name):
    b, t, _ = proj3.shape
    nq = t // tile
    n_dd = bias_tab.shape[1]
    use_sel = sel is not None
    nkk = nq if use_sel else (window + tile - 1) // tile + 1

    def kidx(q, jj):
        return jnp.minimum(jj, q) if use_sel else jnp.maximum(q - (nkk - 1) + jj, 0)

    in_specs = [
        pl.BlockSpec((None, tile, NSA_GROUP * NSA_HD), lambda i, g, q, jj: (i, q, g)),
        pl.BlockSpec((None, tile, NSA_HD), lambda i, g, q, jj: (i, kidx(q, jj), kcol + g)),
        pl.BlockSpec((None, tile, NSA_HD), lambda i, g, q, jj: (i, kidx(q, jj), vcol + g)),
        pl.BlockSpec((NSA_GROUP, None, tile, tile),
                     lambda i, g, q, jj: (g, jnp.minimum(q - kidx(q, jj), n_dd - 1), 0, 0)),
        pl.BlockSpec((None, None, tile, 3 * NSA_GROUP), lambda i, g, q, jj: (i, g, q, 0)),
    ]
    args = [proj3, proj3, proj3, bias_tab, gates_r]
    if use_sel:
        nsbp = sel.shape[-1]
        in_specs.append(pl.BlockSpec((None, None, tile, nsbp), lambda i, g, q, jj: (i, g, q, 0)))
        args.append(sel)
    kern = functools.partial(_flash_prompt_kernel, tq=tile, tk=tile, nkk=nkk, use_sel=use_sel, window=window,
                             gcol=gcol)
    return pl.pallas_call(
        kern,
        grid=(b, NSA_KV_HEADS, nq, nkk),
        in_specs=in_specs,
        out_specs=pl.BlockSpec((None, tile, NSA_GROUP * NSA_HD), lambda i, g, q, jj: (i, q, g)),
        out_shape=jax.ShapeDtypeStruct((b, t, NSA_QDIM), F32),
        scratch_shapes=[pltpu.VMEM((NSA_GROUP * tile, NSA_HD), BF16),
                        pltpu.VMEM((NSA_GROUP * tile, 1), F32),
                        pltpu.VMEM((NSA_GROUP * tile, 1), F32),
                        pltpu.VMEM((NSA_GROUP * tile, NSA_HD), F32)],
        compiler_params=pltpu.CompilerParams(
            dimension_semantics=("parallel", "parallel", "parallel", "arbitrary"), vmem_limit_bytes=VMEM_LIMIT),
        name=name,
    )(*args)


def _flash_sample_kernel(*refs, tq, pps, n_steps, plane0, qtile, kbase, q0, use_sel, window, gcol):
    pt_ref, q_ref, pool_ref, kn_ref, vn_ref, b_ref, gate_ref = refs[:7]
    if use_sel:
        sel_ref, o_ref, kv_ref, sem, q4_ref, m_ref, l_ref, acc_ref = refs[7:]
    else:
        o_ref, kv_ref, sem, q4_ref, m_ref, l_ref, acc_ref = refs[7:]
        sel_ref = None
    i = pl.program_id(0)
    planes = 2 * NSA_KV_HEADS
    page = kv_ref.shape[3]
    tk = pps * page
    n_dd = b_ref.shape[1]

    def page_copies(step, slot):
        return [pltpu.make_async_copy(pool_ref.at[pt_ref[i, step * pps + u], :, plane0 + j, :],
                                      kv_ref.at[slot, u, j], sem.at[slot])
                for u in range(pps) for j in range(planes)]

    for cp in page_copies(0, 0):
        cp.start()
    for g in range(NSA_KV_HEADS):
        q4_ref[g] = _stack_heads(q_ref[:, g * NSA_GROUP * LANE:(g + 1) * NSA_GROUP * LANE]).astype(BF16)
    m_ref[...] = jnp.full(m_ref.shape, NEG_INF, F32)
    l_ref[...] = jnp.zeros(l_ref.shape, F32)
    acc_ref[...] = jnp.zeros(acc_ref.shape, F32)

    def update(g, k, v, bias3, kpos0):
        sel = sel_ref[g] if use_sel else None
        s = _tile_scores(q4_ref[g], k, bias3, sel, tq, k.shape[0], q0, kpos0, window)
        m_new, l_new, acc_new = _softmax_tile_update(s, v, m_ref[g], l_ref[g], acc_ref[g])
        m_ref[g] = m_new
        l_ref[g] = l_new
        acc_ref[g] = acc_new

    def step_body(step, carry):
        slot = step % 2

        @pl.when(step + 1 < n_steps)
        def _():
            for cp in page_copies(step + 1, 1 - slot):
                cp.start()

        for cp in page_copies(step, slot):
            cp.wait()
        dd = jnp.clip(qtile - step, 0, n_dd - 1)
        scores = []
        for g in range(NSA_KV_HEADS):
            k = jnp.concatenate([kv_ref[slot, u, g].astype(BF16) for u in range(pps)], axis=0)
            sel = sel_ref[g] if use_sel else None
            scores.append(_tile_scores(q4_ref[g], k, b_ref[g * NSA_GROUP:(g + 1) * NSA_GROUP, dd], sel, tq, tk, q0,
                                       kbase + step * tk, window))
        for g in range(NSA_KV_HEADS):
            v = jnp.concatenate([kv_ref[slot, u, NSA_KV_HEADS + g].astype(BF16) for u in range(pps)], axis=0)
            m_new, l_new, acc_new = _softmax_tile_update(scores[g], v, m_ref[g], l_ref[g], acc_ref[g])
            m_ref[g] = m_new
            l_ref[g] = l_new
            acc_ref[g] = acc_new
        return carry

    lax.fori_loop(0, n_steps, step_body, 0)

    pad = jnp.zeros((page - tq, LANE), F32)
    for g in range(NSA_KV_HEADS):
        cols = slice(g * LANE, (g + 1) * LANE)
        kn = jnp.concatenate([kn_ref[:, cols], pad], axis=0).astype(BF16)
        vn = jnp.concatenate([vn_ref[:, cols], pad], axis=0).astype(BF16)
        update(g, kn, vn, b_ref[g * NSA_GROUP:(g + 1) * NSA_GROUP, 0, :, 0:page], q0)
        o4 = acc_ref[g] / l_ref[g]
        for j in range(NSA_GROUP):
            h = g * NSA_GROUP + j
            o_ref[:, h * LANE:(h + 1) * LANE] = o4[j * tq:(j + 1) * tq] * gate_ref[g, :, gcol + j:gcol + j + 1]


def nsa_flash_sample(proj3, pool4, table, bias_tab, gates_r, sel, *, plane0, newk, newv, kbase, q0, window,
                     gcol, name):
    b, tq, _ = proj3.shape
    page = pool4.shape[1]
    tk = bias_tab.shape[3]
    pps = tk // page
    n_steps = table.shape[1] // pps
    qtile = (q0 - kbase) // tk
    use_sel = sel is not None
    in_specs = [
        pl.BlockSpec((None, tq, NSA_QDIM), lambda i, pt: (i, 0, 0)),
        pl.BlockSpec(memory_space=pl.ANY),
        pl.BlockSpec((None, tq, NSA_KVDIM), lambda i, pt: (i, 0, newk)),
        pl.BlockSpec((None, tq, NSA_KVDIM), lambda i, pt: (i, 0, newv)),
        pl.BlockSpec(bias_tab.shape, lambda i, pt: (0, 0, 0, 0)),
        pl.BlockSpec((None, NSA_KV_HEADS, tq, 3 * NSA_GROUP), lambda i, pt: (i, 0, 0, 0)),
    ]
    args = [proj3, pool4, proj3, proj3, bias_tab, gates_r]
    if use_sel:
        nsbp = sel.shape[-1]
        in_specs.append(pl.BlockSpec((None, NSA_KV_HEADS, tq, nsbp), lambda i, pt: (i, 0, 0, 0)))
        args.append(sel)
    kern = functools.partial(_flash_sample_kernel, tq=tq, pps=pps, n_steps=n_steps, plane0=plane0, qtile=qtile,
                             kbase=kbase, q0=q0, use_sel=use_sel, window=window, gcol=gcol)
    rows4 = NSA_GROUP * tq
    grid_spec = pltpu.PrefetchScalarGridSpec(
        num_scalar_prefetch=1, grid=(b,), in_specs=in_specs,
        out_specs=pl.BlockSpec((None, tq, NSA_QDIM), lambda i, pt: (i, 0, 0)),
        scratch_shapes=[pltpu.VMEM((2, pps, 2 * NSA_KV_HEADS, page, NSA_HD), F32),
                        pltpu.SemaphoreType.DMA((2,)),
                        pltpu.VMEM((NSA_KV_HEADS, rows4, NSA_HD), BF16),
                        pltpu.VMEM((NSA_KV_HEADS, rows4, 1), F32),
                        pltpu.VMEM((NSA_KV_HEADS, rows4, 1), F32),
                        pltpu.VMEM((NSA_KV_HEADS, rows4, NSA_HD), F32)])
    return pl.pallas_call(
        kern, grid_spec=grid_spec, out_shape=jax.ShapeDtypeStruct((b, tq, NSA_QDIM), F32),
        compiler_params=pltpu.CompilerParams(dimension_semantics=("arbitrary",), vmem_limit_bytes=VMEM_LIMIT),
        name=name,
    )(table, *args)


def _bias_table(rel_bias, unit, tq, tk):
    n_dd = -(-(T5_SATURATION + tk - 1) // unit) + 1
    dist = (jnp.arange(n_dd)[:, None, None] * unit + jnp.arange(tq)[None, :, None]
            - jnp.arange(tk)[None, None, :])
    return _bucket_lookup(rel_bias, dist)


def _bucket_lookup(rel_bias, dist):
    onehot = jax.nn.one_hot(t5_bucket(dist), N_BUCKETS, dtype=F32)
    out = jnp.einsum('...b,bh->h...', onehot, rel_bias.astype(F32), precision=lax.Precision.HIGHEST)
    return out


def _cmp_tables(rel_bias, q0, t, s_all, nc, n_sb, nsbp):
    c = jnp.arange(s_all)
    c_end = c * CMP_STRIDE + (CMP_BLOCK - 1)
    dist = (q0 + jnp.arange(t))[:, None] - c_end[None, :]
    ok = (dist >= 0) & (c < nc)[None, :]
    bias = jnp.where(ok[None], _bucket_lookup(rel_bias, dist), NEG_INF)
    sb_start = jnp.arange(nsbp) * SEL_BLOCK
    c_start = c * CMP_STRIDE
    ovl = jnp.maximum(jnp.minimum(c_end[:, None], sb_start[None, :] + SEL_BLOCK - 1)
                      - jnp.maximum(c_start[:, None], sb_start[None, :]) + 1, 0).astype(F32) / CMP_BLOCK
    ovl = jnp.where((c < nc)[:, None] & (jnp.arange(nsbp) < n_sb)[None, :], ovl, 0.0)
    return bias, ovl.astype(BF16)


def _nsa_weights(w_in, q_norm, k_norm, cmp_pe, cmp_w1, cmp_w2, rel_bias, w_out):
    n_main = NSA_QDIM + 6 * NSA_KVDIM
    nsub = CMP_BLOCK // CMP_STRIDE
    w1r = cmp_w1.reshape(2, nsub, CMP_STRIDE, NSA_HD, CMP_HIDDEN)
    w1r = jnp.moveaxis(w1r, 1, 3).reshape(2, CMP_STRIDE, NSA_HD, nsub * CMP_HIDDEN).astype(BF16)
    pe_hid = jnp.einsum('ck,cke->ce', cmp_pe.reshape(2, -1), cmp_w1, precision=lax.Precision.HIGHEST)
    return dict(
        w_main=w_in[:, :n_main].astype(BF16),
        w_gate=_pad_cols(w_in[:, n_main:], LANE).astype(BF16),
        post=_nsa_in_post(q_norm, k_norm),
        w1r=w1r, pe_hid=pe_hid.reshape(2, 1, CMP_HIDDEN), w2=cmp_w2.astype(BF16), k_norm0=k_norm[0],
        rel_bias=rel_bias, w_out=w_out.astype(BF16))


def nsa_mixer(h2d, b, t, norm_in, nw, paged):
    post_main, post_gate = nw["post"]
    proj2 = proj(h2d, nw["w_main"], norm_w=norm_in, post=post_main, name="nsa_in")
    gate_out = proj(h2d, nw["w_gate"], norm_w=norm_in, post=post_gate, name="nsa_in_gate")
    proj3 = proj2.reshape(b, t, -1)
    gates_r = gate_out[:, :3 * NSA_HEADS].reshape(b, t, 3, NSA_KV_HEADS, NSA_GROUP)
    gates_r = gates_r.transpose(0, 3, 1, 2, 4).reshape(b, NSA_KV_HEADS, t, 3 * NSA_GROUP)
    kv_blk0 = NSA_QDIM // NSA_KVDIM
    kv_col0 = NSA_QDIM // NSA_HD
    rel_bias = nw["rel_bias"]
    if paged is None:
        p_len, tq = 0, 256
        kvc = nsa_compress(proj3, kv_blk0, nw["w1r"], nw["pe_hid"], nw["w2"], nw["k_norm0"])
    else:
        pool, page_table, win_pool = paged
        p_len, tq = page_table.shape[1] * PAGE_SIZE, t
        kvc = nsa_compress_paged(pool, page_table, nw["w1r"], nw["pe_hid"], nw["w2"], nw["k_norm0"])
    n_all = p_len + t
    s_all = kvc.shape[3]
    nc = n_all // CMP_STRIDE - CMP_BLOCK // CMP_STRIDE + 1
    n_sb = -(-n_all // SEL_BLOCK)
    nsbp = -(-n_sb // LANE) * LANE
    cmp_bias, ovl = _cmp_tables(rel_bias, p_len, t, s_all, nc, n_sb, nsbp)
    o_cmp, sel = nsa_cmp_attention(proj3, kvc, cmp_bias, gates_r, ovl, tq, p_len, n_sb)
    if paged is None:
        tab = _bias_table(rel_bias, tq, tq, tq)
        o_sel = nsa_flash_prompt(proj3, tab, gates_r, sel, tile=tq, kcol=kv_col0 + 2 * NSA_KV_HEADS,
                                 vcol=kv_col0 + 3 * NSA_KV_HEADS, window=None, gcol=NSA_GROUP, name="nsa_sel")
        o_win = nsa_flash_prompt(proj3, tab, gates_r, None, tile=tq, kcol=kv_col0 + 4 * NSA_KV_HEADS,
                                 vcol=kv_col0 + 5 * NSA_KV_HEADS, window=WINDOW, gcol=2 * NSA_GROUP, name="nsa_win")
    else:
        tk = PAGES_PER_STEP * PAGE_SIZE
        assert WINDOW % tk == 0 and p_len % tk == 0
        tab = _bias_table(rel_bias, tk, t, tk)
        o_sel = nsa_flash_sample(proj3, pool, page_table, tab, gates_r, sel, plane0=2 * NSA_KV_HEADS, newk=kv_blk0 + 2,
                                 newv=kv_blk0 + 3, kbase=0, q0=p_len, window=None, gcol=NSA_GROUP,
                                 name="nsa_sel_paged")
        n_wt = WINDOW // PAGE_SIZE
        win_table = jnp.arange(b * n_wt, dtype=jnp.int32).reshape(b, n_wt)
        o_win = nsa_flash_sample(proj3, win_pool, win_table, tab, gates_r, None, plane0=0, newk=kv_blk0 + 4,
                                 newv=kv_blk0 + 5, kbase=p_len - WINDOW, q0=p_len, window=WINDOW,
                                 gcol=2 * NSA_GROUP, name="nsa_win_paged")
    y = proj((o_cmp.reshape(b * t, NSA_QDIM), o_sel.reshape(b * t, NSA_QDIM), o_win.reshape(b * t, NSA_QDIM)),
             nw["w_out"], res=h2d, name="nsa_out")
    new_rows = proj3[:, :, NSA_QDIM:NSA_QDIM + 4 * NSA_KVDIM].reshape(b, t, 4, NSA_KV_HEADS, NSA_HD)
    new_win = proj3[:, :, NSA_QDIM + 4 * NSA_KVDIM:].reshape(b, t, 2, NSA_KV_HEADS, NSA_HD)
    return y, new_rows, new_win


def _pad_cols(w, n_pad):
    return jnp.pad(w, ((0, 0), (0, n_pad - w.shape[1])))


def _nsa_in_post(q_norm, k_norm):
    ones_kv = jnp.ones((NSA_KVDIM,), F32)
    zeros_kv = jnp.zeros((NSA_KVDIM,), F32)
    pw = jnp.concatenate([
        jnp.tile(q_norm.astype(F32) * (NSA_HD ** -0.5), NSA_HEADS),
        ones_kv, ones_kv, jnp.tile(k_norm[1].astype(F32), NSA_KV_HEADS), ones_kv,
        jnp.tile(k_norm[2].astype(F32), NSA_KV_HEADS), ones_kv])
    pm = jnp.concatenate([
        jnp.ones((NSA_QDIM,), F32),
        zeros_kv, zeros_kv, ones_kv, zeros_kv, ones_kv, zeros_kv])
    return (pw, pm), (jnp.ones((LANE,), F32), jnp.full((LANE,), 2.0, F32))


def kernel(x_prompt, x_sample, state_gdn, state_gdn_conv, cache_nsa_kv, state_nsa_win, state_ffn_conv, page_table,
           norm_mix, norm_ffn, gdn_w_in, gdn_conv_w, gdn_A_log, gdn_dt_bias, gdn_norm, gdn_w_out,
           nsa_w_in, nsa_q_norm, nsa_k_norm, nsa_cmp_pe, nsa_cmp_w1, nsa_cmp_w2, rel_bias, nsa_w_out,
           ffn_w_up, ffn_conv_w, ffn_conv_b, ffn_w_down):
    depth = norm_mix.shape[0]
    bp, tp, d = x_prompt.shape
    bs, ts, _ = x_sample.shape
    win_buf = state_nsa_win.shape[2]
    assert win_buf == WINDOW and tp >= WINDOW and cache_nsa_kv.shape[2] == PAGE_SIZE
    d_ff = ffn_conv_w.shape[1]
    conv_keep = ffn_conv_w.shape[2] - 1
    hp = x_prompt.reshape(bp * tp, d)
    hs = x_sample.reshape(bs * ts, d)
    gdn_p, gdnc_p, kv_p, win_p, ffn_p = [], [], [], [], []
    gdn_s, gdnc_s, kv_s, win_s, ffn_s = [], [], [], [], []
    for i in range(depth):
        j = i // 2
        if i % 2 == 0:
            n_main = GDN_CONV_DIM + GDN_VDIM
            w_in = (gdn_w_in[j][:, :n_main].astype(BF16), _pad_cols(gdn_w_in[j][:, n_main:], LANE).astype(BF16))
            gw = (norm_mix[i], w_in, gdn_conv_w[j], gdn_A_log[j], gdn_dt_bias[j], gdn_norm[j],
                  gdn_w_out[j].astype(BF16))
            hp, st_p, cv_p = gdn_mixer(hp, bp, tp, jnp.zeros((bp, GDN_V_HEADS, GDN_DK, GDN_DV), F32),
                                       jnp.zeros((bp, gdn_conv_w.shape[2] - 1, GDN_CONV_DIM), F32), *gw)
            hs, st_s, cv_s = gdn_mixer(hs, bs, ts, state_gdn[j], state_gdn_conv[j], *gw)
            gdn_p.append(st_p)
            gdnc_p.append(cv_p)
            gdn_s.append(st_s)
            gdnc_s.append(cv_s)
        else:
            nw = _nsa_weights(nsa_w_in[j], nsa_q_norm[j], nsa_k_norm[j], nsa_cmp_pe[j], nsa_cmp_w1[j],
                              nsa_cmp_w2[j], rel_bias, nsa_w_out[j])
            hp, rows_p, nwin_p = nsa_mixer(hp, bp, tp, norm_mix[i], nw, None)
            n_pool = cache_nsa_kv.shape[1]
            pool = cache_nsa_kv.reshape(cache_nsa_kv.shape[0] * n_pool, PAGE_SIZE, 4 * NSA_KV_HEADS, NSA_HD)
            win_pool = state_nsa_win[j].reshape(bs * (win_buf // PAGE_SIZE), PAGE_SIZE, 2 * NSA_KV_HEADS, NSA_HD)
            hs, rows_s, nwin_s = nsa_mixer(hs, bs, ts, norm_mix[i], nw, (pool, page_table + j * n_pool, win_pool))
            kv_p.append(rows_p)
            win_p.append(nwin_p[:, tp - WINDOW:])
            kv_s.append(rows_s)
            win_s.append(jnp.concatenate([state_nsa_win[j][:, ts:], nwin_s], axis=1))
        n_up = -(-ffn_w_up.shape[2] // 512) * 512
        fw = (norm_ffn[i], _pad_cols(ffn_w_up[i], n_up).astype(BF16), ffn_conv_w[i], ffn_conv_b[i], ffn_w_down[i].astype(BF16))
        hp, cp = conv_ffn(hp, bp, tp, jnp.zeros((bp, conv_keep, d_ff), F32), *fw)
        hs, cs = conv_ffn(hs, bs, ts, state_ffn_conv[i], *fw)
        ffn_p.append(cp)
        ffn_s.append(cs)
    return (hp.reshape(bp, tp, d), hs.reshape(bs, ts, d),
            jnp.stack(gdn_p), jnp.stack(gdnc_p), jnp.stack(kv_p), jnp.stack(win_p), jnp.stack(ffn_p),
            jnp.stack(gdn_s), jnp.stack(gdnc_s), jnp.stack(kv_s), jnp.stack(win_s), jnp.stack(ffn_s))
```

```python
import functools
import math

import jax
import jax.numpy as jnp
from jax import lax
from jax.experimental import pallas as pl
from jax.experimental.pallas import tpu as pltpu

F32 = jnp.float32
BF16 = jnp.bfloat16
RMS_EPS = 1e-6
NEG_INF = -1e30

LANE = 128
SUBLANE = 8
VMEM_LIMIT = 56 * 1024 * 1024
PROJ_VMEM_BUDGET = 40 * 1024 * 1024
FFN_DOWN_ROWS = 256

D_MODEL = 2048
GDN_K_HEADS = 16
GDN_V_HEADS = 32
GDN_DK = 128
GDN_DV = 128
GDN_KDIM = GDN_K_HEADS * GDN_DK
GDN_VDIM = GDN_V_HEADS * GDN_DV
GDN_CONV_DIM = 2 * GDN_KDIM + GDN_VDIM
GDN_CHUNK = 64

NSA_HEADS = 16
NSA_KV_HEADS = 4
NSA_HD = 128
NSA_GROUP = NSA_HEADS // NSA_KV_HEADS
NSA_QDIM = NSA_HEADS * NSA_HD
NSA_KVDIM = NSA_KV_HEADS * NSA_HD
CMP_BLOCK = 32
CMP_STRIDE = 16
CMP_HIDDEN = 2 * NSA_HD
SEL_BLOCK = 64
SEL_SHIFT = 6
N_SEL = 16
WINDOW = 512
FORCE_BONUS = 1e3
N_BUCKETS = 32
REL_MAX_DIST = 1024
PAGE_SIZE = 128
PAGES_PER_STEP = 4
SEL_PAGES_PER_STEP = 8
T5_SATURATION = 790
CMP_TOPK_ROWS = 256
NT_DIMS = (((1,), (1,)), ((), ()))


def _proj_kernel(*refs, n_x, has_norm, has_res, has_post):
    it = iter(refs)
    x_refs = [next(it) for _ in range(n_x)]
    nw_ref = next(it) if has_norm else None
    w_ref = next(it)
    res_ref = next(it) if has_res else None
    pw_ref = next(it) if has_post else None
    pm_ref = next(it) if has_post else None
    o_ref = next(it)
    xs_ref = next(it)

    @pl.when(pl.program_id(1) == 0)
    def _():
        x = x_refs[0][...].astype(F32)
        for r in x_refs[1:]:
            x = x + r[...].astype(F32)
        if has_norm:
            ms = jnp.mean(x * x, axis=-1, keepdims=True)
            x = x * lax.rsqrt(ms + RMS_EPS) * nw_ref[...]
        xs_ref[...] = x.astype(BF16)

    y = jnp.dot(xs_ref[...], w_ref[...], preferred_element_type=F32)
    if has_res:
        y = y + res_ref[...]
    if has_post:
        tn = y.shape[1]
        for g in range(tn // LANE):
            sl = slice(g * LANE, (g + 1) * LANE)
            yg = y[:, sl]
            mode = pm_ref[:, sl]
            ms = jnp.mean(yg * yg, axis=-1, keepdims=True)
            normed = yg * lax.rsqrt(ms + RMS_EPS) * pw_ref[:, sl]
            sig = jax.nn.sigmoid(yg)
            o_ref[:, sl] = jnp.where(mode == 1.0, normed, jnp.where(mode == 2.0, sig, yg))
    else:
        o_ref[...] = y.astype(o_ref.dtype)


def _pick_tile(n, candidates):
    for c in candidates:
        if n % c == 0:
            return c
    return n


def proj(xs, w, *, norm_w=None, res=None, post=None, out_dtype=F32, name="proj"):
    if not isinstance(xs, (tuple, list)):
        xs = (xs,)
    m, k = xs[0].shape
    n = w.shape[1]
    tn = _pick_tile(n, (512, 384, 256, 128))
    x_row_bytes = sum(k * x.dtype.itemsize for x in xs)

    def vmem_bytes(tm):
        return 2 * (tm * x_row_bytes + k * tn * 2 + tm * tn * 4 * (2 if res is not None else 1)) + tm * k * 2

    tm = next((c for c in (1024, 512, 256) if m % c == 0 and vmem_bytes(c) <= PROJ_VMEM_BUDGET), m)
    in_specs = [pl.BlockSpec((tm, k), lambda i, j: (i, 0)) for _ in xs]
    args = list(xs)
    if norm_w is not None:
        in_specs.append(pl.BlockSpec((1, k), lambda i, j: (0, 0)))
        args.append(norm_w.reshape(1, k).astype(F32))
    in_specs.append(pl.BlockSpec((k, tn), lambda i, j: (0, j)))
    args.append(w)
    if res is not None:
        in_specs.append(pl.BlockSpec((tm, tn), lambda i, j: (i, j)))
        args.append(res)
    if post is not None:
        for a in post:
            in_specs.append(pl.BlockSpec((1, tn), lambda i, j: (0, j)))
            args.append(a.reshape(1, n).astype(F32))
    kern = functools.partial(_proj_kernel, n_x=len(xs), has_norm=norm_w is not None, has_res=res is not None,
                             has_post=post is not None)
    return pl.pallas_call(
        kern,
        grid=(m // tm, n // tn),
        in_specs=in_specs,
        out_specs=pl.BlockSpec((tm, tn), lambda i, j: (i, j)),
        out_shape=jax.ShapeDtypeStruct((m, n), out_dtype),
        scratch_shapes=[pltpu.VMEM((tm, k), BF16)],
        compiler_params=pltpu.CompilerParams(dimension_semantics=("parallel", "arbitrary"),
                                             vmem_limit_bytes=VMEM_LIMIT),
        name=name,
    )(*args)


def _conv_gate(pad_ref, prev, gate, val, cw, cb, width):
    t = gate.shape[0]
    halo = SUBLANE
    pad_ref[halo - (width - 1):halo, :] = prev
    pad_ref[halo:halo + t, :] = gate
    acc = cb + pad_ref[halo - (width - 1):halo - (width - 1) + t, :] * cw[0:1, :]
    for i in range(1, width):
        off = halo - (width - 1) + i
        acc = acc + pad_ref[off:off + t, :] * cw[i:i + 1, :]
    return jax.nn.silu(acc) * val


def _ffn_gate_kernel(gate_ref, val_ref, pre_ref, cw_ref, cb_ref, o_ref, pad_ref, *, width):
    for i in range(gate_ref.shape[0]):
        hid = _conv_gate(pad_ref, pre_ref[i], gate_ref[i], val_ref[i], cw_ref[...], cb_ref[...], width)
        o_ref[i] = hid.astype(o_ref.dtype)


def ffn_gate(up, prefix, conv_w, conv_b, d_ff):
    b, t, _ = up.shape
    width = conv_w.shape[1]
    nblk = d_ff // LANE
    kern = functools.partial(_ffn_gate_kernel, width=width)
    return pl.pallas_call(
        kern,
        grid=(nblk,),
        in_specs=[
            pl.BlockSpec((b, t, LANE), lambda j: (0, 0, j)),
            pl.BlockSpec((b, t, LANE), lambda j: (0, 0, j + nblk)),
            pl.BlockSpec((b, width - 1, LANE), lambda j: (0, 0, j)),
            pl.BlockSpec((width, LANE), lambda j: (0, j)),
            pl.BlockSpec((1, LANE), lambda j: (0, j)),
        ],
        out_specs=pl.BlockSpec((b, t, LANE), lambda j: (0, 0, j)),
        out_shape=jax.ShapeDtypeStruct((b, t, d_ff), BF16),
        scratch_shapes=[pltpu.VMEM((t + SUBLANE, LANE), F32)],
        compiler_params=pltpu.CompilerParams(dimension_semantics=("parallel",)),
        name="ffn_gate",
    )(up, up, prefix, conv_w.T, conv_b.reshape(1, d_ff))


def _ffn_down_kernel(gate_ref, val_ref, halo_ref, pre_ref, cw_ref, cb_ref, w_ref, res_ref, o_ref, hid_ref, pad_ref, *,
                     width, tiles_per_batch):
    @pl.when(pl.program_id(1) == 0)
    def _():
        first = (pl.program_id(0) % tiles_per_batch) == 0

        def column_block(c, carry):
            cs = pl.ds(pl.multiple_of(c * LANE, LANE), LANE)
            prev = jnp.where(first, pre_ref[:, cs], halo_ref[SUBLANE - (width - 1):SUBLANE, cs])
            hid = _conv_gate(pad_ref, prev, gate_ref[:, cs], val_ref[:, cs], cw_ref[:, cs], cb_ref[:, cs], width)
            hid_ref[:, cs] = hid.astype(BF16)
            return carry

        lax.fori_loop(0, gate_ref.shape[1] // LANE, column_block, 0)

    o_ref[...] = jnp.dot(hid_ref[...], w_ref[...], preferred_element_type=F32) + res_ref[...]


def ffn_down_fused(up2d, b, t, prefix, conv_w, conv_b, w_down, res):
    d_ff, n = w_down.shape
    width = conv_w.shape[1]
    tm, tn = FFN_DOWN_ROWS, 512
    assert t % tm == 0 and n % tn == 0
    tiles_per_batch = t // tm
    kern = functools.partial(_ffn_down_kernel, width=width, tiles_per_batch=tiles_per_batch)
    return pl.pallas_call(
        kern,
        grid=(b * tiles_per_batch, n // tn),
        in_specs=[
            pl.BlockSpec((tm, d_ff), lambda i, j: (i, 0)),
            pl.BlockSpec((tm, d_ff), lambda i, j: (i, 1)),
            pl.BlockSpec((SUBLANE, d_ff), lambda i, j: (jnp.maximum(i * (tm // SUBLANE) - 1, 0), 0)),
            pl.BlockSpec((None, width - 1, d_ff), lambda i, j: (i // tiles_per_batch, 0, 0)),
            pl.BlockSpec((width, d_ff), lambda i, j: (0, 0)),
            pl.BlockSpec((1, d_ff), lambda i, j: (0, 0)),
            pl.BlockSpec((d_ff, tn), lambda i, j: (0, j)),
            pl.BlockSpec((tm, tn), lambda i, j: (i, j)),
        ],
        out_specs=pl.BlockSpec((tm, tn), lambda i, j: (i, j)),
        out_shape=jax.ShapeDtypeStruct((b * t, n), F32),
        scratch_shapes=[pltpu.VMEM((tm, d_ff), BF16), pltpu.VMEM((tm + SUBLANE, LANE), F32)],
        compiler_params=pltpu.CompilerParams(dimension_semantics=("parallel", "arbitrary"),
                                             vmem_limit_bytes=VMEM_LIMIT),
        name="ffn_down_fused",
    )(up2d, up2d, up2d, prefix, conv_w.T, conv_b.reshape(1, d_ff), w_down, res)


def conv_ffn(h2d, b, t, prefix, norm_w, w_up, conv_w, conv_b, w_down):
    d_ff = conv_w.shape[0]
    keep = conv_w.shape[1] - 1
    assert t >= keep
    up2d = proj(h2d, w_up, norm_w=norm_w, name="ffn_up")
    up = up2d.reshape(b, t, -1)
    new_prefix = up[:, t - keep:, :d_ff]
    if t % FFN_DOWN_ROWS == 0:
        return ffn_down_fused(up2d, b, t, prefix, conv_w, conv_b, w_down, h2d), new_prefix
    hidden = ffn_gate(up, prefix, conv_w, conv_b, d_ff)
    out = proj(hidden.reshape(b * t, d_ff), w_down, res=h2d, name="ffn_down")
    return out, new_prefix


def t5_bucket(dist):
    d = jnp.maximum(dist, 0)
    max_exact = N_BUCKETS // 2
    scale = (N_BUCKETS - max_exact) / math.log(REL_MAX_DIST / max_exact)
    large = max_exact + (jnp.log(jnp.maximum(d, 1).astype(F32) / max_exact) * scale).astype(jnp.int32)
    return jnp.where(d < max_exact, d, jnp.minimum(large, N_BUCKETS - 1))


GDN_CONV_COLS = 512
GDN_CONV_ROWS = 2048
GDN_HEADS_PER_STEP = 8
GDN_GROUP = 4


def _gdn_conv_kernel(x_ref, pre_ref, cw_ref, o_ref, pad_ref, *, width, n_q_blocks, n_qk_blocks):
    j = pl.program_id(1)
    t = x_ref.shape[1]
    halo = SUBLANE
    scale = jnp.where(j < n_q_blocks, GDN_DK ** -0.5, 1.0)
    is_qk = j < n_qk_blocks
    for bi in range(x_ref.shape[0]):
        pad_ref[halo - (width - 1):halo, :] = pre_ref[bi]
        pad_ref[halo:halo + t, :] = x_ref[bi]
        acc = pad_ref[halo - (width - 1):halo - (width - 1) + t, :] * cw_ref[0:1, :]
        for i in range(1, width):
            off = halo - (width - 1) + i
            acc = acc + pad_ref[off:off + t, :] * cw_ref[i:i + 1, :]
        y = jax.nn.silu(acc)
        for h in range(x_ref.shape[2] // LANE):
            seg = y[:, h * LANE:(h + 1) * LANE]
            nrm = seg * lax.rsqrt(jnp.sum(seg * seg, axis=-1, keepdims=True) + RMS_EPS) * scale
            o_ref[bi, :, h * LANE:(h + 1) * LANE] = jnp.where(is_qk, nrm, seg)


def gdn_conv(proj3, prefix, conv_w):
    b, t, _ = proj3.shape
    width = conv_w.shape[1]
    cols = GDN_CONV_COLS
    bb = max(1, min(b, GDN_CONV_ROWS // t))
    assert b % bb == 0
    kern = functools.partial(_gdn_conv_kernel, width=width, n_q_blocks=GDN_KDIM // cols,
                             n_qk_blocks=2 * GDN_KDIM // cols)
    return pl.pallas_call(
        kern,
        grid=(b // bb, GDN_CONV_DIM // cols),
        in_specs=[
            pl.BlockSpec((bb, t, cols), lambda i, j: (i, 0, j)),
            pl.BlockSpec((bb, width - 1, cols), lambda i, j: (i, 0, j)),
            pl.BlockSpec((width, cols), lambda i, j: (0, j)),
        ],
        out_specs=pl.BlockSpec((bb, t, cols), lambda i, j: (i, 0, j)),
        out_shape=jax.ShapeDtypeStruct((b, t, GDN_CONV_DIM), F32),
        scratch_shapes=[pltpu.VMEM((t + SUBLANE, cols), F32)],
        compiler_params=pltpu.CompilerParams(dimension_semantics=("parallel", "parallel"),
                                             vmem_limit_bytes=VMEM_LIMIT),
        name="gdn_conv",
    )(proj3, prefix, conv_w.T)


def _split_bf16(x):
    hi = x.astype(BF16)
    return hi, (x - hi.astype(F32)).astype(BF16)


def _dot_split(a_parts, b_parts):
    (ah, al), (bh, bl) = a_parts, b_parts
    return (jnp.dot(ah, bh, preferred_element_type=F32) + jnp.dot(ah, bl, preferred_element_type=F32)
            + jnp.dot(al, bh, preferred_element_type=F32))


def _gdn_gate_kernel(x_ref, a_ref, dt_ref, o_ref, *, t_real):
    L = GDN_CHUNK
    lane = lax.broadcasted_iota(jnp.int32, (L, LANE), 1)
    row = lax.broadcasted_iota(jnp.int32, (L, LANE), 0)
    ci = lax.broadcasted_iota(jnp.int32, (L, L), 0)
    cj = lax.broadcasted_iota(jnp.int32, (L, L), 1)
    tril = jnp.where(ci >= cj, 1.0, 0.0).astype(BF16)
    neg_a = -jnp.exp(a_ref[...])

    def chunk(c, carry):
        r0 = pl.multiple_of(c * L, L)
        x = x_ref[pl.ds(r0, L), :]
        live = (row + r0) < t_real
        beta = jnp.where(live, jax.nn.sigmoid(x), 0.0)
        z = x + dt_ref[...]
        softplus = jnp.maximum(z, 0.0) + jnp.log(1.0 + jnp.exp(-jnp.abs(z)))
        g = jnp.where(live, neg_a * softplus, 0.0)
        hi, mid = _split_bf16(g)
        lo = (g - hi.astype(F32) - mid.astype(F32)).astype(BF16)
        cum = (jnp.dot(tril, hi, preferred_element_type=F32) + jnp.dot(tril, mid, preferred_element_type=F32)
               + jnp.dot(tril, lo, preferred_element_type=F32))
        o_ref[pl.ds(r0, L), :] = jnp.where(lane < GDN_V_HEADS, beta, cum)
        return carry

    lax.fori_loop(0, x_ref.shape[0] // L, chunk, 0)


def gdn_gates(gate3, a_log, dt_bias, t_real):
    b, t, _ = gate3.shape
    pad = jnp.zeros((GDN_V_HEADS,), F32)
    a_vec = jnp.concatenate([pad, a_log.astype(F32), pad, pad]).reshape(1, LANE)
    dt_vec = jnp.concatenate([pad, dt_bias.astype(F32), pad, pad]).reshape(1, LANE)
    return pl.pallas_call(
        functools.partial(_gdn_gate_kernel, t_real=t_real),
        grid=(b,),
        in_specs=[pl.BlockSpec((None, t, LANE), lambda i: (i, 0, 0)),
                  pl.BlockSpec((1, LANE), lambda i: (0, 0)),
                  pl.BlockSpec((1, LANE), lambda i: (0, 0))],
        out_specs=pl.BlockSpec((None, t, LANE), lambda i: (i, 0, 0)),
        out_shape=jax.ShapeDtypeStruct((b, t, LANE), F32),
        compiler_params=pltpu.CompilerParams(dimension_semantics=("parallel",)),
        name="gdn_gates",
    )(gate3, a_vec, dt_vec)


def _pad_rows(x, rows):
    if x.shape[0] == rows:
        return x
    return jnp.concatenate([x, jnp.zeros((rows - x.shape[0], x.shape[1]), x.dtype)], axis=0)


def _gdn_delta_kernel(q_ref, k_ref, v_ref, z_ref, gb_ref, gt_ref, s0_ref, nw_ref, o_ref, s_out_ref, s_ref, *, nc, tl):
    hg = pl.program_id(1)
    c = pl.program_id(2)
    L = GDN_CHUNK
    hb = GDN_HEADS_PER_STEP

    @pl.when(c == 0)
    def _():
        s_ref[...] = s0_ref[...]

    gs = GDN_GROUP
    n_groups = hb // gs
    rows = gs * L
    shift = L.bit_length() - 1
    gb = gb_ref[...]
    lane = lax.broadcasted_iota(jnp.int32, (L, LANE), 1)
    ri = lax.broadcasted_iota(jnp.int32, (rows, rows), 0)
    cj = lax.broadcasted_iota(jnp.int32, (rows, rows), 1)
    same = (ri >> shift) == (cj >> shift)
    strict = same & (ri > cj)
    incl = same & (ri >= cj)

    def stack(ref, width_of):
        return [jnp.concatenate([_pad_rows(ref[:, width_of(hh) * LANE:(width_of(hh) + 1) * LANE], L)
                                 for hh in range(g * gs, (g + 1) * gs)], axis=0) for g in range(n_groups)]

    q = stack(q_ref, lambda hh: hh // 2)
    k = stack(k_ref, lambda hh: hh // 2)
    v = stack(v_ref, lambda hh: hh)
    bcol, gcol, grow, eg, a_qk, pw, x = [], [], [], [], [], [], []
    for g in range(n_groups):
        heads = [hg * hb + g * gs + hh for hh in range(gs)]
        bcol.append(jnp.concatenate(
            [jnp.sum(jnp.where(lane == h, gb, 0.0), axis=-1, keepdims=True) for h in heads], axis=0))
        gcol.append(jnp.concatenate(
            [jnp.sum(jnp.where(lane == h + GDN_V_HEADS, gb, 0.0), axis=-1, keepdims=True) for h in heads], axis=0))
        grow.append(gt_ref[pl.ds(c, 1), g * rows:(g + 1) * rows])
        decay = jnp.exp(jnp.where(same, gcol[g] - grow[g], 0.0))
        kb = k[g].astype(BF16)
        kk = lax.dot_general(kb, kb, NT_DIMS, preferred_element_type=F32)
        qk = lax.dot_general(q[g].astype(BF16), kb, NT_DIMS, preferred_element_type=F32)
        a_qk.append((qk * jnp.where(incl, decay, 0.0)).astype(BF16))
        eg.append(jnp.exp(gcol[g]))
        pw.append(_split_bf16(-(bcol[g] * kk * jnp.where(strict, decay, 0.0))))
        x.append(jnp.concatenate([bcol[g] * v[g], (bcol[g] * eg[g]) * k[g]], axis=1))
    for i in range(shift):
        for g in range(n_groups):
            x[g] = x[g] + _dot_split(pw[g], _split_bf16(x[g]))
        if i + 1 < shift:
            for g in range(n_groups):
                pw[g] = _split_bf16(_dot_split(pw[g], pw[g]))
    for g in range(n_groups):
        u_eff, w_kb = x[g][:, :GDN_DV], x[g][:, GDN_DV:].astype(BF16)
        sbs = [s_ref[g * gs + hh].astype(BF16) for hh in range(gs)]
        u = jnp.concatenate(
            [u_eff[hh * L:(hh + 1) * L] - jnp.dot(w_kb[hh * L:(hh + 1) * L], sbs[hh], preferred_element_type=F32)
             for hh in range(gs)], axis=0)
        ub = u.astype(BF16)
        o_intra = jnp.dot(a_qk[g], ub, preferred_element_type=F32)
        q_dec = (q[g] * eg[g]).astype(BF16)
        for hh in range(gs):
            hr = slice(hh * L, (hh + 1) * L)
            hi = g * gs + hh
            o = jnp.dot(q_dec[hr], sbs[hh], preferred_element_type=F32) + o_intra[hr]
            g_last = grow[g][:, hh * L + L - 1:hh * L + L]
            k_dec = (k[g][hr] * jnp.exp(g_last - gcol[g][hr])).astype(BF16)
            s_ref[hi] = jnp.exp(g_last) * s_ref[hi] + lax.dot_general(k_dec, ub[hr], (((0,), (0,)), ((), ())),
                                                                      preferred_element_type=F32)
            on = o * lax.rsqrt(jnp.mean(o * o, axis=-1, keepdims=True) + RMS_EPS) * nw_ref[...]
            gated = on[:tl] * jax.nn.silu(z_ref[:, hi * LANE:(hi + 1) * LANE])
            o_ref[:, hi * LANE:(hi + 1) * LANE] = gated.astype(o_ref.dtype)

    @pl.when(c == nc - 1)
    def _():
        s_out_ref[...] = s_ref[...]


def gdn_delta(qkv, proj3, gb, gt, s0, norm_w, tl):
    b, t, _ = qkv.shape
    nc = gb.shape[1] // GDN_CHUNK
    hb = GDN_HEADS_PER_STEP
    kw = (hb // 2) * GDN_DK
    vw = hb * GDN_DV
    kern = functools.partial(_gdn_delta_kernel, nc=nc, tl=tl)
    return pl.pallas_call(
        kern,
        grid=(b, GDN_V_HEADS // hb, nc),
        in_specs=[
            pl.BlockSpec((None, tl, kw), lambda i, h, c: (i, c, h)),
            pl.BlockSpec((None, tl, kw), lambda i, h, c: (i, c, GDN_KDIM // kw + h)),
            pl.BlockSpec((None, tl, vw), lambda i, h, c: (i, c, 2 * GDN_KDIM // vw + h)),
            pl.BlockSpec((None, tl, vw), lambda i, h, c: (i, c, GDN_CONV_DIM // vw + h)),
            pl.BlockSpec((None, GDN_CHUNK, LANE), lambda i, h, c: (i, c, 0)),
            pl.BlockSpec((None, None, nc, hb * GDN_CHUNK), lambda i, h, c: (i, h, 0, 0)),
            pl.BlockSpec((None, hb, GDN_DK, GDN_DV), lambda i, h, c: (i, h, 0, 0)),
            pl.BlockSpec((1, GDN_DV), lambda i, h, c: (0, 0)),
        ],
        out_specs=[
            pl.BlockSpec((None, tl, vw), lambda i, h, c: (i, c, h)),
            pl.BlockSpec((None, hb, GDN_DK, GDN_DV), lambda i, h, c: (i, h, 0, 0)),
        ],
        out_shape=[jax.ShapeDtypeStruct((b, t, GDN_VDIM), BF16),
                   jax.ShapeDtypeStruct(s0.shape, F32)],
        scratch_shapes=[pltpu.VMEM((hb, GDN_DK, GDN_DV), F32)],
        compiler_params=pltpu.CompilerParams(dimension_semantics=("parallel", "parallel", "arbitrary"),
                                             vmem_limit_bytes=VMEM_LIMIT),
        name="gdn_delta",
    )(qkv, qkv, qkv, proj3, gb, gt, s0, norm_w.reshape(1, GDN_DV).astype(F32))


def gdn_mixer(h2d, b, t, s0, conv_prefix, norm_in, w_in, conv_w, a_log, dt_bias, norm_w, w_out):
    w_main, w_gate = w_in
    proj3 = proj(h2d, w_main, norm_w=norm_in, name="gdn_in").reshape(b, t, -1)
    gate3 = proj(h2d, w_gate, norm_w=norm_in, name="gdn_in_gate").reshape(b, t, -1)
    keep = conv_w.shape[1] - 1
    assert t >= keep
    new_prefix = proj3[:, t - keep:, :GDN_CONV_DIM]
    qkv = gdn_conv(proj3, conv_prefix, conv_w)
    tl = min(t, GDN_CHUNK)
    t_pad = -(-t // GDN_CHUNK) * GDN_CHUNK
    gb = gdn_gates(jnp.pad(gate3, ((0, 0), (0, t_pad - t), (0, 0))), a_log, dt_bias, t)
    nc = t_pad // GDN_CHUNK
    hb = GDN_HEADS_PER_STEP
    gt = gb[:, :, GDN_V_HEADS:2 * GDN_V_HEADS].reshape(b, nc, GDN_CHUNK, GDN_V_HEADS // hb, hb)
    gt = gt.transpose(0, 3, 1, 4, 2).reshape(b, GDN_V_HEADS // hb, nc, hb * GDN_CHUNK)
    o, s_new = gdn_delta(qkv, proj3, gb, gt, s0.astype(F32), norm_w, tl)
    y = proj(o.reshape(b * t, GDN_VDIM), w_out, res=h2d, name="gdn_out")
    return y, s_new, new_prefix


def _compress_mlp(get_x, w1_at, pe, w2, k_norm, acc_ref, s_all):
    acc_ref[0:s_all, :] = jnp.zeros((s_all, 2 * CMP_HIDDEN), F32)

    def body(r, carry):
        acc_ref[0:s_all, :] += jnp.dot(get_x(r), w1_at(r), preferred_element_type=F32)
        return carry

    lax.fori_loop(0, CMP_STRIDE, body, 0)
    hid = acc_ref[0:s_all, 0:CMP_HIDDEN] + acc_ref[1:s_all + 1, CMP_HIDDEN:2 * CMP_HIDDEN] + pe
    y = jnp.dot(jax.nn.gelu(hid).astype(BF16), w2, preferred_element_type=F32)
    if k_norm is None:
        return y
    return y * lax.rsqrt(jnp.mean(y * y, axis=-1, keepdims=True) + RMS_EPS) * k_norm


def _compress_kernel(*refs, s_all):
    x_refs = refs[:NSA_KV_HEADS]
    w1_ref, pe_ref, w2_ref, nw_ref, o_ref, xs_ref, acc_ref = refs[NSA_KV_HEADS:]
    for r in range(CMP_STRIDE):
        for g in range(NSA_KV_HEADS):
            xs_ref[r, g] = x_refs[g][pl.ds(r, s_all, stride=CMP_STRIDE), :].astype(BF16)
    is_k = pl.program_id(1) == 0
    acc_ref[s_all:s_all + SUBLANE, :] = jnp.zeros((SUBLANE, 2 * CMP_HIDDEN), F32)
    for g in range(NSA_KV_HEADS):
        y = _compress_mlp(lambda r: xs_ref[r, g], lambda r: w1_ref[r], pe_ref[...], w2_ref[...], None, acc_ref,
                          s_all)
        normed = y * lax.rsqrt(jnp.mean(y * y, axis=-1, keepdims=True) + RMS_EPS) * nw_ref[...]
        o_ref[g] = jnp.where(is_k, normed, y)


def nsa_compress(x, col0, w1r, pe_hid, w2, k_norm0):
    b, rows, _ = x.shape
    s_all = rows // CMP_STRIDE

    def x_spec(g):
        return pl.BlockSpec((None, rows, NSA_HD), lambda i, c: (i, 0, (col0 + c) * NSA_KV_HEADS + g))

    in_specs = [x_spec(g) for g in range(NSA_KV_HEADS)] + [
        pl.BlockSpec((None, CMP_STRIDE, NSA_HD, 2 * CMP_HIDDEN), lambda i, c: (c, 0, 0, 0)),
        pl.BlockSpec((None, 1, CMP_HIDDEN), lambda i, c: (c, 0, 0)),
        pl.BlockSpec((None, CMP_HIDDEN, NSA_HD), lambda i, c: (c, 0, 0)),
        pl.BlockSpec((1, NSA_HD), lambda i, c: (0, 0)),
    ]
    return pl.pallas_call(
        functools.partial(_compress_kernel, s_all=s_all),
        grid=(b, 2),
        in_specs=in_specs,
        out_specs=pl.BlockSpec((None, None, NSA_KV_HEADS, s_all, NSA_HD), lambda i, c: (i, c, 0, 0, 0)),
        out_shape=jax.ShapeDtypeStruct((b, 2, NSA_KV_HEADS, s_all, NSA_HD), F32),
        scratch_shapes=[pltpu.VMEM((CMP_STRIDE, NSA_KV_HEADS, s_all, NSA_HD), BF16),
                        pltpu.VMEM((s_all + SUBLANE, 2 * CMP_HIDDEN), F32)],
        compiler_params=pltpu.CompilerParams(dimension_semantics=("parallel", "parallel"),
                                             vmem_limit_bytes=VMEM_LIMIT),
        name="nsa_compress",
    )(*((x,) * NSA_KV_HEADS), w1r, pe_hid, w2, k_norm0.reshape(1, NSA_HD).astype(F32))


CMP_FINAL_ROWS = 512


def _compress_paged_kernel(*refs, n_steps, pps):
    x_refs = refs[1:1 + pps]
    w1_ref, pe_ref, w2_ref, nw_ref, o_ref, acc_ref = refs[1 + pps:]
    p = pl.program_id(1)
    planes = 2 * NSA_KV_HEADS
    sp = PAGE_SIZE // CMP_STRIDE
    rows_pp = sp * planes
    n_rows = n_steps * pps * rows_pp
    hid_w = CMP_HIDDEN
    lhs = jnp.concatenate(
        [jnp.concatenate([x[pl.ds(r, sp, stride=CMP_STRIDE), :, :].reshape(rows_pp, NSA_HD).astype(BF16)
                          for r in range(CMP_STRIDE)], axis=1) for x in x_refs], axis=0)
    row0 = pl.multiple_of(p * (pps * rows_pp), pps * rows_pp)
    acc_ref[pl.ds(row0, pps * rows_pp), :] = jnp.dot(lhs, w1_ref[...], preferred_element_type=F32)

    @pl.when(p == n_steps - 1)
    def _():
        acc_ref[n_rows:n_rows + planes, :] = jnp.zeros((planes, 4 * hid_w), F32)
        cr = CMP_FINAL_ROWS
        is_k = (lax.broadcasted_iota(jnp.int32, (cr, 1), 0) % planes) < NSA_KV_HEADS

        def chunk(ci, carry):
            r0 = pl.multiple_of(ci * cr, cr)
            a = acc_ref[pl.ds(r0, cr), :]
            nxt = acc_ref[pl.ds(r0 + planes, cr), :]
            hid_k = a[:, 0:hid_w] + nxt[:, hid_w:2 * hid_w] + pe_ref[0]
            hid_v = a[:, 2 * hid_w:3 * hid_w] + nxt[:, 3 * hid_w:4 * hid_w] + pe_ref[1]
            act = jax.nn.gelu(jnp.where(is_k, hid_k, hid_v)).astype(BF16)
            y2 = jnp.dot(act, w2_ref[...], preferred_element_type=F32)
            y = jnp.where(is_k, y2[:, :NSA_HD], y2[:, NSA_HD:])
            normed = y * lax.rsqrt(jnp.mean(y * y, axis=-1, keepdims=True) + RMS_EPS) * nw_ref[...]
            o_ref[pl.ds(r0, cr), :] = jnp.where(is_k, normed, y)
            return carry

        lax.fori_loop(0, n_rows // cr, chunk, 0)


def nsa_compress_paged(pool4, page_table, w1r, pe_hid, w2, k_norm0):
    b, n_pages = page_table.shape
    pps = PAGES_PER_STEP
    assert n_pages % pps == 0
    n_steps = n_pages // pps
    planes = 2 * NSA_KV_HEADS
    s_all = n_pages * (PAGE_SIZE // CMP_STRIDE)
    n_rows = s_all * planes
    assert n_rows % CMP_FINAL_ROWS == 0
    w1cat = w1r.transpose(1, 2, 0, 3).reshape(CMP_STRIDE * NSA_HD, 4 * CMP_HIDDEN)
    w2cat = jnp.concatenate([w2[0], w2[1]], axis=1)

    def page_spec(u):
        return pl.BlockSpec((None, PAGE_SIZE, planes, NSA_HD), lambda i, p, pt: (pt[i, p * pps + u], 0, 0, 0))

    in_specs = [page_spec(u) for u in range(pps)] + [
        pl.BlockSpec(w1cat.shape, lambda i, p, pt: (0, 0)),
        pl.BlockSpec((2, 1, CMP_HIDDEN), lambda i, p, pt: (0, 0, 0)),
        pl.BlockSpec(w2cat.shape, lambda i, p, pt: (0, 0)),
        pl.BlockSpec((1, NSA_HD), lambda i, p, pt: (0, 0)),
    ]
    grid_spec = pltpu.PrefetchScalarGridSpec(
        num_scalar_prefetch=1, grid=(b, n_steps), in_specs=in_specs,
        out_specs=pl.BlockSpec((None, n_rows, NSA_HD), lambda i, p, pt: (i, 0, 0)),
        scratch_shapes=[pltpu.VMEM((n_rows + planes, 4 * CMP_HIDDEN), F32)])
    out = pl.pallas_call(
        functools.partial(_compress_paged_kernel, n_steps=n_steps, pps=pps),
        grid_spec=grid_spec,
        out_shape=jax.ShapeDtypeStruct((b, n_rows, NSA_HD), F32),
        compiler_params=pltpu.CompilerParams(dimension_semantics=("parallel", "arbitrary"),
                                             vmem_limit_bytes=VMEM_LIMIT),
        name="nsa_compress_paged",
    )(page_table, *((pool4,) * pps), w1cat, pe_hid, w2cat, k_norm0.reshape(1, NSA_HD).astype(F32))
    return out.reshape(b, s_all, 2, NSA_KV_HEADS, NSA_HD).transpose(0, 2, 3, 1, 4)


def _stack_heads(qb):
    return jnp.concatenate([qb[:, j * LANE:(j + 1) * LANE] for j in range(NSA_GROUP)], axis=0)


def _cmp_attn_kernel(q_ref, kc_ref, vc_ref, cb_ref, gate_ref, ovl_ref, o_ref, sel_ref, *, tq, q0, n_sb, gpb):
    i = pl.program_id(2)
    s_all = kc_ref.shape[1]
    nsbp = sel_ref.shape[-1]
    width = NSA_GROUP * LANE
    ovl = ovl_ref[...]
    imps = []
    for gg in range(gpb):
        q4 = _stack_heads(q_ref[:, gg * width:(gg + 1) * width]).astype(BF16)
        s = lax.dot_general(q4, kc_ref[gg].astype(BF16), NT_DIMS, preferred_element_type=F32)
        bias = cb_ref[gg * NSA_GROUP:(gg + 1) * NSA_GROUP].reshape(NSA_GROUP * tq, s_all)
        s = s + bias
        ok = bias > 0.5 * NEG_INF
        m = jnp.max(s, axis=-1, keepdims=True)
        e = jnp.exp(s - m)
        p = jnp.where(ok, e / jnp.sum(e, axis=-1, keepdims=True), 0.0)
        o4 = jnp.dot(p.astype(BF16), vc_ref[gg].astype(BF16), preferred_element_type=F32)
        for j in range(NSA_GROUP):
            col = gg * width + j * LANE
            o_ref[:, col:col + LANE] = o4[j * tq:(j + 1) * tq] * gate_ref[gg, :, j:j + 1]
        psum = p[0:tq] + p[tq:2 * tq] + p[2 * tq:3 * tq] + p[3 * tq:4 * tq]
        p_hi = psum.astype(BF16)
        r1 = psum - p_hi.astype(F32)
        p_mid = r1.astype(BF16)
        p_lo = (r1 - p_mid.astype(F32)).astype(BF16)
        imps.append(jnp.dot(p_hi, ovl, preferred_element_type=F32) + jnp.dot(p_mid, ovl, preferred_element_type=F32)
                    + jnp.dot(p_lo, ovl, preferred_element_type=F32))
    imp = jnp.concatenate(imps, axis=0)

    lane1 = lax.broadcasted_iota(jnp.int32, (tq, nsbp), 1)
    qpos1 = q0 + i * tq + lax.broadcasted_iota(jnp.int32, (tq, nsbp), 0)
    lane = jnp.concatenate([lane1] * gpb, axis=0)
    qpos = jnp.concatenate([qpos1] * gpb, axis=0)
    cur = qpos >> SEL_SHIFT
    forced = (lane == 0) | (lane == cur) | (lane == cur - 1)
    sb_ok = (lane << SEL_SHIFT) <= qpos
    score = jnp.where(sb_ok, imp + jnp.where(forced, FORCE_BONUS, 0.0), NEG_INF)
    work = jnp.where(lane < n_sb, score, -jnp.inf)
    lane_f = lane.astype(F32)
    selneg = jnp.full((gpb * tq, nsbp), NEG_INF, F32)
    for _ in range(N_SEL):
        mx = jnp.max(work, axis=-1, keepdims=True)
        first = jnp.min(jnp.where(work == mx, lane_f, 1e9), axis=-1, keepdims=True)
        hit = lane_f == first
        selneg = jnp.where(hit & (mx > 0.5 * NEG_INF), 0.0, selneg)
        work = jnp.where(hit, -jnp.inf, work)
    for gg in range(gpb):
        sel_ref[gg] = selneg[gg * tq:(gg + 1) * tq]


def nsa_cmp_attention(proj3, kvc, cmp_bias, gates_r, ovl, tq, q0, n_sb):
    b, t, _ = proj3.shape
    s_all = kvc.shape[3]
    nsbp = ovl.shape[1]
    gpb = NSA_KV_HEADS if tq * NSA_KV_HEADS <= CMP_TOPK_ROWS else 1
    kern = functools.partial(_cmp_attn_kernel, tq=tq, q0=q0, n_sb=n_sb, gpb=gpb)
    return pl.pallas_call(
        kern,
        grid=(b, NSA_KV_HEADS // gpb, t // tq),
        in_specs=[
            pl.BlockSpec((None, tq, gpb * NSA_GROUP * NSA_HD), lambda i, g, q: (i, q, g)),
            pl.BlockSpec((None, None, gpb, s_all, NSA_HD), lambda i, g, q: (i, 0, g, 0, 0)),
            pl.BlockSpec((None, None, gpb, s_all, NSA_HD), lambda i, g, q: (i, 1, g, 0, 0)),
            pl.BlockSpec((gpb * NSA_GROUP, tq, s_all), lambda i, g, q: (g, q, 0)),
            pl.BlockSpec((None, gpb, tq, 3 * NSA_GROUP), lambda i, g, q: (i, g, q, 0)),
            pl.BlockSpec((s_all, nsbp), lambda i, g, q: (0, 0)),
        ],
        out_specs=[
            pl.BlockSpec((None, tq, gpb * NSA_GROUP * NSA_HD), lambda i, g, q: (i, q, g)),
            pl.BlockSpec((None, gpb, tq, nsbp), lambda i, g, q: (i, g, q, 0)),
        ],
        out_shape=[jax.ShapeDtypeStruct((b, t, NSA_QDIM), F32),
                   jax.ShapeDtypeStruct((b, NSA_KV_HEADS, t, nsbp), F32)],
        compiler_params=pltpu.CompilerParams(dimension_semantics=("parallel", "parallel", "parallel"),
                                             vmem_limit_bytes=VMEM_LIMIT),
        name="nsa_cmp_attn",
    )(proj3, kvc, kvc, cmp_bias, gates_r, ovl)


def _softmax_tile_update(s, v, m_prev, l_prev, acc_prev):
    m_new = jnp.maximum(m_prev, jnp.max(s, axis=-1, keepdims=True))
    alpha = jnp.exp(m_prev - m_new)
    p = jnp.exp(s - m_new)
    l_new = alpha * l_prev + jnp.sum(p, axis=-1, keepdims=True)
    acc_new = alpha * acc_prev + jnp.dot(p.astype(BF16), v, preferred_element_type=F32)
    return m_new, l_new, acc_new


def _tile_scores(q4, k, bias3, sel, tq, tk, qpos0, kpos0, window):
    s = lax.dot_general(q4, k, NT_DIMS, preferred_element_type=F32).reshape(NSA_GROUP, tq, tk) + bias3
    kpos = kpos0 + lax.broadcasted_iota(jnp.int32, (tq, tk), 1)
    qpos = qpos0 + lax.broadcasted_iota(jnp.int32, (tq, tk), 0)
    dist = qpos - kpos
    mask = dist >= 0
    if window is not None:
        mask = mask & (dist < window)
    if sel is not None:
        nsbp = sel.shape[1]
        blk = (kpos0 + lax.broadcasted_iota(jnp.int32, (nsbp, tk), 1)) >> SEL_SHIFT
        onehot = jnp.where(blk == lax.broadcasted_iota(jnp.int32, (nsbp, tk), 0), 1.0, 0.0).astype(BF16)
        shared = jnp.where(mask, jnp.dot(sel.astype(BF16), onehot, preferred_element_type=F32), NEG_INF)
    else:
        shared = jnp.where(mask, 0.0, NEG_INF)
    return (s + shared[None]).reshape(NSA_GROUP * tq, tk)


def _flash_prompt_kernel(*refs, tq, tk, nkk, use_sel, window, gcol):
    if use_sel:
        q_ref, k_ref, v_ref, b_ref, gate_ref, sel_ref, o_ref, q4_ref, m_ref, l_ref, acc_ref = refs
    else:
        q_ref, k_ref, v_ref, b_ref, gate_ref, o_ref, q4_ref, m_ref, l_ref, acc_ref = refs
        sel_ref = None
    i = pl.program_id(2)
    jj = pl.program_id(3)
    if use_sel:
        j, valid = jj, jj <= i
    else:
        j = i - (nkk - 1) + jj
        valid = j >= 0

    @pl.when(jj == 0)
    def _():
        q4_ref[...] = _stack_heads(q_ref[...]).astype(BF16)
        m_ref[...] = jnp.full(m_ref.shape, NEG_INF, F32)
        l_ref[...] = jnp.zeros(l_ref.shape, F32)
        acc_ref[...] = jnp.zeros(acc_ref.shape, F32)

    @pl.when(valid)
    def _():
        sel = sel_ref[...] if use_sel else None
        s = _tile_scores(q4_ref[...], k_ref[...].astype(BF16), b_ref[...], sel, tq, tk, i * tq, j * tk, window)
        m_new, l_new, acc_new = _softmax_tile_update(s, v_ref[...].astype(BF16), m_ref[...], l_ref[...],
                                                     acc_ref[...])
        m_ref[...] = m_new
        l_ref[...] = l_new
        acc_ref[...] = acc_new

    @pl.when(jj == nkk - 1)
    def _():
        o4 = acc_ref[...] / l_ref[...]
        for h in range(NSA_GROUP):
            o_ref[:, h * LANE:(h + 1) * LANE] = o4[h * tq:(h + 1) * tq] * gate_ref[:, gcol + h:gcol + h + 1]


def nsa_flash_prompt(proj3, bias_tab, gates_r, sel, *, tile, kcol, vcol, window, gcol, name):
    b, t, _ = proj3.shape
    nq = t // tile
    n_dd = bias_tab.shape[1]
    use_sel = sel is not None
    nkk = nq if use_sel else (window + tile - 1) // tile + 1

    def kidx(q, jj):
        return jnp.minimum(jj, q) if use_sel else jnp.maximum(q - (nkk - 1) + jj, 0)

    in_specs = [
        pl.BlockSpec((None, tile, NSA_GROUP * NSA_HD), lambda i, g, q, jj: (i, q, g)),
        pl.BlockSpec((None, tile, NSA_HD), lambda i, g, q, jj: (i, kidx(q, jj), kcol + g)),
        pl.BlockSpec((None, tile, NSA_HD), lambda i, g, q, jj: (i, kidx(q, jj), vcol + g)),
        pl.BlockSpec((NSA_GROUP, None, tile, tile),
                     lambda i, g, q, jj: (g, jnp.minimum(q - kidx(q, jj), n_dd - 1), 0, 0)),
        pl.BlockSpec((None, None, tile, 3 * NSA_GROUP), lambda i, g, q, jj: (i, g, q, 0)),
    ]
    args = [proj3, proj3, proj3, bias_tab, gates_r]
    if use_sel:
        nsbp = sel.shape[-1]
        in_specs.append(pl.BlockSpec((None, None, tile, nsbp), lambda i, g, q, jj: (i, g, q, 0)))
        args.append(sel)
    kern = functools.partial(_flash_prompt_kernel, tq=tile, tk=tile, nkk=nkk, use_sel=use_sel, window=window,
                             gcol=gcol)
    return pl.pallas_call(
        kern,
        grid=(b, NSA_KV_HEADS, nq, nkk),
        in_specs=in_specs,
        out_specs=pl.BlockSpec((None, tile, NSA_GROUP * NSA_HD), lambda i, g, q, jj: (i, q, g)),
        out_shape=jax.ShapeDtypeStruct((b, t, NSA_QDIM), F32),
        scratch_shapes=[pltpu.VMEM((NSA_GROUP * tile, NSA_HD), BF16),
                        pltpu.VMEM((NSA_GROUP * tile, 1), F32),
                        pltpu.VMEM((NSA_GROUP * tile, 1), F32),
                        pltpu.VMEM((NSA_GROUP * tile, NSA_HD), F32)],
        compiler_params=pltpu.CompilerParams(
            dimension_semantics=("parallel", "parallel", "parallel", "arbitrary"), vmem_limit_bytes=VMEM_LIMIT),
        name=name,
    )(*args)


def _flash_sample_kernel(*refs, tq, pps, n_steps, plane0, qtile, kbase, q0, use_sel, window, gcol):
    pt_ref, q_ref, pool_ref, kn_ref, vn_ref, b_ref, gate_ref = refs[:7]
    if use_sel:
        sel_ref, o_ref, kv_ref, sem, q4_ref, m_ref, l_ref, acc_ref = refs[7:]
    else:
        o_ref, kv_ref, sem, q4_ref, m_ref, l_ref, acc_ref = refs[7:]
        sel_ref = None
    i = pl.program_id(0)
    planes = 2 * NSA_KV_HEADS
    page = kv_ref.shape[3]
    tk = pps * page
    n_dd = b_ref.shape[1]

    def page_copies(step, slot):
        return [pltpu.make_async_copy(pool_ref.at[pt_ref[i, step * pps + u], :, plane0 + j, :],
                                      kv_ref.at[slot, u, j], sem.at[slot])
                for u in range(pps) for j in range(planes)]

    for cp in page_copies(0, 0):
        cp.start()
    for g in range(NSA_KV_HEADS):
        q4_ref[g] = _stack_heads(q_ref[:, g * NSA_GROUP * LANE:(g + 1) * NSA_GROUP * LANE]).astype(BF16)
    m_ref[...] = jnp.full(m_ref.shape, NEG_INF, F32)
    l_ref[...] = jnp.zeros(l_ref.shape, F32)
    acc_ref[...] = jnp.zeros(acc_ref.shape, F32)

    def update(g, k, v, bias3, kpos0):
        sel = sel_ref[g] if use_sel else None
        s = _tile_scores(q4_ref[g], k, bias3, sel, tq, k.shape[0], q0, kpos0, window)
        m_new, l_new, acc_new = _softmax_tile_update(s, v, m_ref[g], l_ref[g], acc_ref[g])
        m_ref[g] = m_new
        l_ref[g] = l_new
        acc_ref[g] = acc_new

    def step_body(step, carry):
        slot = step % 2

        @pl.when(step + 1 < n_steps)
        def _():
            for cp in page_copies(step + 1, 1 - slot):
                cp.start()

        for cp in page_copies(step, slot):
            cp.wait()
        dd = jnp.clip(qtile - step, 0, n_dd - 1)
        scores = []
        for g in range(NSA_KV_HEADS):
            k = jnp.concatenate([kv_ref[slot, u, g].astype(BF16) for u in range(pps)], axis=0)
            sel = sel_ref[g] if use_sel else None
            scores.append(_tile_scores(q4_ref[g], k, b_ref[g * NSA_GROUP:(g + 1) * NSA_GROUP, dd], sel, tq, tk, q0,
                                       kbase + step * tk, window))
        for g in range(NSA_KV_HEADS):
            v = jnp.concatenate([kv_ref[slot, u, NSA_KV_HEADS + g].astype(BF16) for u in range(pps)], axis=0)
            m_new, l_new, acc_new = _softmax_tile_update(scores[g], v, m_ref[g], l_ref[g], acc_ref[g])
            m_ref[g] = m_new
            l_ref[g] = l_new
            acc_ref[g] = acc_new
        return carry

    lax.fori_loop(0, n_steps, step_body, 0)

    pad = jnp.zeros((page - tq, LANE), F32)
    for g in range(NSA_KV_HEADS):
        cols = slice(g * LANE, (g + 1) * LANE)
        kn = jnp.concatenate([kn_ref[:, cols], pad], axis=0).astype(BF16)
        vn = jnp.concatenate([vn_ref[:, cols], pad], axis=0).astype(BF16)
        update(g, kn, vn, b_ref[g * NSA_GROUP:(g + 1) * NSA_GROUP, 0, :, 0:page], q0)
        o4 = acc_ref[g] / l_ref[g]
        for j in range(NSA_GROUP):
            h = g * NSA_GROUP + j
            o_ref[:, h * LANE:(h + 1) * LANE] = o4[j * tq:(j + 1) * tq] * gate_ref[g, :, gcol + j:gcol + j + 1]


def nsa_flash_sample(proj3, pool4, table, bias_tab, gates_r, sel, *, plane0, newk, newv, kbase, q0, window,
                     gcol, name):
    b, tq, _ = proj3.shape
    page = pool4.shape[1]
    tk = bias_tab.shape[3]
    pps = tk // page
    n_steps = table.shape[1] // pps
    qtile = (q0 - kbase) // tk
    use_sel = sel is not None
    in_specs = [
        pl.BlockSpec((None, tq, NSA_QDIM), lambda i, pt: (i, 0, 0)),
        pl.BlockSpec(memory_space=pl.ANY),
        pl.BlockSpec((None, tq, NSA_KVDIM), lambda i, pt: (i, 0, newk)),
        pl.BlockSpec((None, tq, NSA_KVDIM), lambda i, pt: (i, 0, newv)),
        pl.BlockSpec(bias_tab.shape, lambda i, pt: (0, 0, 0, 0)),
        pl.BlockSpec((None, NSA_KV_HEADS, tq, 3 * NSA_GROUP), lambda i, pt: (i, 0, 0, 0)),
    ]
    args = [proj3, pool4, proj3, proj3, bias_tab, gates_r]
    if use_sel:
        nsbp = sel.shape[-1]
        in_specs.append(pl.BlockSpec((None, NSA_KV_HEADS, tq, nsbp), lambda i, pt: (i, 0, 0, 0)))
        args.append(sel)
    kern = functools.partial(_flash_sample_kernel, tq=tq, pps=pps, n_steps=n_steps, plane0=plane0, qtile=qtile,
                             kbase=kbase, q0=q0, use_sel=use_sel, window=window, gcol=gcol)
    rows4 = NSA_GROUP * tq
    grid_spec = pltpu.PrefetchScalarGridSpec(
        num_scalar_prefetch=1, grid=(b,), in_specs=in_specs,
        out_specs=pl.BlockSpec((None, tq, NSA_QDIM), lambda i, pt: (i, 0, 0)),
        scratch_shapes=[pltpu.VMEM((2, pps, 2 * NSA_KV_HEADS, page, NSA_HD), F32),
                        pltpu.SemaphoreType.DMA((2,)),
                        pltpu.VMEM((NSA_KV_HEADS, rows4, NSA_HD), BF16),
                        pltpu.VMEM((NSA_KV_HEADS, rows4, 1), F32),
                        pltpu.VMEM((NSA_KV_HEADS, rows4, 1), F32),
                        pltpu.VMEM((NSA_KV_HEADS, rows4, NSA_HD), F32)])
    return pl.pallas_call(
        kern, grid_spec=grid_spec, out_shape=jax.ShapeDtypeStruct((b, tq, NSA_QDIM), F32),
        compiler_params=pltpu.CompilerParams(dimension_semantics=("arbitrary",), vmem_limit_bytes=VMEM_LIMIT),
        name=name,
    )(table, *args)


def _bias_table(rel_bias, unit, tq, tk):
    n_dd = -(-(T5_SATURATION + tk - 1) // unit) + 1
    dist = (jnp.arange(n_dd)[:, None, None] * unit + jnp.arange(tq)[None, :, None]
            - jnp.arange(tk)[None, None, :])
    return _bucket_lookup(rel_bias, dist)


def _bucket_lookup(rel_bias, dist):
    onehot = jax.nn.one_hot(t5_bucket(dist), N_BUCKETS, dtype=F32)
    out = jnp.einsum('...b,bh->h...', onehot, rel_bias.astype(F32), precision=lax.Precision.HIGHEST)
    return out


def _cmp_tables(rel_bias, q0, t, s_all, nc, n_sb, nsbp):
    c = jnp.arange(s_all)
    c_end = c * CMP_STRIDE + (CMP_BLOCK - 1)
    dist = (q0 + jnp.arange(t))[:, None] - c_end[None, :]
    ok = (dist >= 0) & (c < nc)[None, :]
    bias = jnp.where(ok[None], _bucket_lookup(rel_bias, dist), NEG_INF)
    sb_start = jnp.arange(nsbp) * SEL_BLOCK
    c_start = c * CMP_STRIDE
    ovl = jnp.maximum(jnp.minimum(c_end[:, None], sb_start[None, :] + SEL_BLOCK - 1)
                      - jnp.maximum(c_start[:, None], sb_start[None, :]) + 1, 0).astype(F32) / CMP_BLOCK
    ovl = jnp.where((c < nc)[:, None] & (jnp.arange(nsbp) < n_sb)[None, :], ovl, 0.0)
    return bias, ovl.astype(BF16)


def _nsa_weights(w_in, q_norm, k_norm, cmp_pe, cmp_w1, cmp_w2, rel_bias, w_out):
    n_main = NSA_QDIM + 6 * NSA_KVDIM
    nsub = CMP_BLOCK // CMP_STRIDE
    w1r = cmp_w1.reshape(2, nsub, CMP_STRIDE, NSA_HD, CMP_HIDDEN)
    w1r = jnp.moveaxis(w1r, 1, 3).reshape(2, CMP_STRIDE, NSA_HD, nsub * CMP_HIDDEN).astype(BF16)
    pe_hid = jnp.einsum('ck,cke->ce', cmp_pe.reshape(2, -1), cmp_w1, precision=lax.Precision.HIGHEST)
    return dict(
        w_main=w_in[:, :n_main].astype(BF16),
        w_gate=_pad_cols(w_in[:, n_main:], LANE).astype(BF16),
        post=_nsa_in_post(q_norm, k_norm),
        w1r=w1r, pe_hid=pe_hid.reshape(2, 1, CMP_HIDDEN), w2=cmp_w2.astype(BF16), k_norm0=k_norm[0],
        rel_bias=rel_bias, w_out=w_out.astype(BF16))


def nsa_mixer(h2d, b, t, norm_in, nw, paged):
    post_main, post_gate = nw["post"]
    proj2 = proj(h2d, nw["w_main"], norm_w=norm_in, post=post_main, name="nsa_in")
    gate_out = proj(h2d, nw["w_gate"], norm_w=norm_in, post=post_gate, name="nsa_in_gate")
    proj3 = proj2.reshape(b, t, -1)
    gates_r = gate_out[:, :3 * NSA_HEADS].reshape(b, t, 3, NSA_KV_HEADS, NSA_GROUP)
    gates_r = gates_r.transpose(0, 3, 1, 2, 4).reshape(b, NSA_KV_HEADS, t, 3 * NSA_GROUP)
    kv_blk0 = NSA_QDIM // NSA_KVDIM
    kv_col0 = NSA_QDIM // NSA_HD
    rel_bias = nw["rel_bias"]
    if paged is None:
        p_len, tq = 0, 256
        kvc = nsa_compress(proj3, kv_blk0, nw["w1r"], nw["pe_hid"], nw["w2"], nw["k_norm0"])
    else:
        pool, page_table, win_pool = paged
        p_len, tq = page_table.shape[1] * PAGE_SIZE, t
        kvc = nsa_compress_paged(pool, page_table, nw["w1r"], nw["pe_hid"], nw["w2"], nw["k_norm0"])
    n_all = p_len + t
    s_all = kvc.shape[3]
    nc = n_all // CMP_STRIDE - CMP_BLOCK // CMP_STRIDE + 1
    n_sb = -(-n_all // SEL_BLOCK)
    nsbp = -(-n_sb // LANE) * LANE
    cmp_bias, ovl = _cmp_tables(rel_bias, p_len, t, s_all, nc, n_sb, nsbp)
    o_cmp, sel = nsa_cmp_attention(proj3, kvc, cmp_bias, gates_r, ovl, tq, p_len, n_sb)
    if paged is None:
        tab = _bias_table(rel_bias, tq, tq, tq)
        o_sel = nsa_flash_prompt(proj3, tab, gates_r, sel, tile=tq, kcol=kv_col0 + 2 * NSA_KV_HEADS,
                                 vcol=kv_col0 + 3 * NSA_KV_HEADS, window=None, gcol=NSA_GROUP, name="nsa_sel")
        o_win = nsa_flash_prompt(proj3, tab, gates_r, None, tile=tq, kcol=kv_col0 + 4 * NSA_KV_HEADS,
                                 vcol=kv_col0 + 5 * NSA_KV_HEADS, window=WINDOW, gcol=2 * NSA_GROUP, name="nsa_win")
    else:
        tk_sel = SEL_PAGES_PER_STEP * PAGE_SIZE
        assert p_len % tk_sel == 0
        tab_sel = _bias_table(rel_bias, tk_sel, t, tk_sel)
        tab = _bias_table(rel_bias, WINDOW, t, WINDOW)
        o_sel = nsa_flash_sample(proj3, pool, page_table, tab_sel, gates_r, sel, plane0=2 * NSA_KV_HEADS,
                                 newk=kv_blk0 + 2,
                                 newv=kv_blk0 + 3, kbase=0, q0=p_len, window=None, gcol=NSA_GROUP,
                                 name="nsa_sel_paged")
        n_wt = WINDOW // PAGE_SIZE
        win_table = jnp.arange(b * n_wt, dtype=jnp.int32).reshape(b, n_wt)
        o_win = nsa_flash_sample(proj3, win_pool, win_table, tab, gates_r, None, plane0=0, newk=kv_blk0 + 4,
                                 newv=kv_blk0 + 5, kbase=p_len - WINDOW, q0=p_len, window=WINDOW,
                                 gcol=2 * NSA_GROUP, name="nsa_win_paged")
    y = proj((o_cmp.reshape(b * t, NSA_QDIM), o_sel.reshape(b * t, NSA_QDIM), o_win.reshape(b * t, NSA_QDIM)),
             nw["w_out"], res=h2d, name="nsa_out")
    new_rows = proj3[:, :, NSA_QDIM:NSA_QDIM + 4 * NSA_KVDIM].reshape(b, t, 4, NSA_KV_HEADS, NSA_HD)
    new_win = proj3[:, :, NSA_QDIM + 4 * NSA_KVDIM:].reshape(b, t, 2, NSA_KV_HEADS, NSA_HD)
    return y, new_rows, new_win


def _pad_cols(w, n_pad):
    return jnp.pad(w, ((0, 0), (0, n_pad - w.shape[1])))


def _nsa_in_post(q_norm, k_norm):
    ones_kv = jnp.ones((NSA_KVDIM,), F32)
    zeros_kv = jnp.zeros((NSA_KVDIM,), F32)
    pw = jnp.concatenate([
        jnp.tile(q_norm.astype(F32) * (NSA_HD ** -0.5), NSA_HEADS),
        ones_kv, ones_kv, jnp.tile(k_norm[1].astype(F32), NSA_KV_HEADS), ones_kv,
        jnp.tile(k_norm[2].astype(F32), NSA_KV_HEADS), ones_kv])
    pm = jnp.concatenate([
        jnp.ones((NSA_QDIM,), F32),
        zeros_kv, zeros_kv, ones_kv, zeros_kv, ones_kv, zeros_kv])
    return (pw, pm), (jnp.ones((LANE,), F32), jnp.full((LANE,), 2.0, F32))


def kernel(x_prompt, x_sample, state_gdn, state_gdn_conv, cache_nsa_kv, state_nsa_win, state_ffn_conv, page_table,
           norm_mix, norm_ffn, gdn_w_in, gdn_conv_w, gdn_A_log, gdn_dt_bias, gdn_norm, gdn_w_out,
           nsa_w_in, nsa_q_norm, nsa_k_norm, nsa_cmp_pe, nsa_cmp_w1, nsa_cmp_w2, rel_bias, nsa_w_out,
           ffn_w_up, ffn_conv_w, ffn_conv_b, ffn_w_down):
    depth = norm_mix.shape[0]
    bp, tp, d = x_prompt.shape
    bs, ts, _ = x_sample.shape
    win_buf = state_nsa_win.shape[2]
    assert win_buf == WINDOW and tp >= WINDOW and cache_nsa_kv.shape[2] == PAGE_SIZE
    d_ff = ffn_conv_w.shape[1]
    conv_keep = ffn_conv_w.shape[2] - 1
    hp = x_prompt.reshape(bp * tp, d)
    hs = x_sample.reshape(bs * ts, d)
    gdn_p, gdnc_p, kv_p, win_p, ffn_p = [], [], [], [], []
    gdn_s, gdnc_s, kv_s, win_s, ffn_s = [], [], [], [], []
    for i in range(depth):
        j = i // 2
        if i % 2 == 0:
            n_main = GDN_CONV_DIM + GDN_VDIM
            w_in = (gdn_w_in[j][:, :n_main].astype(BF16), _pad_cols(gdn_w_in[j][:, n_main:], LANE).astype(BF16))
            gw = (norm_mix[i], w_in, gdn_conv_w[j], gdn_A_log[j], gdn_dt_bias[j], gdn_norm[j],
                  gdn_w_out[j].astype(BF16))
            hp, st_p, cv_p = gdn_mixer(hp, bp, tp, jnp.zeros((bp, GDN_V_HEADS, GDN_DK, GDN_DV), F32),
                                       jnp.zeros((bp, gdn_conv_w.shape[2] - 1, GDN_CONV_DIM), F32), *gw)
            hs, st_s, cv_s = gdn_mixer(hs, bs, ts, state_gdn[j], state_gdn_conv[j], *gw)
            gdn_p.append(st_p)
            gdnc_p.append(cv_p)
            gdn_s.append(st_s)
            gdnc_s.append(cv_s)
        else:
            nw = _nsa_weights(nsa_w_in[j], nsa_q_norm[j], nsa_k_norm[j], nsa_cmp_pe[j], nsa_cmp_w1[j],
                              nsa_cmp_w2[j], rel_bias, nsa_w_out[j])
            hp, rows_p, nwin_p = nsa_mixer(hp, bp, tp, norm_mix[i], nw, None)
            n_pool = cache_nsa_kv.shape[1]
            pool = cache_nsa_kv.reshape(cache_nsa_kv.shape[0] * n_pool, PAGE_SIZE, 4 * NSA_KV_HEADS, NSA_HD)
            win_pool = state_nsa_win[j].reshape(bs * (win_buf // PAGE_SIZE), PAGE_SIZE, 2 * NSA_KV_HEADS, NSA_HD)
            hs, rows_s, nwin_s = nsa_mixer(hs, bs, ts, norm_mix[i], nw, (pool, page_table + j * n_pool, win_pool))
            kv_p.append(rows_p)
            win_p.append(nwin_p[:, tp - WINDOW:])
            kv_s.append(rows_s)
            win_s.append(jnp.concatenate([state_nsa_win[j][:, ts:], nwin_s], axis=1))
        n_up = -(-ffn_w_up.shape[2] // 512) * 512
        fw = (norm_ffn[i], _pad_cols(ffn_w_up[i], n_up).astype(BF16), ffn_conv_w[i], ffn_conv_b[i], ffn_w_down[i].astype(BF16))
        hp, cp = conv_ffn(hp, bp, tp, jnp.zeros((bp, conv_keep, d_ff), F32), *fw)
        hs, cs = conv_ffn(hs, bs, ts, state_ffn_conv[i], *fw)
        ffn_p.append(cp)
        ffn_s.append(cs)
    return (hp.reshape(bp, tp, d), hs.reshape(bs, ts, d),
            jnp.stack(gdn_p), jnp.stack(gdnc_p), jnp.stack(kv_p), jnp.stack(win_p), jnp.stack(ffn_p),
            jnp.stack(gdn_s), jnp.stack(gdnc_s), jnp.stack(kv_s), jnp.stack(win_s), jnp.stack(ffn_s))
```

```python
import functools
import math

import jax
import jax.numpy as jnp
from jax import lax
from jax.experimental import pallas as pl
from jax.experimental.pallas import tpu as pltpu

F32 = jnp.float32
BF16 = jnp.bfloat16
RMS_EPS = 1e-6
NEG_INF = -1e30

LANE = 128
SUBLANE = 8
VMEM_LIMIT = 56 * 1024 * 1024
PROJ_VMEM_BUDGET = 40 * 1024 * 1024
FFN_DOWN_ROWS = 256

D_MODEL = 2048
GDN_K_HEADS = 16
GDN_V_HEADS = 32
GDN_DK = 128
GDN_DV = 128
GDN_KDIM = GDN_K_HEADS * GDN_DK
GDN_VDIM = GDN_V_HEADS * GDN_DV
GDN_CONV_DIM = 2 * GDN_KDIM + GDN_VDIM
GDN_CHUNK = 64

NSA_HEADS = 16
NSA_KV_HEADS = 4
NSA_HD = 128
NSA_GROUP = NSA_HEADS // NSA_KV_HEADS
NSA_QDIM = NSA_HEADS * NSA_HD
NSA_KVDIM = NSA_KV_HEADS * NSA_HD
CMP_BLOCK = 32
CMP_STRIDE = 16
CMP_HIDDEN = 2 * NSA_HD
SEL_BLOCK = 64
SEL_SHIFT = 6
N_SEL = 16
WINDOW = 512
FORCE_BONUS = 1e3
N_BUCKETS = 32
REL_MAX_DIST = 1024
PAGE_SIZE = 128
PAGES_PER_STEP = 8
SEL_PAGES_PER_STEP = 16
T5_SATURATION = 790
CMP_TOPK_ROWS = 256
NT_DIMS = (((1,), (1,)), ((), ()))


def _proj_kernel(*refs, n_x, has_norm, has_res, has_post):
    it = iter(refs)
    x_refs = [next(it) for _ in range(n_x)]
    nw_ref = next(it) if has_norm else None
    w_ref = next(it)
    res_ref = next(it) if has_res else None
    pw_ref = next(it) if has_post else None
    pm_ref = next(it) if has_post else None
    o_ref = next(it)
    xs_ref = next(it)

    @pl.when(pl.program_id(1) == 0)
    def _():
        x = x_refs[0][...].astype(F32)
        for r in x_refs[1:]:
            x = x + r[...].astype(F32)
        if has_norm:
            ms = jnp.mean(x * x, axis=-1, keepdims=True)
            x = x * lax.rsqrt(ms + RMS_EPS) * nw_ref[...]
        xs_ref[...] = x.astype(BF16)

    y = jnp.dot(xs_ref[...], w_ref[...], preferred_element_type=F32)
    if has_res:
        y = y + res_ref[...]
    if has_post:
        tn = y.shape[1]
        for g in range(tn // LANE):
            sl = slice(g * LANE, (g + 1) * LANE)
            yg = y[:, sl]
            mode = pm_ref[:, sl]
            ms = jnp.mean(yg * yg, axis=-1, keepdims=True)
            normed = yg * lax.rsqrt(ms + RMS_EPS) * pw_ref[:, sl]
            sig = jax.nn.sigmoid(yg)
            o_ref[:, sl] = jnp.where(mode == 1.0, normed, jnp.where(mode == 2.0, sig, yg))
    else:
        o_ref[...] = y.astype(o_ref.dtype)


def _pick_tile(n, candidates):
    for c in candidates:
        if n % c == 0:
            return c
    return n


def proj(xs, w, *, norm_w=None, res=None, post=None, out_dtype=F32, name="proj"):
    if not isinstance(xs, (tuple, list)):
        xs = (xs,)
    m, k = xs[0].shape
    n = w.shape[1]
    tn = _pick_tile(n, (512, 384, 256, 128))
    x_row_bytes = sum(k * x.dtype.itemsize for x in xs)

    def vmem_bytes(tm):
        return 2 * (tm * x_row_bytes + k * tn * 2 + tm * tn * 4 * (2 if res is not None else 1)) + tm * k * 2

    tm = next((c for c in (1024, 512, 256) if m % c == 0 and vmem_bytes(c) <= PROJ_VMEM_BUDGET), m)
    in_specs = [pl.BlockSpec((tm, k), lambda i, j: (i, 0)) for _ in xs]
    args = list(xs)
    if norm_w is not None:
        in_specs.append(pl.BlockSpec((1, k), lambda i, j: (0, 0)))
        args.append(norm_w.reshape(1, k).astype(F32))
    in_specs.append(pl.BlockSpec((k, tn), lambda i, j: (0, j)))
    args.append(w)
    if res is not None:
        in_specs.append(pl.BlockSpec((tm, tn), lambda i, j: (i, j)))
        args.append(res)
    if post is not None:
        for a in post:
            in_specs.append(pl.BlockSpec((1, tn), lambda i, j: (0, j)))
            args.append(a.reshape(1, n).astype(F32))
    kern = functools.partial(_proj_kernel, n_x=len(xs), has_norm=norm_w is not None, has_res=res is not None,
                             has_post=post is not None)
    return pl.pallas_call(
        kern,
        grid=(m // tm, n // tn),
        in_specs=in_specs,
        out_specs=pl.BlockSpec((tm, tn), lambda i, j: (i, j)),
        out_shape=jax.ShapeDtypeStruct((m, n), out_dtype),
        scratch_shapes=[pltpu.VMEM((tm, k), BF16)],
        compiler_params=pltpu.CompilerParams(dimension_semantics=("parallel", "arbitrary"),
                                             vmem_limit_bytes=VMEM_LIMIT),
        name=name,
    )(*args)


def _conv_gate(pad_ref, prev, gate, val, cw, cb, width):
    t = gate.shape[0]
    halo = SUBLANE
    pad_ref[halo - (width - 1):halo, :] = prev
    pad_ref[halo:halo + t, :] = gate
    acc = cb + pad_ref[halo - (width - 1):halo - (width - 1) + t, :] * cw[0:1, :]
    for i in range(1, width):
        off = halo - (width - 1) + i
        acc = acc + pad_ref[off:off + t, :] * cw[i:i + 1, :]
    return jax.nn.silu(acc) * val


def _ffn_gate_kernel(gate_ref, val_ref, pre_ref, cw_ref, cb_ref, o_ref, pad_ref, *, width):
    for i in range(gate_ref.shape[0]):
        hid = _conv_gate(pad_ref, pre_ref[i], gate_ref[i], val_ref[i], cw_ref[...], cb_ref[...], width)
        o_ref[i] = hid.astype(o_ref.dtype)


def ffn_gate(up, prefix, conv_w, conv_b, d_ff):
    b, t, _ = up.shape
    width = conv_w.shape[1]
    nblk = d_ff // LANE
    kern = functools.partial(_ffn_gate_kernel, width=width)
    return pl.pallas_call(
        kern,
        grid=(nblk,),
        in_specs=[
            pl.BlockSpec((b, t, LANE), lambda j: (0, 0, j)),
            pl.BlockSpec((b, t, LANE), lambda j: (0, 0, j + nblk)),
            pl.BlockSpec((b, width - 1, LANE), lambda j: (0, 0, j)),
            pl.BlockSpec((width, LANE), lambda j: (0, j)),
            pl.BlockSpec((1, LANE), lambda j: (0, j)),
        ],
        out_specs=pl.BlockSpec((b, t, LANE), lambda j: (0, 0, j)),
        out_shape=jax.ShapeDtypeStruct((b, t, d_ff), BF16),
        scratch_shapes=[pltpu.VMEM((t + SUBLANE, LANE), F32)],
        compiler_params=pltpu.CompilerParams(dimension_semantics=("parallel",)),
        name="ffn_gate",
    )(up, up, prefix, conv_w.T, conv_b.reshape(1, d_ff))


def _ffn_down_kernel(gate_ref, val_ref, halo_ref, pre_ref, cw_ref, cb_ref, w_ref, res_ref, o_ref, hid_ref, pad_ref, *,
                     width, tiles_per_batch):
    @pl.when(pl.program_id(1) == 0)
    def _():
        first = (pl.program_id(0) % tiles_per_batch) == 0

        def column_block(c, carry):
            cs = pl.ds(pl.multiple_of(c * LANE, LANE), LANE)
            prev = jnp.where(first, pre_ref[:, cs], halo_ref[SUBLANE - (width - 1):SUBLANE, cs])
            hid = _conv_gate(pad_ref, prev, gate_ref[:, cs], val_ref[:, cs], cw_ref[:, cs], cb_ref[:, cs], width)
            hid_ref[:, cs] = hid.astype(BF16)
            return carry

        lax.fori_loop(0, gate_ref.shape[1] // LANE, column_block, 0)

    o_ref[...] = jnp.dot(hid_ref[...], w_ref[...], preferred_element_type=F32) + res_ref[...]


def ffn_down_fused(up2d, b, t, prefix, conv_w, conv_b, w_down, res):
    d_ff, n = w_down.shape
    width = conv_w.shape[1]
    tm, tn = FFN_DOWN_ROWS, 512
    assert t % tm == 0 and n % tn == 0
    tiles_per_batch = t // tm
    kern = functools.partial(_ffn_down_kernel, width=width, tiles_per_batch=tiles_per_batch)
    return pl.pallas_call(
        kern,
        grid=(b * tiles_per_batch, n // tn),
        in_specs=[
            pl.BlockSpec((tm, d_ff), lambda i, j: (i, 0)),
            pl.BlockSpec((tm, d_ff), lambda i, j: (i, 1)),
            pl.BlockSpec((SUBLANE, d_ff), lambda i, j: (jnp.maximum(i * (tm // SUBLANE) - 1, 0), 0)),
            pl.BlockSpec((None, width - 1, d_ff), lambda i, j: (i // tiles_per_batch, 0, 0)),
            pl.BlockSpec((width, d_ff), lambda i, j: (0, 0)),
            pl.BlockSpec((1, d_ff), lambda i, j: (0, 0)),
            pl.BlockSpec((d_ff, tn), lambda i, j: (0, j)),
            pl.BlockSpec((tm, tn), lambda i, j: (i, j)),
        ],
        out_specs=pl.BlockSpec((tm, tn), lambda i, j: (i, j)),
        out_shape=jax.ShapeDtypeStruct((b * t, n), F32),
        scratch_shapes=[pltpu.VMEM((tm, d_ff), BF16), pltpu.VMEM((tm + SUBLANE, LANE), F32)],
        compiler_params=pltpu.CompilerParams(dimension_semantics=("parallel", "arbitrary"),
                                             vmem_limit_bytes=VMEM_LIMIT),
        name="ffn_down_fused",
    )(up2d, up2d, up2d, prefix, conv_w.T, conv_b.reshape(1, d_ff), w_down, res)


def conv_ffn(h2d, b, t, prefix, norm_w, w_up, conv_w, conv_b, w_down):
    d_ff = conv_w.shape[0]
    keep = conv_w.shape[1] - 1
    assert t >= keep
    up2d = proj(h2d, w_up, norm_w=norm_w, name="ffn_up")
    up = up2d.reshape(b, t, -1)
    new_prefix = up[:, t - keep:, :d_ff]
    if t % FFN_DOWN_ROWS == 0:
        return ffn_down_fused(up2d, b, t, prefix, conv_w, conv_b, w_down, h2d), new_prefix
    hidden = ffn_gate(up, prefix, conv_w, conv_b, d_ff)
    out = proj(hidden.reshape(b * t, d_ff), w_down, res=h2d, name="ffn_down")
    return out, new_prefix


def t5_bucket(dist):
    d = jnp.maximum(dist, 0)
    max_exact = N_BUCKETS // 2
    scale = (N_BUCKETS - max_exact) / math.log(REL_MAX_DIST / max_exact)
    large = max_exact + (jnp.log(jnp.maximum(d, 1).astype(F32) / max_exact) * scale).astype(jnp.int32)
    return jnp.where(d < max_exact, d, jnp.minimum(large, N_BUCKETS - 1))


GDN_CONV_COLS = 512
GDN_CONV_ROWS = 2048
GDN_HEADS_PER_STEP = 16
GDN_GROUP = 4


def _gdn_conv_kernel(x_ref, pre_ref, cw_ref, o_ref, pad_ref, *, width, n_q_blocks, n_qk_blocks):
    j = pl.program_id(1)
    t = x_ref.shape[1]
    halo = SUBLANE
    scale = jnp.where(j < n_q_blocks, GDN_DK ** -0.5, 1.0)
    is_qk = j < n_qk_blocks
    for bi in range(x_ref.shape[0]):
        pad_ref[halo - (width - 1):halo, :] = pre_ref[bi]
        pad_ref[halo:halo + t, :] = x_ref[bi]
        acc = pad_ref[halo - (width - 1):halo - (width - 1) + t, :] * cw_ref[0:1, :]
        for i in range(1, width):
            off = halo - (width - 1) + i
            acc = acc + pad_ref[off:off + t, :] * cw_ref[i:i + 1, :]
        y = jax.nn.silu(acc)
        for h in range(x_ref.shape[2] // LANE):
            seg = y[:, h * LANE:(h + 1) * LANE]
            nrm = seg * lax.rsqrt(jnp.sum(seg * seg, axis=-1, keepdims=True) + RMS_EPS) * scale
            o_ref[bi, :, h * LANE:(h + 1) * LANE] = jnp.where(is_qk, nrm, seg)


def gdn_conv(proj3, prefix, conv_w):
    b, t, _ = proj3.shape
    width = conv_w.shape[1]
    cols = GDN_CONV_COLS
    bb = max(1, min(b, GDN_CONV_ROWS // t))
    assert b % bb == 0
    kern = functools.partial(_gdn_conv_kernel, width=width, n_q_blocks=GDN_KDIM // cols,
                             n_qk_blocks=2 * GDN_KDIM // cols)
    return pl.pallas_call(
        kern,
        grid=(b // bb, GDN_CONV_DIM // cols),
        in_specs=[
            pl.BlockSpec((bb, t, cols), lambda i, j: (i, 0, j)),
            pl.BlockSpec((bb, width - 1, cols), lambda i, j: (i, 0, j)),
            pl.BlockSpec((width, cols), lambda i, j: (0, j)),
        ],
        out_specs=pl.BlockSpec((bb, t, cols), lambda i, j: (i, 0, j)),
        out_shape=jax.ShapeDtypeStruct((b, t, GDN_CONV_DIM), F32),
        scratch_shapes=[pltpu.VMEM((t + SUBLANE, cols), F32)],
        compiler_params=pltpu.CompilerParams(dimension_semantics=("parallel", "parallel"),
                                             vmem_limit_bytes=VMEM_LIMIT),
        name="gdn_conv",
    )(proj3, prefix, conv_w.T)


def _split_bf16(x):
    hi = x.astype(BF16)
    return hi, (x - hi.astype(F32)).astype(BF16)


def _dot_split(a_parts, b_parts):
    (ah, al), (bh, bl) = a_parts, b_parts
    return (jnp.dot(ah, bh, preferred_element_type=F32) + jnp.dot(ah, bl, preferred_element_type=F32)
            + jnp.dot(al, bh, preferred_element_type=F32))


def _gdn_gate_kernel(x_ref, a_ref, dt_ref, o_ref, *, t_real):
    L = GDN_CHUNK
    lane = lax.broadcasted_iota(jnp.int32, (L, LANE), 1)
    row = lax.broadcasted_iota(jnp.int32, (L, LANE), 0)
    ci = lax.broadcasted_iota(jnp.int32, (L, L), 0)
    cj = lax.broadcasted_iota(jnp.int32, (L, L), 1)
    tril = jnp.where(ci >= cj, 1.0, 0.0).astype(BF16)
    neg_a = -jnp.exp(a_ref[...])

    def chunk(c, carry):
        r0 = pl.multiple_of(c * L, L)
        x = x_ref[pl.ds(r0, L), :]
        live = (row + r0) < t_real
        beta = jnp.where(live, jax.nn.sigmoid(x), 0.0)
        z = x + dt_ref[...]
        softplus = jnp.maximum(z, 0.0) + jnp.log(1.0 + jnp.exp(-jnp.abs(z)))
        g = jnp.where(live, neg_a * softplus, 0.0)
        hi, mid = _split_bf16(g)
        lo = (g - hi.astype(F32) - mid.astype(F32)).astype(BF16)
        cum = (jnp.dot(tril, hi, preferred_element_type=F32) + jnp.dot(tril, mid, preferred_element_type=F32)
               + jnp.dot(tril, lo, preferred_element_type=F32))
        o_ref[pl.ds(r0, L), :] = jnp.where(lane < GDN_V_HEADS, beta, cum)
        return carry

    lax.fori_loop(0, x_ref.shape[0] // L, chunk, 0)


def gdn_gates(gate3, a_log, dt_bias, t_real):
    b, t, _ = gate3.shape
    pad = jnp.zeros((GDN_V_HEADS,), F32)
    a_vec = jnp.concatenate([pad, a_log.astype(F32), pad, pad]).reshape(1, LANE)
    dt_vec = jnp.concatenate([pad, dt_bias.astype(F32), pad, pad]).reshape(1, LANE)
    return pl.pallas_call(
        functools.partial(_gdn_gate_kernel, t_real=t_real),
        grid=(b,),
        in_specs=[pl.BlockSpec((None, t, LANE), lambda i: (i, 0, 0)),
                  pl.BlockSpec((1, LANE), lambda i: (0, 0)),
                  pl.BlockSpec((1, LANE), lambda i: (0, 0))],
        out_specs=pl.BlockSpec((None, t, LANE), lambda i: (i, 0, 0)),
        out_shape=jax.ShapeDtypeStruct((b, t, LANE), F32),
        compiler_params=pltpu.CompilerParams(dimension_semantics=("parallel",)),
        name="gdn_gates",
    )(gate3, a_vec, dt_vec)


def _pad_rows(x, rows):
    if x.shape[0] == rows:
        return x
    return jnp.concatenate([x, jnp.zeros((rows - x.shape[0], x.shape[1]), x.dtype)], axis=0)


def _gdn_delta_kernel(q_ref, k_ref, v_ref, z_ref, gb_ref, gt_ref, s0_ref, nw_ref, o_ref, s_out_ref, s_ref, *, nc, tl):
    hg = pl.program_id(1)
    c = pl.program_id(2)
    L = GDN_CHUNK
    hb = GDN_HEADS_PER_STEP

    @pl.when(c == 0)
    def _():
        s_ref[...] = s0_ref[...]

    gs = GDN_GROUP
    n_groups = hb // gs
    rows = gs * L
    shift = L.bit_length() - 1
    gb = gb_ref[...]
    lane = lax.broadcasted_iota(jnp.int32, (L, LANE), 1)
    ri = lax.broadcasted_iota(jnp.int32, (rows, rows), 0)
    cj = lax.broadcasted_iota(jnp.int32, (rows, rows), 1)
    same = (ri >> shift) == (cj >> shift)
    strict = same & (ri > cj)
    incl = same & (ri >= cj)

    def stack(ref, width_of):
        return [jnp.concatenate([_pad_rows(ref[:, width_of(hh) * LANE:(width_of(hh) + 1) * LANE], L)
                                 for hh in range(g * gs, (g + 1) * gs)], axis=0) for g in range(n_groups)]

    q = stack(q_ref, lambda hh: hh // 2)
    k = stack(k_ref, lambda hh: hh // 2)
    v = stack(v_ref, lambda hh: hh)
    bcol, gcol, grow, eg, a_qk, pw, x = [], [], [], [], [], [], []
    for g in range(n_groups):
        heads = [hg * hb + g * gs + hh for hh in range(gs)]
        bcol.append(jnp.concatenate(
            [jnp.sum(jnp.where(lane == h, gb, 0.0), axis=-1, keepdims=True) for h in heads], axis=0))
        gcol.append(jnp.concatenate(
            [jnp.sum(jnp.where(lane == h + GDN_V_HEADS, gb, 0.0), axis=-1, keepdims=True) for h in heads], axis=0))
        grow.append(gt_ref[pl.ds(c, 1), g * rows:(g + 1) * rows])
        decay = jnp.exp(jnp.where(same, gcol[g] - grow[g], 0.0))
        kb = k[g].astype(BF16)
        kk = lax.dot_general(kb, kb, NT_DIMS, preferred_element_type=F32)
        qk = lax.dot_general(q[g].astype(BF16), kb, NT_DIMS, preferred_element_type=F32)
        a_qk.append((qk * jnp.where(incl, decay, 0.0)).astype(BF16))
        eg.append(jnp.exp(gcol[g]))
        pw.append(_split_bf16(-(bcol[g] * kk * jnp.where(strict, decay, 0.0))))
        x.append(jnp.concatenate([bcol[g] * v[g], (bcol[g] * eg[g]) * k[g]], axis=1))
    for i in range(shift):
        for g in range(n_groups):
            x[g] = x[g] + _dot_split(pw[g], _split_bf16(x[g]))
        if i + 1 < shift:
            for g in range(n_groups):
                pw[g] = _split_bf16(_dot_split(pw[g], pw[g]))
    for g in range(n_groups):
        u_eff, w_kb = x[g][:, :GDN_DV], x[g][:, GDN_DV:].astype(BF16)
        sbs = [s_ref[g * gs + hh].astype(BF16) for hh in range(gs)]
        u = jnp.concatenate(
            [u_eff[hh * L:(hh + 1) * L] - jnp.dot(w_kb[hh * L:(hh + 1) * L], sbs[hh], preferred_element_type=F32)
             for hh in range(gs)], axis=0)
        ub = u.astype(BF16)
        o_intra = jnp.dot(a_qk[g], ub, preferred_element_type=F32)
        q_dec = (q[g] * eg[g]).astype(BF16)
        for hh in range(gs):
            hr = slice(hh * L, (hh + 1) * L)
            hi = g * gs + hh
            o = jnp.dot(q_dec[hr], sbs[hh], preferred_element_type=F32) + o_intra[hr]
            g_last = grow[g][:, hh * L + L - 1:hh * L + L]
            k_dec = (k[g][hr] * jnp.exp(g_last - gcol[g][hr])).astype(BF16)
            s_ref[hi] = jnp.exp(g_last) * s_ref[hi] + lax.dot_general(k_dec, ub[hr], (((0,), (0,)), ((), ())),
                                                                      preferred_element_type=F32)
            on = o * lax.rsqrt(jnp.mean(o * o, axis=-1, keepdims=True) + RMS_EPS) * nw_ref[...]
            gated = on[:tl] * jax.nn.silu(z_ref[:, hi * LANE:(hi + 1) * LANE])
            o_ref[:, hi * LANE:(hi + 1) * LANE] = gated.astype(o_ref.dtype)

    @pl.when(c == nc - 1)
    def _():
        s_out_ref[...] = s_ref[...]


def gdn_delta(qkv, proj3, gb, gt, s0, norm_w, tl):
    b, t, _ = qkv.shape
    nc = gb.shape[1] // GDN_CHUNK
    hb = GDN_HEADS_PER_STEP
    kw = (hb // 2) * GDN_DK
    vw = hb * GDN_DV
    kern = functools.partial(_gdn_delta_kernel, nc=nc, tl=tl)
    return pl.pallas_call(
        kern,
        grid=(b, GDN_V_HEADS // hb, nc),
        in_specs=[
            pl.BlockSpec((None, tl, kw), lambda i, h, c: (i, c, h)),
            pl.BlockSpec((None, tl, kw), lambda i, h, c: (i, c, GDN_KDIM // kw + h)),
            pl.BlockSpec((None, tl, vw), lambda i, h, c: (i, c, 2 * GDN_KDIM // vw + h)),
            pl.BlockSpec((None, tl, vw), lambda i, h, c: (i, c, GDN_CONV_DIM // vw + h)),
            pl.BlockSpec((None, GDN_CHUNK, LANE), lambda i, h, c: (i, c, 0)),
            pl.BlockSpec((None, None, nc, hb * GDN_CHUNK), lambda i, h, c: (i, h, 0, 0)),
            pl.BlockSpec((None, hb, GDN_DK, GDN_DV), lambda i, h, c: (i, h, 0, 0)),
            pl.BlockSpec((1, GDN_DV), lambda i, h, c: (0, 0)),
        ],
        out_specs=[
            pl.BlockSpec((None, tl, vw), lambda i, h, c: (i, c, h)),
            pl.BlockSpec((None, hb, GDN_DK, GDN_DV), lambda i, h, c: (i, h, 0, 0)),
        ],
        out_shape=[jax.ShapeDtypeStruct((b, t, GDN_VDIM), BF16),
                   jax.ShapeDtypeStruct(s0.shape, F32)],
        scratch_shapes=[pltpu.VMEM((hb, GDN_DK, GDN_DV), F32)],
        compiler_params=pltpu.CompilerParams(dimension_semantics=("parallel", "parallel", "arbitrary"),
                                             vmem_limit_bytes=VMEM_LIMIT),
        name="gdn_delta",
    )(qkv, qkv, qkv, proj3, gb, gt, s0, norm_w.reshape(1, GDN_DV).astype(F32))


def gdn_mixer(h2d, b, t, s0, conv_prefix, norm_in, w_in, conv_w, a_log, dt_bias, norm_w, w_out):
    w_main, w_gate = w_in
    proj3 = proj(h2d, w_main, norm_w=norm_in, name="gdn_in").reshape(b, t, -1)
    gate3 = proj(h2d, w_gate, norm_w=norm_in, name="gdn_in_gate").reshape(b, t, -1)
    keep = conv_w.shape[1] - 1
    assert t >= keep
    new_prefix = proj3[:, t - keep:, :GDN_CONV_DIM]
    qkv = gdn_conv(proj3, conv_prefix, conv_w)
    tl = min(t, GDN_CHUNK)
    t_pad = -(-t // GDN_CHUNK) * GDN_CHUNK
    gb = gdn_gates(jnp.pad(gate3, ((0, 0), (0, t_pad - t), (0, 0))), a_log, dt_bias, t)
    nc = t_pad // GDN_CHUNK
    hb = GDN_HEADS_PER_STEP
    gt = gb[:, :, GDN_V_HEADS:2 * GDN_V_HEADS].reshape(b, nc, GDN_CHUNK, GDN_V_HEADS // hb, hb)
    gt = gt.transpose(0, 3, 1, 4, 2).reshape(b, GDN_V_HEADS // hb, nc, hb * GDN_CHUNK)
    o, s_new = gdn_delta(qkv, proj3, gb, gt, s0.astype(F32), norm_w, tl)
    y = proj(o.reshape(b * t, GDN_VDIM), w_out, res=h2d, name="gdn_out")
    return y, s_new, new_prefix


def _compress_mlp(get_x, w1_at, pe, w2, k_norm, acc_ref, s_all):
    acc_ref[0:s_all, :] = jnp.zeros((s_all, 2 * CMP_HIDDEN), F32)

    def body(r, carry):
        acc_ref[0:s_all, :] += jnp.dot(get_x(r), w1_at(r), preferred_element_type=F32)
        return carry

    lax.fori_loop(0, CMP_STRIDE, body, 0)
    hid = acc_ref[0:s_all, 0:CMP_HIDDEN] + acc_ref[1:s_all + 1, CMP_HIDDEN:2 * CMP_HIDDEN] + pe
    y = jnp.dot(jax.nn.gelu(hid).astype(BF16), w2, preferred_element_type=F32)
    if k_norm is None:
        return y
    return y * lax.rsqrt(jnp.mean(y * y, axis=-1, keepdims=True) + RMS_EPS) * k_norm


def _compress_kernel(*refs, s_all):
    x_refs = refs[:NSA_KV_HEADS]
    w1_ref, pe_ref, w2_ref, nw_ref, o_ref, xs_ref, acc_ref = refs[NSA_KV_HEADS:]
    for r in range(CMP_STRIDE):
        for g in range(NSA_KV_HEADS):
            xs_ref[r, g] = x_refs[g][pl.ds(r, s_all, stride=CMP_STRIDE), :].astype(BF16)
    is_k = pl.program_id(1) == 0
    acc_ref[s_all:s_all + SUBLANE, :] = jnp.zeros((SUBLANE, 2 * CMP_HIDDEN), F32)
    for g in range(NSA_KV_HEADS):
        y = _compress_mlp(lambda r: xs_ref[r, g], lambda r: w1_ref[r], pe_ref[...], w2_ref[...], None, acc_ref,
                          s_all)
        normed = y * lax.rsqrt(jnp.mean(y * y, axis=-1, keepdims=True) + RMS_EPS) * nw_ref[...]
        o_ref[g] = jnp.where(is_k, normed, y)


def nsa_compress(x, col0, w1r, pe_hid, w2, k_norm0):
    b, rows, _ = x.shape
    s_all = rows // CMP_STRIDE

    def x_spec(g):
        return pl.BlockSpec((None, rows, NSA_HD), lambda i, c: (i, 0, (col0 + c) * NSA_KV_HEADS + g))

    in_specs = [x_spec(g) for g in range(NSA_KV_HEADS)] + [
        pl.BlockSpec((None, CMP_STRIDE, NSA_HD, 2 * CMP_HIDDEN), lambda i, c: (c, 0, 0, 0)),
        pl.BlockSpec((None, 1, CMP_HIDDEN), lambda i, c: (c, 0, 0)),
        pl.BlockSpec((None, CMP_HIDDEN, NSA_HD), lambda i, c: (c, 0, 0)),
        pl.BlockSpec((1, NSA_HD), lambda i, c: (0, 0)),
    ]
    return pl.pallas_call(
        functools.partial(_compress_kernel, s_all=s_all),
        grid=(b, 2),
        in_specs=in_specs,
        out_specs=pl.BlockSpec((None, None, NSA_KV_HEADS, s_all, NSA_HD), lambda i, c: (i, c, 0, 0, 0)),
        out_shape=jax.ShapeDtypeStruct((b, 2, NSA_KV_HEADS, s_all, NSA_HD), F32),
        scratch_shapes=[pltpu.VMEM((CMP_STRIDE, NSA_KV_HEADS, s_all, NSA_HD), BF16),
                        pltpu.VMEM((s_all + SUBLANE, 2 * CMP_HIDDEN), F32)],
        compiler_params=pltpu.CompilerParams(dimension_semantics=("parallel", "parallel"),
                                             vmem_limit_bytes=VMEM_LIMIT),
        name="nsa_compress",
    )(*((x,) * NSA_KV_HEADS), w1r, pe_hid, w2, k_norm0.reshape(1, NSA_HD).astype(F32))


CMP_FINAL_ROWS = 512


def _compress_paged_kernel(*refs, n_steps, pps):
    x_refs = refs[1:1 + pps]
    w1_ref, pe_ref, w2_ref, nw_ref, o_ref, acc_ref = refs[1 + pps:]
    p = pl.program_id(1)
    planes = 2 * NSA_KV_HEADS
    sp = PAGE_SIZE // CMP_STRIDE
    rows_pp = sp * planes
    n_rows = n_steps * pps * rows_pp
    hid_w = CMP_HIDDEN
    lhs = jnp.concatenate(
        [jnp.concatenate([x[pl.ds(r, sp, stride=CMP_STRIDE), :, :].reshape(rows_pp, NSA_HD).astype(BF16)
                          for r in range(CMP_STRIDE)], axis=1) for x in x_refs], axis=0)
    row0 = pl.multiple_of(p * (pps * rows_pp), pps * rows_pp)
    acc_ref[pl.ds(row0, pps * rows_pp), :] = jnp.dot(lhs, w1_ref[...], preferred_element_type=F32)

    @pl.when(p == n_steps - 1)
    def _():
        acc_ref[n_rows:n_rows + planes, :] = jnp.zeros((planes, 4 * hid_w), F32)
        cr = CMP_FINAL_ROWS
        is_k = (lax.broadcasted_iota(jnp.int32, (cr, 1), 0) % planes) < NSA_KV_HEADS

        def chunk(ci, carry):
            r0 = pl.multiple_of(ci * cr, cr)
            a = acc_ref[pl.ds(r0, cr), :]
            nxt = acc_ref[pl.ds(r0 + planes, cr), :]
            hid_k = a[:, 0:hid_w] + nxt[:, hid_w:2 * hid_w] + pe_ref[0]
            hid_v = a[:, 2 * hid_w:3 * hid_w] + nxt[:, 3 * hid_w:4 * hid_w] + pe_ref[1]
            act = jax.nn.gelu(jnp.where(is_k, hid_k, hid_v)).astype(BF16)
            y2 = jnp.dot(act, w2_ref[...], preferred_element_type=F32)
            y = jnp.where(is_k, y2[:, :NSA_HD], y2[:, NSA_HD:])
            normed = y * lax.rsqrt(jnp.mean(y * y, axis=-1, keepdims=True) + RMS_EPS) * nw_ref[...]
            o_ref[pl.ds(r0, cr), :] = jnp.where(is_k, normed, y)
            return carry

        lax.fori_loop(0, n_rows // cr, chunk, 0)


def nsa_compress_paged(pool4, page_table, w1r, pe_hid, w2, k_norm0):
    b, n_pages = page_table.shape
    pps = PAGES_PER_STEP
    assert n_pages % pps == 0
    n_steps = n_pages // pps
    planes = 2 * NSA_KV_HEADS
    s_all = n_pages * (PAGE_SIZE // CMP_STRIDE)
    n_rows = s_all * planes
    assert n_rows % CMP_FINAL_ROWS == 0
    w1cat = w1r.transpose(1, 2, 0, 3).reshape(CMP_STRIDE * NSA_HD, 4 * CMP_HIDDEN)
    w2cat = jnp.concatenate([w2[0], w2[1]], axis=1)

    def page_spec(u):
        return pl.BlockSpec((None, PAGE_SIZE, planes, NSA_HD), lambda i, p, pt: (pt[i, p * pps + u], 0, 0, 0))

    in_specs = [page_spec(u) for u in range(pps)] + [
        pl.BlockSpec(w1cat.shape, lambda i, p, pt: (0, 0)),
        pl.BlockSpec((2, 1, CMP_HIDDEN), lambda i, p, pt: (0, 0, 0)),
        pl.BlockSpec(w2cat.shape, lambda i, p, pt: (0, 0)),
        pl.BlockSpec((1, NSA_HD), lambda i, p, pt: (0, 0)),
    ]
    grid_spec = pltpu.PrefetchScalarGridSpec(
        num_scalar_prefetch=1, grid=(b, n_steps), in_specs=in_specs,
        out_specs=pl.BlockSpec((None, n_rows, NSA_HD), lambda i, p, pt: (i, 0, 0)),
        scratch_shapes=[pltpu.VMEM((n_rows + planes, 4 * CMP_HIDDEN), F32)])
    out = pl.pallas_call(
        functools.partial(_compress_paged_kernel, n_steps=n_steps, pps=pps),
        grid_spec=grid_spec,
        out_shape=jax.ShapeDtypeStruct((b, n_rows, NSA_HD), F32),
        compiler_params=pltpu.CompilerParams(dimension_semantics=("parallel", "arbitrary"),
                                             vmem_limit_bytes=VMEM_LIMIT),
        name="nsa_compress_paged",
    )(page_table, *((pool4,) * pps), w1cat, pe_hid, w2cat, k_norm0.reshape(1, NSA_HD).astype(F32))
    return out.reshape(b, s_all, 2, NSA_KV_HEADS, NSA_HD).transpose(0, 2, 3, 1, 4)


def _stack_heads(qb):
    return jnp.concatenate([qb[:, j * LANE:(j + 1) * LANE] for j in range(NSA_GROUP)], axis=0)


def _cmp_attn_kernel(q_ref, kc_ref, vc_ref, cb_ref, gate_ref, ovl_ref, o_ref, sel_ref, *, tq, q0, n_sb, gpb):
    i = pl.program_id(2)
    s_all = kc_ref.shape[1]
    nsbp = sel_ref.shape[-1]
    width = NSA_GROUP * LANE
    ovl = ovl_ref[...]
    imps = []
    for gg in range(gpb):
        q4 = _stack_heads(q_ref[:, gg * width:(gg + 1) * width]).astype(BF16)
        s = lax.dot_general(q4, kc_ref[gg].astype(BF16), NT_DIMS, preferred_element_type=F32)
        bias = cb_ref[gg * NSA_GROUP:(gg + 1) * NSA_GROUP].reshape(NSA_GROUP * tq, s_all)
        s = s + bias
        ok = bias > 0.5 * NEG_INF
        m = jnp.max(s, axis=-1, keepdims=True)
        e = jnp.exp(s - m)
        p = jnp.where(ok, e / jnp.sum(e, axis=-1, keepdims=True), 0.0)
        o4 = jnp.dot(p.astype(BF16), vc_ref[gg].astype(BF16), preferred_element_type=F32)
        for j in range(NSA_GROUP):
            col = gg * width + j * LANE
            o_ref[:, col:col + LANE] = o4[j * tq:(j + 1) * tq] * gate_ref[gg, :, j:j + 1]
        psum = p[0:tq] + p[tq:2 * tq] + p[2 * tq:3 * tq] + p[3 * tq:4 * tq]
        p_hi = psum.astype(BF16)
        r1 = psum - p_hi.astype(F32)
        p_mid = r1.astype(BF16)
        p_lo = (r1 - p_mid.astype(F32)).astype(BF16)
        imps.append(jnp.dot(p_hi, ovl, preferred_element_type=F32) + jnp.dot(p_mid, ovl, preferred_element_type=F32)
                    + jnp.dot(p_lo, ovl, preferred_element_type=F32))
    imp = jnp.concatenate(imps, axis=0)

    lane1 = lax.broadcasted_iota(jnp.int32, (tq, nsbp), 1)
    qpos1 = q0 + i * tq + lax.broadcasted_iota(jnp.int32, (tq, nsbp), 0)
    lane = jnp.concatenate([lane1] * gpb, axis=0)
    qpos = jnp.concatenate([qpos1] * gpb, axis=0)
    cur = qpos >> SEL_SHIFT
    forced = (lane == 0) | (lane == cur) | (lane == cur - 1)
    sb_ok = (lane << SEL_SHIFT) <= qpos
    score = jnp.where(sb_ok, imp + jnp.where(forced, FORCE_BONUS, 0.0), NEG_INF)
    work = jnp.where(lane < n_sb, score, -jnp.inf)
    lane_f = lane.astype(F32)
    selneg = jnp.full((gpb * tq, nsbp), NEG_INF, F32)
    for _ in range(N_SEL):
        mx = jnp.max(work, axis=-1, keepdims=True)
        first = jnp.min(jnp.where(work == mx, lane_f, 1e9), axis=-1, keepdims=True)
        hit = lane_f == first
        selneg = jnp.where(hit & (mx > 0.5 * NEG_INF), 0.0, selneg)
        work = jnp.where(hit, -jnp.inf, work)
    for gg in range(gpb):
        sel_ref[gg] = selneg[gg * tq:(gg + 1) * tq]


def nsa_cmp_attention(proj3, kvc, cmp_bias, gates_r, ovl, tq, q0, n_sb):
    b, t, _ = proj3.shape
    s_all = kvc.shape[3]
    nsbp = ovl.shape[1]
    gpb = NSA_KV_HEADS if tq * NSA_KV_HEADS <= CMP_TOPK_ROWS else 1
    kern = functools.partial(_cmp_attn_kernel, tq=tq, q0=q0, n_sb=n_sb, gpb=gpb)
    return pl.pallas_call(
        kern,
        grid=(b, NSA_KV_HEADS // gpb, t // tq),
        in_specs=[
            pl.BlockSpec((None, tq, gpb * NSA_GROUP * NSA_HD), lambda i, g, q: (i, q, g)),
            pl.BlockSpec((None, None, gpb, s_all, NSA_HD), lambda i, g, q: (i, 0, g, 0, 0)),
            pl.BlockSpec((None, None, gpb, s_all, NSA_HD), lambda i, g, q: (i, 1, g, 0, 0)),
            pl.BlockSpec((gpb * NSA_GROUP, tq, s_all), lambda i, g, q: (g, q, 0)),
            pl.BlockSpec((None, gpb, tq, 3 * NSA_GROUP), lambda i, g, q: (i, g, q, 0)),
            pl.BlockSpec((s_all, nsbp), lambda i, g, q: (0, 0)),
        ],
        out_specs=[
            pl.BlockSpec((None, tq, gpb * NSA_GROUP * NSA_HD), lambda i, g, q: (i, q, g)),
            pl.BlockSpec((None, gpb, tq, nsbp), lambda i, g, q: (i, g, q, 0)),
        ],
        out_shape=[jax.ShapeDtypeStruct((b, t, NSA_QDIM), F32),
                   jax.ShapeDtypeStruct((b, NSA_KV_HEADS, t, nsbp), F32)],
        compiler_params=pltpu.CompilerParams(dimension_semantics=("parallel", "parallel", "parallel"),
                                             vmem_limit_bytes=VMEM_LIMIT),
        name="nsa_cmp_attn",
    )(proj3, kvc, kvc, cmp_bias, gates_r, ovl)


def _softmax_tile_update(s, v, m_prev, l_prev, acc_prev):
    m_new = jnp.maximum(m_prev, jnp.max(s, axis=-1, keepdims=True))
    alpha = jnp.exp(m_prev - m_new)
    p = jnp.exp(s - m_new)
    l_new = alpha * l_prev + jnp.sum(p, axis=-1, keepdims=True)
    acc_new = alpha * acc_prev + jnp.dot(p.astype(BF16), v, preferred_element_type=F32)
    return m_new, l_new, acc_new


def _tile_scores(q4, k, bias3, sel, tq, tk, qpos0, kpos0, window):
    s = lax.dot_general(q4, k, NT_DIMS, preferred_element_type=F32).reshape(NSA_GROUP, tq, tk) + bias3
    kpos = kpos0 + lax.broadcasted_iota(jnp.int32, (tq, tk), 1)
    qpos = qpos0 + lax.broadcasted_iota(jnp.int32, (tq, tk), 0)
    dist = qpos - kpos
    mask = dist >= 0
    if window is not None:
        mask = mask & (dist < window)
    if sel is not None:
        nsbp = sel.shape[1]
        blk = (kpos0 + lax.broadcasted_iota(jnp.int32, (nsbp, tk), 1)) >> SEL_SHIFT
        onehot = jnp.where(blk == lax.broadcasted_iota(jnp.int32, (nsbp, tk), 0), 1.0, 0.0).astype(BF16)
        shared = jnp.where(mask, jnp.dot(sel.astype(BF16), onehot, preferred_element_type=F32), NEG_INF)
    else:
        shared = jnp.where(mask, 0.0, NEG_INF)
    return (s + shared[None]).reshape(NSA_GROUP * tq, tk)


def _flash_prompt_kernel(*refs, tq, tk, nkk, use_sel, window, gcol):
    if use_sel:
        q_ref, k_ref, v_ref, b_ref, gate_ref, sel_ref, o_ref, q4_ref, m_ref, l_ref, acc_ref = refs
    else:
        q_ref, k_ref, v_ref, b_ref, gate_ref, o_ref, q4_ref, m_ref, l_ref, acc_ref = refs
        sel_ref = None
    i = pl.program_id(2)
    jj = pl.program_id(3)
    if use_sel:
        j, valid = jj, jj <= i
    else:
        j = i - (nkk - 1) + jj
        valid = j >= 0

    @pl.when(jj == 0)
    def _():
        q4_ref[...] = _stack_heads(q_ref[...]).astype(BF16)
        m_ref[...] = jnp.full(m_ref.shape, NEG_INF, F32)
        l_ref[...] = jnp.zeros(l_ref.shape, F32)
        acc_ref[...] = jnp.zeros(acc_ref.shape, F32)

    @pl.when(valid)
    def _():
        sel = sel_ref[...] if use_sel else None
        s = _tile_scores(q4_ref[...], k_ref[...].astype(BF16), b_ref[...], sel, tq, tk, i * tq, j * tk, window)
        m_new, l_new, acc_new = _softmax_tile_update(s, v_ref[...].astype(BF16), m_ref[...], l_ref[...],
                                                     acc_ref[...])
        m_ref[...] = m_new
        l_ref[...] = l_new
        acc_ref[...] = acc_new

    @pl.when(jj == nkk - 1)
    def _():
        o4 = acc_ref[...] / l_ref[...]
        for h in range(NSA_GROUP):
            o_ref[:, h * LANE:(h + 1) * LANE] = o4[h * tq:(h + 1) * tq] * gate_ref[:, gcol + h:gcol + h + 1]


def nsa_flash_prompt(proj3, bias_tab, gates_r, sel, *, tile, kcol, vcol, window, gcol, name):
    b, t, _ = proj3.shape
    nq = t // tile
    n_dd = bias_tab.shape[1]
    use_sel = sel is not None
    nkk = nq if use_sel else (window + tile - 1) // tile + 1

    def kidx(q, jj):
        return jnp.minimum(jj, q) if use_sel else jnp.maximum(q - (nkk - 1) + jj, 0)

    in_specs = [
        pl.BlockSpec((None, tile, NSA_GROUP * NSA_HD), lambda i, g, q, jj: (i, q, g)),
        pl.BlockSpec((None, tile, NSA_HD), lambda i, g, q, jj: (i, kidx(q, jj), kcol + g)),
        pl.BlockSpec((None, tile, NSA_HD), lambda i, g, q, jj: (i, kidx(q, jj), vcol + g)),
        pl.BlockSpec((NSA_GROUP, None, tile, tile),
                     lambda i, g, q, jj: (g, jnp.minimum(q - kidx(q, jj), n_dd - 1), 0, 0)),
        pl.BlockSpec((None, None, tile, 3 * NSA_GROUP), lambda i, g, q, jj: (i, g, q, 0)),
    ]
    args = [proj3, proj3, proj3, bias_tab, gates_r]
    if use_sel:
        nsbp = sel.shape[-1]
        in_specs.append(pl.BlockSpec((None, None, tile, nsbp), lambda i, g, q, jj: (i, g, q, 0)))
        args.append(sel)
    kern = functools.partial(_flash_prompt_kernel, tq=tile, tk=tile, nkk=nkk, use_sel=use_sel, window=window,
                             gcol=gcol)
    return pl.pallas_call(
        kern,
        grid=(b, NSA_KV_HEADS, nq, nkk),
        in_specs=in_specs,
        out_specs=pl.BlockSpec((None, tile, NSA_GROUP * NSA_HD), lambda i, g, q, jj: (i, q, g)),
        out_shape=jax.ShapeDtypeStruct((b, t, NSA_QDIM), F32),
        scratch_shapes=[pltpu.VMEM((NSA_GROUP * tile, NSA_HD), BF16),
                        pltpu.VMEM((NSA_GROUP * tile, 1), F32),
                        pltpu.VMEM((NSA_GROUP * tile, 1), F32),
                        pltpu.VMEM((NSA_GROUP * tile, NSA_HD), F32)],
        compiler_params=pltpu.CompilerParams(
            dimension_semantics=("parallel", "parallel", "parallel", "arbitrary"), vmem_limit_bytes=VMEM_LIMIT),
        name=name,
    )(*args)


def _flash_sample_kernel(*refs, tq, pps, n_steps, plane0, qtile, kbase, q0, use_sel, window, gcol):
    pt_ref, q_ref, pool_ref, kn_ref, vn_ref, b_ref, gate_ref = refs[:7]
    if use_sel:
        sel_ref, o_ref, kv_ref, sem, q4_ref, m_ref, l_ref, acc_ref = refs[7:]
    else:
        o_ref, kv_ref, sem, q4_ref, m_ref, l_ref, acc_ref = refs[7:]
        sel_ref = None
    i = pl.program_id(0)
    planes = 2 * NSA_KV_HEADS
    page = kv_ref.shape[3]
    tk = pps * page
    n_dd = b_ref.shape[1]

    def page_copies(step, slot):
        return [pltpu.make_async_copy(pool_ref.at[pt_ref[i, step * pps + u], :, plane0 + j, :],
                                      kv_ref.at[slot, u, j], sem.at[slot])
                for u in range(pps) for j in range(planes)]

    for cp in page_copies(0, 0):
        cp.start()
    for g in range(NSA_KV_HEADS):
        q4_ref[g] = _stack_heads(q_ref[:, g * NSA_GROUP * LANE:(g + 1) * NSA_GROUP * LANE]).astype(BF16)
    m_ref[...] = jnp.full(m_ref.shape, NEG_INF, F32)
    l_ref[...] = jnp.zeros(l_ref.shape, F32)
    acc_ref[...] = jnp.zeros(acc_ref.shape, F32)

    def update(g, k, v, bias3, kpos0):
        sel = sel_ref[g] if use_sel else None
        s = _tile_scores(q4_ref[g], k, bias3, sel, tq, k.shape[0], q0, kpos0, window)
        m_new, l_new, acc_new = _softmax_tile_update(s, v, m_ref[g], l_ref[g], acc_ref[g])
        m_ref[g] = m_new
        l_ref[g] = l_new
        acc_ref[g] = acc_new

    def step_body(step, carry):
        slot = step % 2

        @pl.when(step + 1 < n_steps)
        def _():
            for cp in page_copies(step + 1, 1 - slot):
                cp.start()

        for cp in page_copies(step, slot):
            cp.wait()
        dd = jnp.clip(qtile - step, 0, n_dd - 1)
        scores = []
        for g in range(NSA_KV_HEADS):
            k = jnp.concatenate([kv_ref[slot, u, g].astype(BF16) for u in range(pps)], axis=0)
            sel = sel_ref[g] if use_sel else None
            scores.append(_tile_scores(q4_ref[g], k, b_ref[g * NSA_GROUP:(g + 1) * NSA_GROUP, dd], sel, tq, tk, q0,
                                       kbase + step * tk, window))
        for g in range(NSA_KV_HEADS):
            v = jnp.concatenate([kv_ref[slot, u, NSA_KV_HEADS + g].astype(BF16) for u in range(pps)], axis=0)
            m_new, l_new, acc_new = _softmax_tile_update(scores[g], v, m_ref[g], l_ref[g], acc_ref[g])
            m_ref[g] = m_new
            l_ref[g] = l_new
            acc_ref[g] = acc_new
        return carry

    lax.fori_loop(0, n_steps, step_body, 0)

    pad = jnp.zeros((page - tq, LANE), F32)
    for g in range(NSA_KV_HEADS):
        cols = slice(g * LANE, (g + 1) * LANE)
        kn = jnp.concatenate([kn_ref[:, cols], pad], axis=0).astype(BF16)
        vn = jnp.concatenate([vn_ref[:, cols], pad], axis=0).astype(BF16)
        update(g, kn, vn, b_ref[g * NSA_GROUP:(g + 1) * NSA_GROUP, 0, :, 0:page], q0)
        o4 = acc_ref[g] / l_ref[g]
        for j in range(NSA_GROUP):
            h = g * NSA_GROUP + j
            o_ref[:, h * LANE:(h + 1) * LANE] = o4[j * tq:(j + 1) * tq] * gate_ref[g, :, gcol + j:gcol + j + 1]


def nsa_flash_sample(proj3, pool4, table, bias_tab, gates_r, sel, *, plane0, newk, newv, kbase, q0, window,
                     gcol, name):
    b, tq, _ = proj3.shape
    page = pool4.shape[1]
    tk = bias_tab.shape[3]
    pps = tk // page
    n_steps = table.shape[1] // pps
    qtile = (q0 - kbase) // tk
    use_sel = sel is not None
    in_specs = [
        pl.BlockSpec((None, tq, NSA_QDIM), lambda i, pt: (i, 0, 0)),
        pl.BlockSpec(memory_space=pl.ANY),
        pl.BlockSpec((None, tq, NSA_KVDIM), lambda i, pt: (i, 0, newk)),
        pl.BlockSpec((None, tq, NSA_KVDIM), lambda i, pt: (i, 0, newv)),
        pl.BlockSpec(bias_tab.shape, lambda i, pt: (0, 0, 0, 0)),
        pl.BlockSpec((None, NSA_KV_HEADS, tq, 3 * NSA_GROUP), lambda i, pt: (i, 0, 0, 0)),
    ]
    args = [proj3, pool4, proj3, proj3, bias_tab, gates_r]
    if use_sel:
        nsbp = sel.shape[-1]
        in_specs.append(pl.BlockSpec((None, NSA_KV_HEADS, tq, nsbp), lambda i, pt: (i, 0, 0, 0)))
        args.append(sel)
    kern = functools.partial(_flash_sample_kernel, tq=tq, pps=pps, n_steps=n_steps, plane0=plane0, qtile=qtile,
                             kbase=kbase, q0=q0, use_sel=use_sel, window=window, gcol=gcol)
    rows4 = NSA_GROUP * tq
    grid_spec = pltpu.PrefetchScalarGridSpec(
        num_scalar_prefetch=1, grid=(b,), in_specs=in_specs,
        out_specs=pl.BlockSpec((None, tq, NSA_QDIM), lambda i, pt: (i, 0, 0)),
        scratch_shapes=[pltpu.VMEM((2, pps, 2 * NSA_KV_HEADS, page, NSA_HD), F32),
                        pltpu.SemaphoreType.DMA((2,)),
                        pltpu.VMEM((NSA_KV_HEADS, rows4, NSA_HD), BF16),
                        pltpu.VMEM((NSA_KV_HEADS, rows4, 1), F32),
                        pltpu.VMEM((NSA_KV_HEADS, rows4, 1), F32),
                        pltpu.VMEM((NSA_KV_HEADS, rows4, NSA_HD), F32)])
    return pl.pallas_call(
        kern, grid_spec=grid_spec, out_shape=jax.ShapeDtypeStruct((b, tq, NSA_QDIM), F32),
        compiler_params=pltpu.CompilerParams(dimension_semantics=("arbitrary",), vmem_limit_bytes=VMEM_LIMIT),
        name=name,
    )(table, *args)


def _bias_table(rel_bias, unit, tq, tk):
    n_dd = -(-(T5_SATURATION + tk - 1) // unit) + 1
    dist = (jnp.arange(n_dd)[:, None, None] * unit + jnp.arange(tq)[None, :, None]
            - jnp.arange(tk)[None, None, :])
    return _bucket_lookup(rel_bias, dist)


def _bucket_lookup(rel_bias, dist):
    onehot = jax.nn.one_hot(t5_bucket(dist), N_BUCKETS, dtype=F32)
    out = jnp.einsum('...b,bh->h...', onehot, rel_bias.astype(F32), precision=lax.Precision.HIGHEST)
    return out


def _cmp_tables(rel_bias, q0, t, s_all, nc, n_sb, nsbp):
    c = jnp.arange(s_all)
    c_end = c * CMP_STRIDE + (CMP_BLOCK - 1)
    dist = (q0 + jnp.arange(t))[:, None] - c_end[None, :]
    ok = (dist >= 0) & (c < nc)[None, :]
    bias = jnp.where(ok[None], _bucket_lookup(rel_bias, dist), NEG_INF)
    sb_start = jnp.arange(nsbp) * SEL_BLOCK
    c_start = c * CMP_STRIDE
    ovl = jnp.maximum(jnp.minimum(c_end[:, None], sb_start[None, :] + SEL_BLOCK - 1)
                      - jnp.maximum(c_start[:, None], sb_start[None, :]) + 1, 0).astype(F32) / CMP_BLOCK
    ovl = jnp.where((c < nc)[:, None] & (jnp.arange(nsbp) < n_sb)[None, :], ovl, 0.0)
    return bias, ovl.astype(BF16)


def _nsa_weights(w_in, q_norm, k_norm, cmp_pe, cmp_w1, cmp_w2, rel_bias, w_out):
    n_main = NSA_QDIM + 6 * NSA_KVDIM
    nsub = CMP_BLOCK // CMP_STRIDE
    w1r = cmp_w1.reshape(2, nsub, CMP_STRIDE, NSA_HD, CMP_HIDDEN)
    w1r = jnp.moveaxis(w1r, 1, 3).reshape(2, CMP_STRIDE, NSA_HD, nsub * CMP_HIDDEN).astype(BF16)
    pe_hid = jnp.einsum('ck,cke->ce', cmp_pe.reshape(2, -1), cmp_w1, precision=lax.Precision.HIGHEST)
    return dict(
        w_main=w_in[:, :n_main].astype(BF16),
        w_gate=_pad_cols(w_in[:, n_main:], LANE).astype(BF16),
        post=_nsa_in_post(q_norm, k_norm),
        w1r=w1r, pe_hid=pe_hid.reshape(2, 1, CMP_HIDDEN), w2=cmp_w2.astype(BF16), k_norm0=k_norm[0],
        rel_bias=rel_bias, w_out=w_out.astype(BF16))


def nsa_mixer(h2d, b, t, norm_in, nw, paged):
    post_main, post_gate = nw["post"]
    proj2 = proj(h2d, nw["w_main"], norm_w=norm_in, post=post_main, name="nsa_in")
    gate_out = proj(h2d, nw["w_gate"], norm_w=norm_in, post=post_gate, name="nsa_in_gate")
    proj3 = proj2.reshape(b, t, -1)
    gates_r = gate_out[:, :3 * NSA_HEADS].reshape(b, t, 3, NSA_KV_HEADS, NSA_GROUP)
    gates_r = gates_r.transpose(0, 3, 1, 2, 4).reshape(b, NSA_KV_HEADS, t, 3 * NSA_GROUP)
    kv_blk0 = NSA_QDIM // NSA_KVDIM
    kv_col0 = NSA_QDIM // NSA_HD
    rel_bias = nw["rel_bias"]
    if paged is None:
        p_len, tq = 0, 256
        kvc = nsa_compress(proj3, kv_blk0, nw["w1r"], nw["pe_hid"], nw["w2"], nw["k_norm0"])
    else:
        pool, page_table, win_pool = paged
        p_len, tq = page_table.shape[1] * PAGE_SIZE, t
        kvc = nsa_compress_paged(pool, page_table, nw["w1r"], nw["pe_hid"], nw["w2"], nw["k_norm0"])
    n_all = p_len + t
    s_all = kvc.shape[3]
    nc = n_all // CMP_STRIDE - CMP_BLOCK // CMP_STRIDE + 1
    n_sb = -(-n_all // SEL_BLOCK)
    nsbp = -(-n_sb // LANE) * LANE
    cmp_bias, ovl = _cmp_tables(rel_bias, p_len, t, s_all, nc, n_sb, nsbp)
    o_cmp, sel = nsa_cmp_attention(proj3, kvc, cmp_bias, gates_r, ovl, tq, p_len, n_sb)
    if paged is None:
        tab = _bias_table(rel_bias, tq, tq, tq)
        o_sel = nsa_flash_prompt(proj3, tab, gates_r, sel, tile=tq, kcol=kv_col0 + 2 * NSA_KV_HEADS,
                                 vcol=kv_col0 + 3 * NSA_KV_HEADS, window=None, gcol=NSA_GROUP, name="nsa_sel")
        o_win = nsa_flash_prompt(proj3, tab, gates_r, None, tile=tq, kcol=kv_col0 + 4 * NSA_KV_HEADS,
                                 vcol=kv_col0 + 5 * NSA_KV_HEADS, window=WINDOW, gcol=2 * NSA_GROUP, name="nsa_win")
    else:
        tk_sel = SEL_PAGES_PER_STEP * PAGE_SIZE
        assert p_len % tk_sel == 0
        tab_sel = _bias_table(rel_bias, tk_sel, t, tk_sel)
        tab = _bias_table(rel_bias, WINDOW, t, WINDOW)
        o_sel = nsa_flash_sample(proj3, pool, page_table, tab_sel, gates_r, sel, plane0=2 * NSA_KV_HEADS,
                                 newk=kv_blk0 + 2,
                                 newv=kv_blk0 + 3, kbase=0, q0=p_len, window=None, gcol=NSA_GROUP,
                                 name="nsa_sel_paged")
        n_wt = WINDOW // PAGE_SIZE
        win_table = jnp.arange(b * n_wt, dtype=jnp.int32).reshape(b, n_wt)
        o_win = nsa_flash_sample(proj3, win_pool, win_table, tab, gates_r, None, plane0=0, newk=kv_blk0 + 4,
                                 newv=kv_blk0 + 5, kbase=p_len - WINDOW, q0=p_len, window=WINDOW,
                                 gcol=2 * NSA_GROUP, name="nsa_win_paged")
    y = proj((o_cmp.reshape(b * t, NSA_QDIM), o_sel.reshape(b * t, NSA_QDIM), o_win.reshape(b * t, NSA_QDIM)),
             nw["w_out"], res=h2d, name="nsa_out")
    new_rows = proj3[:, :, NSA_QDIM:NSA_QDIM + 4 * NSA_KVDIM].reshape(b, t, 4, NSA_KV_HEADS, NSA_HD)
    new_win = proj3[:, :, NSA_QDIM + 4 * NSA_KVDIM:].reshape(b, t, 2, NSA_KV_HEADS, NSA_HD)
    return y, new_rows, new_win


def _pad_cols(w, n_pad):
    return jnp.pad(w, ((0, 0), (0, n_pad - w.shape[1])))


def _nsa_in_post(q_norm, k_norm):
    ones_kv = jnp.ones((NSA_KVDIM,), F32)
    zeros_kv = jnp.zeros((NSA_KVDIM,), F32)
    pw = jnp.concatenate([
        jnp.tile(q_norm.astype(F32) * (NSA_HD ** -0.5), NSA_HEADS),
        ones_kv, ones_kv, jnp.tile(k_norm[1].astype(F32), NSA_KV_HEADS), ones_kv,
        jnp.tile(k_norm[2].astype(F32), NSA_KV_HEADS), ones_kv])
    pm = jnp.concatenate([
        jnp.ones((NSA_QDIM,), F32),
        zeros_kv, zeros_kv, ones_kv, zeros_kv, ones_kv, zeros_kv])
    return (pw, pm), (jnp.ones((LANE,), F32), jnp.full((LANE,), 2.0, F32))


def kernel(x_prompt, x_sample, state_gdn, state_gdn_conv, cache_nsa_kv, state_nsa_win, state_ffn_conv, page_table,
           norm_mix, norm_ffn, gdn_w_in, gdn_conv_w, gdn_A_log, gdn_dt_bias, gdn_norm, gdn_w_out,
           nsa_w_in, nsa_q_norm, nsa_k_norm, nsa_cmp_pe, nsa_cmp_w1, nsa_cmp_w2, rel_bias, nsa_w_out,
           ffn_w_up, ffn_conv_w, ffn_conv_b, ffn_w_down):
    depth = norm_mix.shape[0]
    bp, tp, d = x_prompt.shape
    bs, ts, _ = x_sample.shape
    win_buf = state_nsa_win.shape[2]
    assert win_buf == WINDOW and tp >= WINDOW and cache_nsa_kv.shape[2] == PAGE_SIZE
    d_ff = ffn_conv_w.shape[1]
    conv_keep = ffn_conv_w.shape[2] - 1
    hp = x_prompt.reshape(bp * tp, d)
    hs = x_sample.reshape(bs * ts, d)
    gdn_p, gdnc_p, kv_p, win_p, ffn_p = [], [], [], [], []
    gdn_s, gdnc_s, kv_s, win_s, ffn_s = [], [], [], [], []
    for i in range(depth):
        j = i // 2
        if i % 2 == 0:
            n_main = GDN_CONV_DIM + GDN_VDIM
            w_in = (gdn_w_in[j][:, :n_main].astype(BF16), _pad_cols(gdn_w_in[j][:, n_main:], LANE).astype(BF16))
            gw = (norm_mix[i], w_in, gdn_conv_w[j], gdn_A_log[j], gdn_dt_bias[j], gdn_norm[j],
                  gdn_w_out[j].astype(BF16))
            hp, st_p, cv_p = gdn_mixer(hp, bp, tp, jnp.zeros((bp, GDN_V_HEADS, GDN_DK, GDN_DV), F32),
                                       jnp.zeros((bp, gdn_conv_w.shape[2] - 1, GDN_CONV_DIM), F32), *gw)
            hs, st_s, cv_s = gdn_mixer(hs, bs, ts, state_gdn[j], state_gdn_conv[j], *gw)
            gdn_p.append(st_p)
            gdnc_p.append(cv_p)
            gdn_s.append(st_s)
            gdnc_s.append(cv_s)
        else:
            nw = _nsa_weights(nsa_w_in[j], nsa_q_norm[j], nsa_k_norm[j], nsa_cmp_pe[j], nsa_cmp_w1[j],
                              nsa_cmp_w2[j], rel_bias, nsa_w_out[j])
            hp, rows_p, nwin_p = nsa_mixer(hp, bp, tp, norm_mix[i], nw, None)
            n_pool = cache_nsa_kv.shape[1]
            pool = cache_nsa_kv.reshape(cache_nsa_kv.shape[0] * n_pool, PAGE_SIZE, 4 * NSA_KV_HEADS, NSA_HD)
            win_pool = state_nsa_win[j].reshape(bs * (win_buf // PAGE_SIZE), PAGE_SIZE, 2 * NSA_KV_HEADS, NSA_HD)
            hs, rows_s, nwin_s = nsa_mixer(hs, bs, ts, norm_mix[i], nw, (pool, page_table + j * n_pool, win_pool))
            kv_p.append(rows_p)
            win_p.append(nwin_p[:, tp - WINDOW:])
            kv_s.append(rows_s)
            win_s.append(jnp.concatenate([state_nsa_win[j][:, ts:], nwin_s], axis=1))
        n_up = -(-ffn_w_up.shape[2] // 512) * 512
        fw = (norm_ffn[i], _pad_cols(ffn_w_up[i], n_up).astype(BF16), ffn_conv_w[i], ffn_conv_b[i], ffn_w_down[i].astype(BF16))
        hp, cp = conv_ffn(hp, bp, tp, jnp.zeros((bp, conv_keep, d_ff), F32), *fw)
        hs, cs = conv_ffn(hs, bs, ts, state_ffn_conv[i], *fw)
        ffn_p.append(cp)
        ffn_s.append(cs)
    return (hp.reshape(bp, tp, d), hs.reshape(bs, ts, d),
            jnp.stack(gdn_p), jnp.stack(gdnc_p), jnp.stack(kv_p), jnp.stack(win_p), jnp.stack(ffn_p),
            jnp.stack(gdn_s), jnp.stack(gdnc_s), jnp.stack(kv_s), jnp.stack(win_s), jnp.stack(ffn_s))
```

```python
import functools
import math

import jax
import jax.numpy as jnp
from jax import lax
from jax.experimental import pallas as pl
from jax.experimental.pallas import tpu as pltpu

F32 = jnp.float32
BF16 = jnp.bfloat16
RMS_EPS = 1e-6
NEG_INF = -1e30

LANE = 128
SUBLANE = 8
VMEM_LIMIT = 56 * 1024 * 1024
PROJ_VMEM_BUDGET = 40 * 1024 * 1024
FFN_DOWN_ROWS = 256

D_MODEL = 2048
GDN_K_HEADS = 16
GDN_V_HEADS = 32
GDN_DK = 128
GDN_DV = 128
GDN_KDIM = GDN_K_HEADS * GDN_DK
GDN_VDIM = GDN_V_HEADS * GDN_DV
GDN_CONV_DIM = 2 * GDN_KDIM + GDN_VDIM
GDN_CHUNK = 64

NSA_HEADS = 16
NSA_KV_HEADS = 4
NSA_HD = 128
NSA_GROUP = NSA_HEADS // NSA_KV_HEADS
NSA_QDIM = NSA_HEADS * NSA_HD
NSA_KVDIM = NSA_KV_HEADS * NSA_HD
CMP_BLOCK = 32
CMP_STRIDE = 16
CMP_HIDDEN = 2 * NSA_HD
SEL_BLOCK = 64
SEL_SHIFT = 6
N_SEL = 16
WINDOW = 512
FORCE_BONUS = 1e3
N_BUCKETS = 32
REL_MAX_DIST = 1024
PAGE_SIZE = 128
PAGES_PER_STEP = 8
SEL_PAGES_PER_STEP = 16
T5_SATURATION = 790
CMP_TOPK_ROWS = 256
NT_DIMS = (((1,), (1,)), ((), ()))


def _proj_kernel(*refs, n_x, has_norm, has_res, has_post):
    it = iter(refs)
    x_refs = [next(it) for _ in range(n_x)]
    nw_ref = next(it) if has_norm else None
    w_ref = next(it)
    res_ref = next(it) if has_res else None
    pw_ref = next(it) if has_post else None
    pm_ref = next(it) if has_post else None
    o_ref = next(it)
    xs_ref = next(it)

    @pl.when(pl.program_id(1) == 0)
    def _():
        x = x_refs[0][...].astype(F32)
        for r in x_refs[1:]:
            x = x + r[...].astype(F32)
        if has_norm:
            ms = jnp.mean(x * x, axis=-1, keepdims=True)
            x = x * lax.rsqrt(ms + RMS_EPS) * nw_ref[...]
        xs_ref[...] = x.astype(BF16)

    y = jnp.dot(xs_ref[...], w_ref[...].astype(BF16), preferred_element_type=F32)
    if has_res:
        y = y + res_ref[...]
    if has_post:
        tn = y.shape[1]
        for g in range(tn // LANE):
            sl = slice(g * LANE, (g + 1) * LANE)
            yg = y[:, sl]
            mode = pm_ref[:, sl]
            ms = jnp.mean(yg * yg, axis=-1, keepdims=True)
            normed = yg * lax.rsqrt(ms + RMS_EPS) * pw_ref[:, sl]
            sig = jax.nn.sigmoid(yg)
            o_ref[:, sl] = jnp.where(mode == 1.0, normed, jnp.where(mode == 2.0, sig, yg))
    else:
        o_ref[...] = y.astype(o_ref.dtype)


def _pick_tile(n, candidates):
    for c in candidates:
        if n % c == 0:
            return c
    return n


def proj(xs, w, *, norm_w=None, res=None, post=None, out_dtype=F32, n_cols=None, name="proj"):
    if not isinstance(xs, (tuple, list)):
        xs = (xs,)
    m, k = xs[0].shape
    n = w.shape[1] if n_cols is None else n_cols
    tn = _pick_tile(n, (512, 384, 256, 128))
    x_row_bytes = sum(k * x.dtype.itemsize for x in xs)
    w_bytes = w.dtype.itemsize

    def vmem_bytes(tm):
        return 2 * (tm * x_row_bytes + k * tn * w_bytes + tm * tn * 4 * (2 if res is not None else 1)) + tm * k * 2

    tm = next((c for c in (1024, 512, 256) if m % c == 0 and vmem_bytes(c) <= PROJ_VMEM_BUDGET), m)
    in_specs = [pl.BlockSpec((tm, k), lambda i, j: (i, 0)) for _ in xs]
    args = list(xs)
    if norm_w is not None:
        in_specs.append(pl.BlockSpec((1, k), lambda i, j: (0, 0)))
        args.append(norm_w.reshape(1, k).astype(F32))
    in_specs.append(pl.BlockSpec((k, tn), lambda i, j: (0, j)))
    args.append(w)
    if res is not None:
        in_specs.append(pl.BlockSpec((tm, tn), lambda i, j: (i, j)))
        args.append(res)
    if post is not None:
        for a in post:
            in_specs.append(pl.BlockSpec((1, tn), lambda i, j: (0, j)))
            args.append(a.reshape(1, n).astype(F32))
    kern = functools.partial(_proj_kernel, n_x=len(xs), has_norm=norm_w is not None, has_res=res is not None,
                             has_post=post is not None)
    return pl.pallas_call(
        kern,
        grid=(m // tm, n // tn),
        in_specs=in_specs,
        out_specs=pl.BlockSpec((tm, tn), lambda i, j: (i, j)),
        out_shape=jax.ShapeDtypeStruct((m, n), out_dtype),
        scratch_shapes=[pltpu.VMEM((tm, k), BF16)],
        compiler_params=pltpu.CompilerParams(dimension_semantics=("parallel", "arbitrary"),
                                             vmem_limit_bytes=VMEM_LIMIT),
        name=name,
    )(*args)


def _conv_gate(pad_ref, prev, gate, val, cw, cb, width):
    t = gate.shape[0]
    halo = SUBLANE
    pad_ref[halo - (width - 1):halo, :] = prev
    pad_ref[halo:halo + t, :] = gate
    acc = cb + pad_ref[halo - (width - 1):halo - (width - 1) + t, :] * cw[0:1, :]
    for i in range(1, width):
        off = halo - (width - 1) + i
        acc = acc + pad_ref[off:off + t, :] * cw[i:i + 1, :]
    return jax.nn.silu(acc) * val


def _ffn_gate_kernel(gate_ref, val_ref, pre_ref, cw_ref, cb_ref, o_ref, pad_ref, *, width):
    for i in range(gate_ref.shape[0]):
        hid = _conv_gate(pad_ref, pre_ref[i], gate_ref[i], val_ref[i], cw_ref[...], cb_ref[...], width)
        o_ref[i] = hid.astype(o_ref.dtype)


def ffn_gate(up, prefix, conv_w, conv_b, d_ff):
    b, t, _ = up.shape
    width = conv_w.shape[1]
    nblk = d_ff // LANE
    kern = functools.partial(_ffn_gate_kernel, width=width)
    return pl.pallas_call(
        kern,
        grid=(nblk,),
        in_specs=[
            pl.BlockSpec((b, t, LANE), lambda j: (0, 0, j)),
            pl.BlockSpec((b, t, LANE), lambda j: (0, 0, j + nblk)),
            pl.BlockSpec((b, width - 1, LANE), lambda j: (0, 0, j)),
            pl.BlockSpec((width, LANE), lambda j: (0, j)),
            pl.BlockSpec((1, LANE), lambda j: (0, j)),
        ],
        out_specs=pl.BlockSpec((b, t, LANE), lambda j: (0, 0, j)),
        out_shape=jax.ShapeDtypeStruct((b, t, d_ff), BF16),
        scratch_shapes=[pltpu.VMEM((t + SUBLANE, LANE), F32)],
        compiler_params=pltpu.CompilerParams(dimension_semantics=("parallel",)),
        name="ffn_gate",
    )(up, up, prefix, conv_w.T, conv_b.reshape(1, d_ff))


def _ffn_down_kernel(gate_ref, val_ref, halo_ref, pre_ref, cw_ref, cb_ref, w_ref, res_ref, o_ref, hid_ref, pad_ref, *,
                     width, tiles_per_batch):
    @pl.when(pl.program_id(1) == 0)
    def _():
        first = (pl.program_id(0) % tiles_per_batch) == 0

        def column_block(c, carry):
            cs = pl.ds(pl.multiple_of(c * LANE, LANE), LANE)
            prev = jnp.where(first, pre_ref[:, cs], halo_ref[SUBLANE - (width - 1):SUBLANE, cs])
            hid = _conv_gate(pad_ref, prev, gate_ref[:, cs], val_ref[:, cs], cw_ref[:, cs], cb_ref[:, cs], width)
            hid_ref[:, cs] = hid.astype(BF16)
            return carry

        lax.fori_loop(0, gate_ref.shape[1] // LANE, column_block, 0)

    o_ref[...] = jnp.dot(hid_ref[...], w_ref[...], preferred_element_type=F32) + res_ref[...]


def ffn_down_fused(up2d, b, t, prefix, conv_w, conv_b, w_down, res):
    d_ff, n = w_down.shape
    width = conv_w.shape[1]
    tm, tn = FFN_DOWN_ROWS, 512
    assert t % tm == 0 and n % tn == 0
    tiles_per_batch = t // tm
    kern = functools.partial(_ffn_down_kernel, width=width, tiles_per_batch=tiles_per_batch)
    return pl.pallas_call(
        kern,
        grid=(b * tiles_per_batch, n // tn),
        in_specs=[
            pl.BlockSpec((tm, d_ff), lambda i, j: (i, 0)),
            pl.BlockSpec((tm, d_ff), lambda i, j: (i, 1)),
            pl.BlockSpec((SUBLANE, d_ff), lambda i, j: (jnp.maximum(i * (tm // SUBLANE) - 1, 0), 0)),
            pl.BlockSpec((None, width - 1, d_ff), lambda i, j: (i // tiles_per_batch, 0, 0)),
            pl.BlockSpec((width, d_ff), lambda i, j: (0, 0)),
            pl.BlockSpec((1, d_ff), lambda i, j: (0, 0)),
            pl.BlockSpec((d_ff, tn), lambda i, j: (0, j)),
            pl.BlockSpec((tm, tn), lambda i, j: (i, j)),
        ],
        out_specs=pl.BlockSpec((tm, tn), lambda i, j: (i, j)),
        out_shape=jax.ShapeDtypeStruct((b * t, n), F32),
        scratch_shapes=[pltpu.VMEM((tm, d_ff), BF16), pltpu.VMEM((tm + SUBLANE, LANE), F32)],
        compiler_params=pltpu.CompilerParams(dimension_semantics=("parallel", "arbitrary"),
                                             vmem_limit_bytes=VMEM_LIMIT),
        name="ffn_down_fused",
    )(up2d, up2d, up2d, prefix, conv_w.T, conv_b.reshape(1, d_ff), w_down, res)


def conv_ffn(h2d, b, t, prefix, norm_w, w_up, conv_w, conv_b, w_down):
    d_ff = conv_w.shape[0]
    keep = conv_w.shape[1] - 1
    assert t >= keep
    up2d = proj(h2d, w_up, norm_w=norm_w, name="ffn_up")
    up = up2d.reshape(b, t, -1)
    new_prefix = up[:, t - keep:, :d_ff]
    if t % FFN_DOWN_ROWS == 0:
        return ffn_down_fused(up2d, b, t, prefix, conv_w, conv_b, w_down, h2d), new_prefix
    hidden = ffn_gate(up, prefix, conv_w, conv_b, d_ff)
    out = proj(hidden.reshape(b * t, d_ff), w_down, res=h2d, name="ffn_down")
    return out, new_prefix


def t5_bucket(dist):
    d = jnp.maximum(dist, 0)
    max_exact = N_BUCKETS // 2
    scale = (N_BUCKETS - max_exact) / math.log(REL_MAX_DIST / max_exact)
    large = max_exact + (jnp.log(jnp.maximum(d, 1).astype(F32) / max_exact) * scale).astype(jnp.int32)
    return jnp.where(d < max_exact, d, jnp.minimum(large, N_BUCKETS - 1))


GDN_CONV_COLS = 512
GDN_CONV_ROWS = 2048
GDN_HEADS_PER_STEP = 16
GDN_GROUP = 4


def _gdn_conv_kernel(x_ref, pre_ref, cw_ref, o_ref, pad_ref, *, width, n_q_blocks, n_qk_blocks):
    j = pl.program_id(1)
    t = x_ref.shape[1]
    halo = SUBLANE
    scale = jnp.where(j < n_q_blocks, GDN_DK ** -0.5, 1.0)
    is_qk = j < n_qk_blocks
    for bi in range(x_ref.shape[0]):
        pad_ref[halo - (width - 1):halo, :] = pre_ref[bi]
        pad_ref[halo:halo + t, :] = x_ref[bi]
        acc = pad_ref[halo - (width - 1):halo - (width - 1) + t, :] * cw_ref[0:1, :]
        for i in range(1, width):
            off = halo - (width - 1) + i
            acc = acc + pad_ref[off:off + t, :] * cw_ref[i:i + 1, :]
        y = jax.nn.silu(acc)
        for h in range(x_ref.shape[2] // LANE):
            seg = y[:, h * LANE:(h + 1) * LANE]
            nrm = seg * lax.rsqrt(jnp.sum(seg * seg, axis=-1, keepdims=True) + RMS_EPS) * scale
            o_ref[bi, :, h * LANE:(h + 1) * LANE] = jnp.where(is_qk, nrm, seg)


def gdn_conv(proj3, prefix, conv_w):
    b, t, _ = proj3.shape
    width = conv_w.shape[1]
    cols = GDN_CONV_COLS
    bb = max(1, min(b, GDN_CONV_ROWS // t))
    assert b % bb == 0
    kern = functools.partial(_gdn_conv_kernel, width=width, n_q_blocks=GDN_KDIM // cols,
                             n_qk_blocks=2 * GDN_KDIM // cols)
    return pl.pallas_call(
        kern,
        grid=(b // bb, GDN_CONV_DIM // cols),
        in_specs=[
            pl.BlockSpec((bb, t, cols), lambda i, j: (i, 0, j)),
            pl.BlockSpec((bb, width - 1, cols), lambda i, j: (i, 0, j)),
            pl.BlockSpec((width, cols), lambda i, j: (0, j)),
        ],
        out_specs=pl.BlockSpec((bb, t, cols), lambda i, j: (i, 0, j)),
        out_shape=jax.ShapeDtypeStruct((b, t, GDN_CONV_DIM), F32),
        scratch_shapes=[pltpu.VMEM((t + SUBLANE, cols), F32)],
        compiler_params=pltpu.CompilerParams(dimension_semantics=("parallel", "parallel"),
                                             vmem_limit_bytes=VMEM_LIMIT),
        name="gdn_conv",
    )(proj3, prefix, conv_w.T)


def _split_bf16(x):
    hi = x.astype(BF16)
    return hi, (x - hi.astype(F32)).astype(BF16)


def _dot_split(a_parts, b_parts):
    (ah, al), (bh, bl) = a_parts, b_parts
    return (jnp.dot(ah, bh, preferred_element_type=F32) + jnp.dot(ah, bl, preferred_element_type=F32)
            + jnp.dot(al, bh, preferred_element_type=F32))


def _gdn_gate_kernel(x_ref, a_ref, dt_ref, o_ref, *, t_real):
    L = GDN_CHUNK
    lane = lax.broadcasted_iota(jnp.int32, (L, LANE), 1)
    row = lax.broadcasted_iota(jnp.int32, (L, LANE), 0)
    ci = lax.broadcasted_iota(jnp.int32, (L, L), 0)
    cj = lax.broadcasted_iota(jnp.int32, (L, L), 1)
    tril = jnp.where(ci >= cj, 1.0, 0.0).astype(BF16)
    neg_a = -jnp.exp(a_ref[...])

    def chunk(c, carry):
        r0 = pl.multiple_of(c * L, L)
        x = x_ref[pl.ds(r0, L), :]
        live = (row + r0) < t_real
        beta = jnp.where(live, jax.nn.sigmoid(x), 0.0)
        z = x + dt_ref[...]
        softplus = jnp.maximum(z, 0.0) + jnp.log(1.0 + jnp.exp(-jnp.abs(z)))
        g = jnp.where(live, neg_a * softplus, 0.0)
        hi, mid = _split_bf16(g)
        lo = (g - hi.astype(F32) - mid.astype(F32)).astype(BF16)
        cum = (jnp.dot(tril, hi, preferred_element_type=F32) + jnp.dot(tril, mid, preferred_element_type=F32)
               + jnp.dot(tril, lo, preferred_element_type=F32))
        o_ref[pl.ds(r0, L), :] = jnp.where(lane < GDN_V_HEADS, beta, cum)
        return carry

    lax.fori_loop(0, x_ref.shape[0] // L, chunk, 0)


def gdn_gates(gate3, a_log, dt_bias, t_real):
    b, t, _ = gate3.shape
    pad = jnp.zeros((GDN_V_HEADS,), F32)
    a_vec = jnp.concatenate([pad, a_log.astype(F32), pad, pad]).reshape(1, LANE)
    dt_vec = jnp.concatenate([pad, dt_bias.astype(F32), pad, pad]).reshape(1, LANE)
    return pl.pallas_call(
        functools.partial(_gdn_gate_kernel, t_real=t_real),
        grid=(b,),
        in_specs=[pl.BlockSpec((None, t, LANE), lambda i: (i, 0, 0)),
                  pl.BlockSpec((1, LANE), lambda i: (0, 0)),
                  pl.BlockSpec((1, LANE), lambda i: (0, 0))],
        out_specs=pl.BlockSpec((None, t, LANE), lambda i: (i, 0, 0)),
        out_shape=jax.ShapeDtypeStruct((b, t, LANE), F32),
        compiler_params=pltpu.CompilerParams(dimension_semantics=("parallel",)),
        name="gdn_gates",
    )(gate3, a_vec, dt_vec)


def _pad_rows(x, rows):
    if x.shape[0] == rows:
        return x
    return jnp.concatenate([x, jnp.zeros((rows - x.shape[0], x.shape[1]), x.dtype)], axis=0)


def _gdn_delta_kernel(q_ref, k_ref, v_ref, z_ref, gb_ref, gt_ref, s0_ref, nw_ref, o_ref, s_out_ref, s_ref, *, nc, tl):
    hg = pl.program_id(1)
    c = pl.program_id(2)
    L = GDN_CHUNK
    hb = GDN_HEADS_PER_STEP

    @pl.when(c == 0)
    def _():
        s_ref[...] = s0_ref[...]

    gs = GDN_GROUP
    n_groups = hb // gs
    rows = gs * L
    shift = L.bit_length() - 1
    gb = gb_ref[...]
    lane = lax.broadcasted_iota(jnp.int32, (L, LANE), 1)
    ri = lax.broadcasted_iota(jnp.int32, (rows, rows), 0)
    cj = lax.broadcasted_iota(jnp.int32, (rows, rows), 1)
    same = (ri >> shift) == (cj >> shift)
    strict = same & (ri > cj)
    incl = same & (ri >= cj)

    def stack(ref, width_of):
        return [jnp.concatenate([_pad_rows(ref[:, width_of(hh) * LANE:(width_of(hh) + 1) * LANE], L)
                                 for hh in range(g * gs, (g + 1) * gs)], axis=0) for g in range(n_groups)]

    q = stack(q_ref, lambda hh: hh // 2)
    k = stack(k_ref, lambda hh: hh // 2)
    v = stack(v_ref, lambda hh: hh)
    bcol, gcol, grow, eg, a_qk, pw, x = [], [], [], [], [], [], []
    for g in range(n_groups):
        heads = [hg * hb + g * gs + hh for hh in range(gs)]
        bcol.append(jnp.concatenate(
            [jnp.sum(jnp.where(lane == h, gb, 0.0), axis=-1, keepdims=True) for h in heads], axis=0))
        gcol.append(jnp.concatenate(
            [jnp.sum(jnp.where(lane == h + GDN_V_HEADS, gb, 0.0), axis=-1, keepdims=True) for h in heads], axis=0))
        grow.append(gt_ref[pl.ds(c, 1), g * rows:(g + 1) * rows])
        decay = jnp.exp(jnp.where(same, gcol[g] - grow[g], 0.0))
        kb = k[g].astype(BF16)
        kk = lax.dot_general(kb, kb, NT_DIMS, preferred_element_type=F32)
        qk = lax.dot_general(q[g].astype(BF16), kb, NT_DIMS, preferred_element_type=F32)
        a_qk.append((qk * jnp.where(incl, decay, 0.0)).astype(BF16))
        eg.append(jnp.exp(gcol[g]))
        pw.append(_split_bf16(-(bcol[g] * kk * jnp.where(strict, decay, 0.0))))
        x.append(jnp.concatenate([bcol[g] * v[g], (bcol[g] * eg[g]) * k[g]], axis=1))
    for i in range(shift):
        for g in range(n_groups):
            x[g] = x[g] + _dot_split(pw[g], _split_bf16(x[g]))
        if i + 1 < shift:
            for g in range(n_groups):
                pw[g] = _split_bf16(_dot_split(pw[g], pw[g]))
    for g in range(n_groups):
        u_eff, w_kb = x[g][:, :GDN_DV], x[g][:, GDN_DV:].astype(BF16)
        sbs = [s_ref[g * gs + hh].astype(BF16) for hh in range(gs)]
        u = jnp.concatenate(
            [u_eff[hh * L:(hh + 1) * L] - jnp.dot(w_kb[hh * L:(hh + 1) * L], sbs[hh], preferred_element_type=F32)
             for hh in range(gs)], axis=0)
        ub = u.astype(BF16)
        o_intra = jnp.dot(a_qk[g], ub, preferred_element_type=F32)
        q_dec = (q[g] * eg[g]).astype(BF16)
        for hh in range(gs):
            hr = slice(hh * L, (hh + 1) * L)
            hi = g * gs + hh
            o = jnp.dot(q_dec[hr], sbs[hh], preferred_element_type=F32) + o_intra[hr]
            g_last = grow[g][:, hh * L + L - 1:hh * L + L]
            k_dec = (k[g][hr] * jnp.exp(g_last - gcol[g][hr])).astype(BF16)
            s_ref[hi] = jnp.exp(g_last) * s_ref[hi] + lax.dot_general(k_dec, ub[hr], (((0,), (0,)), ((), ())),
                                                                      preferred_element_type=F32)
            on = o * lax.rsqrt(jnp.mean(o * o, axis=-1, keepdims=True) + RMS_EPS) * nw_ref[...]
            gated = on[:tl] * jax.nn.silu(z_ref[:, hi * LANE:(hi + 1) * LANE])
            o_ref[:, hi * LANE:(hi + 1) * LANE] = gated.astype(o_ref.dtype)

    @pl.when(c == nc - 1)
    def _():
        s_out_ref[...] = s_ref[...]


def gdn_delta(qkv, proj3, gb, gt, s0, norm_w, tl):
    b, t, _ = qkv.shape
    nc = gb.shape[1] // GDN_CHUNK
    hb = GDN_HEADS_PER_STEP
    kw = (hb // 2) * GDN_DK
    vw = hb * GDN_DV
    kern = functools.partial(_gdn_delta_kernel, nc=nc, tl=tl)
    return pl.pallas_call(
        kern,
        grid=(b, GDN_V_HEADS // hb, nc),
        in_specs=[
            pl.BlockSpec((None, tl, kw), lambda i, h, c: (i, c, h)),
            pl.BlockSpec((None, tl, kw), lambda i, h, c: (i, c, GDN_KDIM // kw + h)),
            pl.BlockSpec((None, tl, vw), lambda i, h, c: (i, c, 2 * GDN_KDIM // vw + h)),
            pl.BlockSpec((None, tl, vw), lambda i, h, c: (i, c, GDN_CONV_DIM // vw + h)),
            pl.BlockSpec((None, GDN_CHUNK, LANE), lambda i, h, c: (i, c, 0)),
            pl.BlockSpec((None, None, nc, hb * GDN_CHUNK), lambda i, h, c: (i, h, 0, 0)),
            pl.BlockSpec((None, hb, GDN_DK, GDN_DV), lambda i, h, c: (i, h, 0, 0)),
            pl.BlockSpec((1, GDN_DV), lambda i, h, c: (0, 0)),
        ],
        out_specs=[
            pl.BlockSpec((None, tl, vw), lambda i, h, c: (i, c, h)),
            pl.BlockSpec((None, hb, GDN_DK, GDN_DV), lambda i, h, c: (i, h, 0, 0)),
        ],
        out_shape=[jax.ShapeDtypeStruct((b, t, GDN_VDIM), BF16),
                   jax.ShapeDtypeStruct(s0.shape, F32)],
        scratch_shapes=[pltpu.VMEM((hb, GDN_DK, GDN_DV), F32)],
        compiler_params=pltpu.CompilerParams(dimension_semantics=("parallel", "parallel", "arbitrary"),
                                             vmem_limit_bytes=VMEM_LIMIT),
        name="gdn_delta",
    )(qkv, qkv, qkv, proj3, gb, gt, s0, norm_w.reshape(1, GDN_DV).astype(F32))


def gdn_mixer(h2d, b, t, s0, conv_prefix, norm_in, w_in, conv_w, a_log, dt_bias, norm_w, w_out):
    w_main, w_gate = w_in
    proj3 = proj(h2d, w_main, norm_w=norm_in, n_cols=GDN_CONV_DIM + GDN_VDIM, name="gdn_in").reshape(b, t, -1)
    gate3 = proj(h2d, w_gate, norm_w=norm_in, name="gdn_in_gate").reshape(b, t, -1)
    keep = conv_w.shape[1] - 1
    assert t >= keep
    new_prefix = proj3[:, t - keep:, :GDN_CONV_DIM]
    qkv = gdn_conv(proj3, conv_prefix, conv_w)
    tl = min(t, GDN_CHUNK)
    t_pad = -(-t // GDN_CHUNK) * GDN_CHUNK
    gb = gdn_gates(jnp.pad(gate3, ((0, 0), (0, t_pad - t), (0, 0))), a_log, dt_bias, t)
    nc = t_pad // GDN_CHUNK
    hb = GDN_HEADS_PER_STEP
    gt = gb[:, :, GDN_V_HEADS:2 * GDN_V_HEADS].reshape(b, nc, GDN_CHUNK, GDN_V_HEADS // hb, hb)
    gt = gt.transpose(0, 3, 1, 4, 2).reshape(b, GDN_V_HEADS // hb, nc, hb * GDN_CHUNK)
    o, s_new = gdn_delta(qkv, proj3, gb, gt, s0.astype(F32), norm_w, tl)
    y = proj(o.reshape(b * t, GDN_VDIM), w_out, res=h2d, name="gdn_out")
    return y, s_new, new_prefix


def _compress_mlp(get_x, w1_at, pe, w2, k_norm, acc_ref, s_all):
    acc_ref[0:s_all, :] = jnp.zeros((s_all, 2 * CMP_HIDDEN), F32)

    def body(r, carry):
        acc_ref[0:s_all, :] += jnp.dot(get_x(r), w1_at(r), preferred_element_type=F32)
        return carry

    lax.fori_loop(0, CMP_STRIDE, body, 0)
    hid = acc_ref[0:s_all, 0:CMP_HIDDEN] + acc_ref[1:s_all + 1, CMP_HIDDEN:2 * CMP_HIDDEN] + pe
    y = jnp.dot(jax.nn.gelu(hid).astype(BF16), w2, preferred_element_type=F32)
    if k_norm is None:
        return y
    return y * lax.rsqrt(jnp.mean(y * y, axis=-1, keepdims=True) + RMS_EPS) * k_norm


def _compress_kernel(*refs, s_all):
    x_refs = refs[:NSA_KV_HEADS]
    w1_ref, pe_ref, w2_ref, nw_ref, o_ref, xs_ref, acc_ref = refs[NSA_KV_HEADS:]
    for r in range(CMP_STRIDE):
        for g in range(NSA_KV_HEADS):
            xs_ref[r, g] = x_refs[g][pl.ds(r, s_all, stride=CMP_STRIDE), :].astype(BF16)
    is_k = pl.program_id(1) == 0
    acc_ref[s_all:s_all + SUBLANE, :] = jnp.zeros((SUBLANE, 2 * CMP_HIDDEN), F32)
    for g in range(NSA_KV_HEADS):
        y = _compress_mlp(lambda r: xs_ref[r, g], lambda r: w1_ref[r], pe_ref[...], w2_ref[...], None, acc_ref,
                          s_all)
        normed = y * lax.rsqrt(jnp.mean(y * y, axis=-1, keepdims=True) + RMS_EPS) * nw_ref[...]
        o_ref[g] = jnp.where(is_k, normed, y)


def nsa_compress(x, col0, w1r, pe_hid, w2, k_norm0):
    b, rows, _ = x.shape
    s_all = rows // CMP_STRIDE

    def x_spec(g):
        return pl.BlockSpec((None, rows, NSA_HD), lambda i, c: (i, 0, (col0 + c) * NSA_KV_HEADS + g))

    in_specs = [x_spec(g) for g in range(NSA_KV_HEADS)] + [
        pl.BlockSpec((None, CMP_STRIDE, NSA_HD, 2 * CMP_HIDDEN), lambda i, c: (c, 0, 0, 0)),
        pl.BlockSpec((None, 1, CMP_HIDDEN), lambda i, c: (c, 0, 0)),
        pl.BlockSpec((None, CMP_HIDDEN, NSA_HD), lambda i, c: (c, 0, 0)),
        pl.BlockSpec((1, NSA_HD), lambda i, c: (0, 0)),
    ]
    return pl.pallas_call(
        functools.partial(_compress_kernel, s_all=s_all),
        grid=(b, 2),
        in_specs=in_specs,
        out_specs=pl.BlockSpec((None, None, NSA_KV_HEADS, s_all, NSA_HD), lambda i, c: (i, c, 0, 0, 0)),
        out_shape=jax.ShapeDtypeStruct((b, 2, NSA_KV_HEADS, s_all, NSA_HD), F32),
        scratch_shapes=[pltpu.VMEM((CMP_STRIDE, NSA_KV_HEADS, s_all, NSA_HD), BF16),
                        pltpu.VMEM((s_all + SUBLANE, 2 * CMP_HIDDEN), F32)],
        compiler_params=pltpu.CompilerParams(dimension_semantics=("parallel", "parallel"),
                                             vmem_limit_bytes=VMEM_LIMIT),
        name="nsa_compress",
    )(*((x,) * NSA_KV_HEADS), w1r, pe_hid, w2, k_norm0.reshape(1, NSA_HD).astype(F32))


CMP_FINAL_ROWS = 512


def _compress_paged_kernel(*refs, n_steps, pps):
    x_refs = refs[1:1 + pps]
    w1_ref, pe_ref, w2_ref, nw_ref, o_ref, acc_ref = refs[1 + pps:]
    p = pl.program_id(1)
    planes = 2 * NSA_KV_HEADS
    sp = PAGE_SIZE // CMP_STRIDE
    rows_pp = sp * planes
    n_rows = n_steps * pps * rows_pp
    hid_w = CMP_HIDDEN
    lhs = jnp.concatenate(
        [jnp.concatenate([x[pl.ds(r, sp, stride=CMP_STRIDE), :, :].reshape(rows_pp, NSA_HD).astype(BF16)
                          for r in range(CMP_STRIDE)], axis=1) for x in x_refs], axis=0)
    row0 = pl.multiple_of(p * (pps * rows_pp), pps * rows_pp)
    acc_ref[pl.ds(row0, pps * rows_pp), :] = jnp.dot(lhs, w1_ref[...], preferred_element_type=F32)

    @pl.when(p == n_steps - 1)
    def _():
        acc_ref[n_rows:n_rows + planes, :] = jnp.zeros((planes, 4 * hid_w), F32)
        cr = CMP_FINAL_ROWS
        is_k = (lax.broadcasted_iota(jnp.int32, (cr, 1), 0) % planes) < NSA_KV_HEADS

        def chunk(ci, carry):
            r0 = pl.multiple_of(ci * cr, cr)
            a = acc_ref[pl.ds(r0, cr), :]
            nxt = acc_ref[pl.ds(r0 + planes, cr), :]
            hid_k = a[:, 0:hid_w] + nxt[:, hid_w:2 * hid_w] + pe_ref[0]
            hid_v = a[:, 2 * hid_w:3 * hid_w] + nxt[:, 3 * hid_w:4 * hid_w] + pe_ref[1]
            act = jax.nn.gelu(jnp.where(is_k, hid_k, hid_v)).astype(BF16)
            y2 = jnp.dot(act, w2_ref[...], preferred_element_type=F32)
            y = jnp.where(is_k, y2[:, :NSA_HD], y2[:, NSA_HD:])
            normed = y * lax.rsqrt(jnp.mean(y * y, axis=-1, keepdims=True) + RMS_EPS) * nw_ref[...]
            o_ref[pl.ds(r0, cr), :] = jnp.where(is_k, normed, y)
            return carry

        lax.fori_loop(0, n_rows // cr, chunk, 0)


def nsa_compress_paged(pool4, page_table, w1r, pe_hid, w2, k_norm0):
    b, n_pages = page_table.shape
    pps = PAGES_PER_STEP
    assert n_pages % pps == 0
    n_steps = n_pages // pps
    planes = 2 * NSA_KV_HEADS
    s_all = n_pages * (PAGE_SIZE // CMP_STRIDE)
    n_rows = s_all * planes
    assert n_rows % CMP_FINAL_ROWS == 0
    w1cat = w1r.transpose(1, 2, 0, 3).reshape(CMP_STRIDE * NSA_HD, 4 * CMP_HIDDEN)
    w2cat = jnp.concatenate([w2[0], w2[1]], axis=1)

    def page_spec(u):
        return pl.BlockSpec((None, PAGE_SIZE, planes, NSA_HD), lambda i, p, pt: (pt[i, p * pps + u], 0, 0, 0))

    in_specs = [page_spec(u) for u in range(pps)] + [
        pl.BlockSpec(w1cat.shape, lambda i, p, pt: (0, 0)),
        pl.BlockSpec((2, 1, CMP_HIDDEN), lambda i, p, pt: (0, 0, 0)),
        pl.BlockSpec(w2cat.shape, lambda i, p, pt: (0, 0)),
        pl.BlockSpec((1, NSA_HD), lambda i, p, pt: (0, 0)),
    ]
    grid_spec = pltpu.PrefetchScalarGridSpec(
        num_scalar_prefetch=1, grid=(b, n_steps), in_specs=in_specs,
        out_specs=pl.BlockSpec((None, n_rows, NSA_HD), lambda i, p, pt: (i, 0, 0)),
        scratch_shapes=[pltpu.VMEM((n_rows + planes, 4 * CMP_HIDDEN), F32)])
    out = pl.pallas_call(
        functools.partial(_compress_paged_kernel, n_steps=n_steps, pps=pps),
        grid_spec=grid_spec,
        out_shape=jax.ShapeDtypeStruct((b, n_rows, NSA_HD), F32),
        compiler_params=pltpu.CompilerParams(dimension_semantics=("parallel", "arbitrary"),
                                             vmem_limit_bytes=VMEM_LIMIT),
        name="nsa_compress_paged",
    )(page_table, *((pool4,) * pps), w1cat, pe_hid, w2cat, k_norm0.reshape(1, NSA_HD).astype(F32))
    return out.reshape(b, s_all, 2, NSA_KV_HEADS, NSA_HD).transpose(0, 2, 3, 1, 4)


def _stack_heads(qb):
    return jnp.concatenate([qb[:, j * LANE:(j + 1) * LANE] for j in range(NSA_GROUP)], axis=0)


def _cmp_attn_kernel(q_ref, kc_ref, vc_ref, cb_ref, gate_ref, ovl_ref, o_ref, sel_ref, *, tq, q0, n_sb, gpb):
    i = pl.program_id(2)
    s_all = kc_ref.shape[1]
    nsbp = sel_ref.shape[-1]
    width = NSA_GROUP * LANE
    ovl = ovl_ref[...]
    imps = []
    for gg in range(gpb):
        q4 = _stack_heads(q_ref[:, gg * width:(gg + 1) * width]).astype(BF16)
        s = lax.dot_general(q4, kc_ref[gg].astype(BF16), NT_DIMS, preferred_element_type=F32)
        bias = cb_ref[gg * NSA_GROUP:(gg + 1) * NSA_GROUP].reshape(NSA_GROUP * tq, s_all)
        s = s + bias
        ok = bias > 0.5 * NEG_INF
        m = jnp.max(s, axis=-1, keepdims=True)
        e = jnp.exp(s - m)
        p = jnp.where(ok, e / jnp.sum(e, axis=-1, keepdims=True), 0.0)
        o4 = jnp.dot(p.astype(BF16), vc_ref[gg].astype(BF16), preferred_element_type=F32)
        for j in range(NSA_GROUP):
            col = gg * width + j * LANE
            o_ref[:, col:col + LANE] = o4[j * tq:(j + 1) * tq] * gate_ref[gg, :, j:j + 1]
        psum = p[0:tq] + p[tq:2 * tq] + p[2 * tq:3 * tq] + p[3 * tq:4 * tq]
        p_hi = psum.astype(BF16)
        r1 = psum - p_hi.astype(F32)
        p_mid = r1.astype(BF16)
        p_lo = (r1 - p_mid.astype(F32)).astype(BF16)
        imps.append(jnp.dot(p_hi, ovl, preferred_element_type=F32) + jnp.dot(p_mid, ovl, preferred_element_type=F32)
                    + jnp.dot(p_lo, ovl, preferred_element_type=F32))
    imp = jnp.concatenate(imps, axis=0)

    lane1 = lax.broadcasted_iota(jnp.int32, (tq, nsbp), 1)
    qpos1 = q0 + i * tq + lax.broadcasted_iota(jnp.int32, (tq, nsbp), 0)
    lane = jnp.concatenate([lane1] * gpb, axis=0)
    qpos = jnp.concatenate([qpos1] * gpb, axis=0)
    cur = qpos >> SEL_SHIFT
    forced = (lane == 0) | (lane == cur) | (lane == cur - 1)
    sb_ok = (lane << SEL_SHIFT) <= qpos
    score = jnp.where(sb_ok, imp + jnp.where(forced, FORCE_BONUS, 0.0), NEG_INF)
    work = jnp.where(lane < n_sb, score, -jnp.inf)
    lane_f = lane.astype(F32)
    selneg = jnp.full((gpb * tq, nsbp), NEG_INF, F32)
    for _ in range(N_SEL):
        mx = jnp.max(work, axis=-1, keepdims=True)
        first = jnp.min(jnp.where(work == mx, lane_f, 1e9), axis=-1, keepdims=True)
        hit = lane_f == first
        selneg = jnp.where(hit & (mx > 0.5 * NEG_INF), 0.0, selneg)
        work = jnp.where(hit, -jnp.inf, work)
    for gg in range(gpb):
        sel_ref[gg] = selneg[gg * tq:(gg + 1) * tq]


def nsa_cmp_attention(proj3, kvc, cmp_bias, gates_r, ovl, tq, q0, n_sb):
    b, t, _ = proj3.shape
    s_all = kvc.shape[3]
    nsbp = ovl.shape[1]
    gpb = NSA_KV_HEADS if tq * NSA_KV_HEADS <= CMP_TOPK_ROWS else 1
    kern = functools.partial(_cmp_attn_kernel, tq=tq, q0=q0, n_sb=n_sb, gpb=gpb)
    return pl.pallas_call(
        kern,
        grid=(b, NSA_KV_HEADS // gpb, t // tq),
        in_specs=[
            pl.BlockSpec((None, tq, gpb * NSA_GROUP * NSA_HD), lambda i, g, q: (i, q, g)),
            pl.BlockSpec((None, None, gpb, s_all, NSA_HD), lambda i, g, q: (i, 0, g, 0, 0)),
            pl.BlockSpec((None, None, gpb, s_all, NSA_HD), lambda i, g, q: (i, 1, g, 0, 0)),
            pl.BlockSpec((gpb * NSA_GROUP, tq, s_all), lambda i, g, q: (g, q, 0)),
            pl.BlockSpec((None, gpb, tq, 3 * NSA_GROUP), lambda i, g, q: (i, g, q, 0)),
            pl.BlockSpec((s_all, nsbp), lambda i, g, q: (0, 0)),
        ],
        out_specs=[
            pl.BlockSpec((None, tq, gpb * NSA_GROUP * NSA_HD), lambda i, g, q: (i, q, g)),
            pl.BlockSpec((None, gpb, tq, nsbp), lambda i, g, q: (i, g, q, 0)),
        ],
        out_shape=[jax.ShapeDtypeStruct((b, t, NSA_QDIM), F32),
                   jax.ShapeDtypeStruct((b, NSA_KV_HEADS, t, nsbp), F32)],
        compiler_params=pltpu.CompilerParams(dimension_semantics=("parallel", "parallel", "parallel"),
                                             vmem_limit_bytes=VMEM_LIMIT),
        name="nsa_cmp_attn",
    )(proj3, kvc, kvc, cmp_bias, gates_r, ovl)


def _softmax_tile_update(s, v, m_prev, l_prev, acc_prev):
    m_new = jnp.maximum(m_prev, jnp.max(s, axis=-1, keepdims=True))
    alpha = jnp.exp(m_prev - m_new)
    p = jnp.exp(s - m_new)
    l_new = alpha * l_prev + jnp.sum(p, axis=-1, keepdims=True)
    acc_new = alpha * acc_prev + jnp.dot(p.astype(BF16), v, preferred_element_type=F32)
    return m_new, l_new, acc_new


def _tile_scores(q4, k, bias3, sel, tq, tk, qpos0, kpos0, window):
    s = lax.dot_general(q4, k, NT_DIMS, preferred_element_type=F32).reshape(NSA_GROUP, tq, tk) + bias3
    kpos = kpos0 + lax.broadcasted_iota(jnp.int32, (tq, tk), 1)
    qpos = qpos0 + lax.broadcasted_iota(jnp.int32, (tq, tk), 0)
    dist = qpos - kpos
    mask = dist >= 0
    if window is not None:
        mask = mask & (dist < window)
    if sel is not None:
        nsbp = sel.shape[1]
        blk = (kpos0 + lax.broadcasted_iota(jnp.int32, (nsbp, tk), 1)) >> SEL_SHIFT
        onehot = jnp.where(blk == lax.broadcasted_iota(jnp.int32, (nsbp, tk), 0), 1.0, 0.0).astype(BF16)
        shared = jnp.where(mask, jnp.dot(sel.astype(BF16), onehot, preferred_element_type=F32), NEG_INF)
    else:
        shared = jnp.where(mask, 0.0, NEG_INF)
    return (s + shared[None]).reshape(NSA_GROUP * tq, tk)


def _flash_prompt_kernel(*refs, tq, tk, nkk, use_sel, window, gcol):
    if use_sel:
        q_ref, k_ref, v_ref, b_ref, gate_ref, sel_ref, o_ref, q4_ref, m_ref, l_ref, acc_ref = refs
    else:
        q_ref, k_ref, v_ref, b_ref, gate_ref, o_ref, q4_ref, m_ref, l_ref, acc_ref = refs
        sel_ref = None
    i = pl.program_id(2)
    jj = pl.program_id(3)
    if use_sel:
        j, valid = jj, jj <= i
    else:
        j = i - (nkk - 1) + jj
        valid = j >= 0

    @pl.when(jj == 0)
    def _():
        q4_ref[...] = _stack_heads(q_ref[...]).astype(BF16)
        m_ref[...] = jnp.full(m_ref.shape, NEG_INF, F32)
        l_ref[...] = jnp.zeros(l_ref.shape, F32)
        acc_ref[...] = jnp.zeros(acc_ref.shape, F32)

    @pl.when(valid)
    def _():
        sel = sel_ref[...] if use_sel else None
        s = _tile_scores(q4_ref[...], k_ref[...].astype(BF16), b_ref[...], sel, tq, tk, i * tq, j * tk, window)
        m_new, l_new, acc_new = _softmax_tile_update(s, v_ref[...].astype(BF16), m_ref[...], l_ref[...],
                                                     acc_ref[...])
        m_ref[...] = m_new
        l_ref[...] = l_new
        acc_ref[...] = acc_new

    @pl.when(jj == nkk - 1)
    def _():
        o4 = acc_ref[...] / l_ref[...]
        for h in range(NSA_GROUP):
            o_ref[:, h * LANE:(h + 1) * LANE] = o4[h * tq:(h + 1) * tq] * gate_ref[:, gcol + h:gcol + h + 1]


def nsa_flash_prompt(proj3, bias_tab, gates_r, sel, *, tile, kcol, vcol, window, gcol, name):
    b, t, _ = proj3.shape
    nq = t // tile
    n_dd = bias_tab.shape[1]
    use_sel = sel is not None
    nkk = nq if use_sel else (window + tile - 1) // tile + 1

    def kidx(q, jj):
        return jnp.minimum(jj, q) if use_sel else jnp.maximum(q - (nkk - 1) + jj, 0)

    in_specs = [
        pl.BlockSpec((None, tile, NSA_GROUP * NSA_HD), lambda i, g, q, jj: (i, q, g)),
        pl.BlockSpec((None, tile, NSA_HD), lambda i, g, q, jj: (i, kidx(q, jj), kcol + g)),
        pl.BlockSpec((None, tile, NSA_HD), lambda i, g, q, jj: (i, kidx(q, jj), vcol + g)),
        pl.BlockSpec((NSA_GROUP, None, tile, tile),
                     lambda i, g, q, jj: (g, jnp.minimum(q - kidx(q, jj), n_dd - 1), 0, 0)),
        pl.BlockSpec((None, None, tile, 3 * NSA_GROUP), lambda i, g, q, jj: (i, g, q, 0)),
    ]
    args = [proj3, proj3, proj3, bias_tab, gates_r]
    if use_sel:
        nsbp = sel.shape[-1]
        in_specs.append(pl.BlockSpec((None, None, tile, nsbp), lambda i, g, q, jj: (i, g, q, 0)))
        args.append(sel)
    kern = functools.partial(_flash_prompt_kernel, tq=tile, tk=tile, nkk=nkk, use_sel=use_sel, window=window,
                             gcol=gcol)
    return pl.pallas_call(
        kern,
        grid=(b, NSA_KV_HEADS, nq, nkk),
        in_specs=in_specs,
        out_specs=pl.BlockSpec((None, tile, NSA_GROUP * NSA_HD), lambda i, g, q, jj: (i, q, g)),
        out_shape=jax.ShapeDtypeStruct((b, t, NSA_QDIM), F32),
        scratch_shapes=[pltpu.VMEM((NSA_GROUP * tile, NSA_HD), BF16),
                        pltpu.VMEM((NSA_GROUP * tile, 1), F32),
                        pltpu.VMEM((NSA_GROUP * tile, 1), F32),
                        pltpu.VMEM((NSA_GROUP * tile, NSA_HD), F32)],
        compiler_params=pltpu.CompilerParams(
            dimension_semantics=("parallel", "parallel", "parallel", "arbitrary"), vmem_limit_bytes=VMEM_LIMIT),
        name=name,
    )(*args)


def _flash_sample_kernel(*refs, tq, pps, n_steps, plane0, qtile, kbase, q0, use_sel, window, gcol):
    pt_ref, q_ref, pool_ref, kn_ref, vn_ref, b_ref, gate_ref = refs[:7]
    if use_sel:
        sel_ref, o_ref, kv_ref, sem, q4_ref, m_ref, l_ref, acc_ref = refs[7:]
    else:
        o_ref, kv_ref, sem, q4_ref, m_ref, l_ref, acc_ref = refs[7:]
        sel_ref = None
    i = pl.program_id(0)
    planes = 2 * NSA_KV_HEADS
    page = kv_ref.shape[3]
    tk = pps * page
    n_dd = b_ref.shape[1]

    def page_copies(step, slot):
        return [pltpu.make_async_copy(pool_ref.at[pt_ref[i, step * pps + u], :, plane0 + j, :],
                                      kv_ref.at[slot, u, j], sem.at[slot])
                for u in range(pps) for j in range(planes)]

    for cp in page_copies(0, 0):
        cp.start()
    for g in range(NSA_KV_HEADS):
        q4_ref[g] = _stack_heads(q_ref[:, g * NSA_GROUP * LANE:(g + 1) * NSA_GROUP * LANE]).astype(BF16)
    m_ref[...] = jnp.full(m_ref.shape, NEG_INF, F32)
    l_ref[...] = jnp.zeros(l_ref.shape, F32)
    acc_ref[...] = jnp.zeros(acc_ref.shape, F32)

    def update(g, k, v, bias3, kpos0):
        sel = sel_ref[g] if use_sel else None
        s = _tile_scores(q4_ref[g], k, bias3, sel, tq, k.shape[0], q0, kpos0, window)
        m_new, l_new, acc_new = _softmax_tile_update(s, v, m_ref[g], l_ref[g], acc_ref[g])
        m_ref[g] = m_new
        l_ref[g] = l_new
        acc_ref[g] = acc_new

    def step_body(step, carry):
        slot = step % 2

        @pl.when(step + 1 < n_steps)
        def _():
            for cp in page_copies(step + 1, 1 - slot):
                cp.start()

        for cp in page_copies(step, slot):
            cp.wait()
        dd = jnp.clip(qtile - step, 0, n_dd - 1)
        scores = []
        for g in range(NSA_KV_HEADS):
            k = jnp.concatenate([kv_ref[slot, u, g].astype(BF16) for u in range(pps)], axis=0)
            sel = sel_ref[g] if use_sel else None
            scores.append(_tile_scores(q4_ref[g], k, b_ref[g * NSA_GROUP:(g + 1) * NSA_GROUP, dd], sel, tq, tk, q0,
                                       kbase + step * tk, window))
        for g in range(NSA_KV_HEADS):
            v = jnp.concatenate([kv_ref[slot, u, NSA_KV_HEADS + g].astype(BF16) for u in range(pps)], axis=0)
            m_new, l_new, acc_new = _softmax_tile_update(scores[g], v, m_ref[g], l_ref[g], acc_ref[g])
            m_ref[g] = m_new
            l_ref[g] = l_new
            acc_ref[g] = acc_new
        return carry

    lax.fori_loop(0, n_steps, step_body, 0)

    pad = jnp.zeros((page - tq, LANE), F32)
    for g in range(NSA_KV_HEADS):
        cols = slice(g * LANE, (g + 1) * LANE)
        kn = jnp.concatenate([kn_ref[:, cols], pad], axis=0).astype(BF16)
        vn = jnp.concatenate([vn_ref[:, cols], pad], axis=0).astype(BF16)
        update(g, kn, vn, b_ref[g * NSA_GROUP:(g + 1) * NSA_GROUP, 0, :, 0:page], q0)
        o4 = acc_ref[g] / l_ref[g]
        for j in range(NSA_GROUP):
            h = g * NSA_GROUP + j
            o_ref[:, h * LANE:(h + 1) * LANE] = o4[j * tq:(j + 1) * tq] * gate_ref[g, :, gcol + j:gcol + j + 1]


def nsa_flash_sample(proj3, pool4, table, bias_tab, gates_r, sel, *, plane0, newk, newv, kbase, q0, window,
                     gcol, name):
    b, tq, _ = proj3.shape
    page = pool4.shape[1]
    tk = bias_tab.shape[3]
    pps = tk // page
    n_steps = table.shape[1] // pps
    qtile = (q0 - kbase) // tk
    use_sel = sel is not None
    in_specs = [
        pl.BlockSpec((None, tq, NSA_QDIM), lambda i, pt: (i, 0, 0)),
        pl.BlockSpec(memory_space=pl.ANY),
        pl.BlockSpec((None, tq, NSA_KVDIM), lambda i, pt: (i, 0, newk)),
        pl.BlockSpec((None, tq, NSA_KVDIM), lambda i, pt: (i, 0, newv)),
        pl.BlockSpec(bias_tab.shape, lambda i, pt: (0, 0, 0, 0)),
        pl.BlockSpec((None, NSA_KV_HEADS, tq, 3 * NSA_GROUP), lambda i, pt: (i, 0, 0, 0)),
    ]
    args = [proj3, pool4, proj3, proj3, bias_tab, gates_r]
    if use_sel:
        nsbp = sel.shape[-1]
        in_specs.append(pl.BlockSpec((None, NSA_KV_HEADS, tq, nsbp), lambda i, pt: (i, 0, 0, 0)))
        args.append(sel)
    kern = functools.partial(_flash_sample_kernel, tq=tq, pps=pps, n_steps=n_steps, plane0=plane0, qtile=qtile,
                             kbase=kbase, q0=q0, use_sel=use_sel, window=window, gcol=gcol)
    rows4 = NSA_GROUP * tq
    grid_spec = pltpu.PrefetchScalarGridSpec(
        num_scalar_prefetch=1, grid=(b,), in_specs=in_specs,
        out_specs=pl.BlockSpec((None, tq, NSA_QDIM), lambda i, pt: (i, 0, 0)),
        scratch_shapes=[pltpu.VMEM((2, pps, 2 * NSA_KV_HEADS, page, NSA_HD), F32),
                        pltpu.SemaphoreType.DMA((2,)),
                        pltpu.VMEM((NSA_KV_HEADS, rows4, NSA_HD), BF16),
                        pltpu.VMEM((NSA_KV_HEADS, rows4, 1), F32),
                        pltpu.VMEM((NSA_KV_HEADS, rows4, 1), F32),
                        pltpu.VMEM((NSA_KV_HEADS, rows4, NSA_HD), F32)])
    return pl.pallas_call(
        kern, grid_spec=grid_spec, out_shape=jax.ShapeDtypeStruct((b, tq, NSA_QDIM), F32),
        compiler_params=pltpu.CompilerParams(dimension_semantics=("arbitrary",), vmem_limit_bytes=VMEM_LIMIT),
        name=name,
    )(table, *args)


def _bias_table(rel_bias, unit, tq, tk):
    n_dd = -(-(T5_SATURATION + tk - 1) // unit) + 1
    dist = (jnp.arange(n_dd)[:, None, None] * unit + jnp.arange(tq)[None, :, None]
            - jnp.arange(tk)[None, None, :])
    return _bucket_lookup(rel_bias, dist)


def _bucket_lookup(rel_bias, dist):
    onehot = jax.nn.one_hot(t5_bucket(dist), N_BUCKETS, dtype=F32)
    out = jnp.einsum('...b,bh->h...', onehot, rel_bias.astype(F32), precision=lax.Precision.HIGHEST)
    return out


def _cmp_tables(rel_bias, q0, t, s_all, nc, n_sb, nsbp):
    c = jnp.arange(s_all)
    c_end = c * CMP_STRIDE + (CMP_BLOCK - 1)
    dist = (q0 + jnp.arange(t))[:, None] - c_end[None, :]
    ok = (dist >= 0) & (c < nc)[None, :]
    bias = jnp.where(ok[None], _bucket_lookup(rel_bias, dist), NEG_INF)
    sb_start = jnp.arange(nsbp) * SEL_BLOCK
    c_start = c * CMP_STRIDE
    ovl = jnp.maximum(jnp.minimum(c_end[:, None], sb_start[None, :] + SEL_BLOCK - 1)
                      - jnp.maximum(c_start[:, None], sb_start[None, :]) + 1, 0).astype(F32) / CMP_BLOCK
    ovl = jnp.where((c < nc)[:, None] & (jnp.arange(nsbp) < n_sb)[None, :], ovl, 0.0)
    return bias, ovl.astype(BF16)


def _nsa_weights(w_in, q_norm, k_norm, cmp_pe, cmp_w1, cmp_w2, rel_bias, w_out):
    n_main = NSA_QDIM + 6 * NSA_KVDIM
    nsub = CMP_BLOCK // CMP_STRIDE
    w1r = cmp_w1.reshape(2, nsub, CMP_STRIDE, NSA_HD, CMP_HIDDEN)
    w1r = jnp.moveaxis(w1r, 1, 3).reshape(2, CMP_STRIDE, NSA_HD, nsub * CMP_HIDDEN).astype(BF16)
    pe_hid = jnp.einsum('ck,cke->ce', cmp_pe.reshape(2, -1), cmp_w1, precision=lax.Precision.HIGHEST)
    return dict(
        w_main=w_in,
        w_gate=_pad_cols(w_in[:, n_main:], LANE).astype(BF16),
        post=_nsa_in_post(q_norm, k_norm),
        w1r=w1r, pe_hid=pe_hid.reshape(2, 1, CMP_HIDDEN), w2=cmp_w2.astype(BF16), k_norm0=k_norm[0],
        rel_bias=rel_bias, w_out=w_out)


def nsa_mixer(h2d, b, t, norm_in, nw, paged):
    post_main, post_gate = nw["post"]
    proj2 = proj(h2d, nw["w_main"], norm_w=norm_in, post=post_main, n_cols=NSA_QDIM + 6 * NSA_KVDIM, name="nsa_in")
    gate_out = proj(h2d, nw["w_gate"], norm_w=norm_in, post=post_gate, name="nsa_in_gate")
    proj3 = proj2.reshape(b, t, -1)
    gates_r = gate_out[:, :3 * NSA_HEADS].reshape(b, t, 3, NSA_KV_HEADS, NSA_GROUP)
    gates_r = gates_r.transpose(0, 3, 1, 2, 4).reshape(b, NSA_KV_HEADS, t, 3 * NSA_GROUP)
    kv_blk0 = NSA_QDIM // NSA_KVDIM
    kv_col0 = NSA_QDIM // NSA_HD
    rel_bias = nw["rel_bias"]
    if paged is None:
        p_len, tq = 0, 256
        kvc = nsa_compress(proj3, kv_blk0, nw["w1r"], nw["pe_hid"], nw["w2"], nw["k_norm0"])
    else:
        pool, page_table, win_pool = paged
        p_len, tq = page_table.shape[1] * PAGE_SIZE, t
        kvc = nsa_compress_paged(pool, page_table, nw["w1r"], nw["pe_hid"], nw["w2"], nw["k_norm0"])
    n_all = p_len + t
    s_all = kvc.shape[3]
    nc = n_all // CMP_STRIDE - CMP_BLOCK // CMP_STRIDE + 1
    n_sb = -(-n_all // SEL_BLOCK)
    nsbp = -(-n_sb // LANE) * LANE
    cmp_bias, ovl = _cmp_tables(rel_bias, p_len, t, s_all, nc, n_sb, nsbp)
    o_cmp, sel = nsa_cmp_attention(proj3, kvc, cmp_bias, gates_r, ovl, tq, p_len, n_sb)
    if paged is None:
        tab = _bias_table(rel_bias, tq, tq, tq)
        o_sel = nsa_flash_prompt(proj3, tab, gates_r, sel, tile=tq, kcol=kv_col0 + 2 * NSA_KV_HEADS,
                                 vcol=kv_col0 + 3 * NSA_KV_HEADS, window=None, gcol=NSA_GROUP, name="nsa_sel")
        o_win = nsa_flash_prompt(proj3, tab, gates_r, None, tile=tq, kcol=kv_col0 + 4 * NSA_KV_HEADS,
                                 vcol=kv_col0 + 5 * NSA_KV_HEADS, window=WINDOW, gcol=2 * NSA_GROUP, name="nsa_win")
    else:
        tk_sel = SEL_PAGES_PER_STEP * PAGE_SIZE
        assert p_len % tk_sel == 0
        tab_sel = _bias_table(rel_bias, tk_sel, t, tk_sel)
        tab = _bias_table(rel_bias, WINDOW, t, WINDOW)
        o_sel = nsa_flash_sample(proj3, pool, page_table, tab_sel, gates_r, sel, plane0=2 * NSA_KV_HEADS,
                                 newk=kv_blk0 + 2,
                                 newv=kv_blk0 + 3, kbase=0, q0=p_len, window=None, gcol=NSA_GROUP,
                                 name="nsa_sel_paged")
        n_wt = WINDOW // PAGE_SIZE
        win_table = jnp.arange(b * n_wt, dtype=jnp.int32).reshape(b, n_wt)
        o_win = nsa_flash_sample(proj3, win_pool, win_table, tab, gates_r, None, plane0=0, newk=kv_blk0 + 4,
                                 newv=kv_blk0 + 5, kbase=p_len - WINDOW, q0=p_len, window=WINDOW,
                                 gcol=2 * NSA_GROUP, name="nsa_win_paged")
    y = proj((o_cmp.reshape(b * t, NSA_QDIM), o_sel.reshape(b * t, NSA_QDIM), o_win.reshape(b * t, NSA_QDIM)),
             nw["w_out"], res=h2d, name="nsa_out")
    new_rows = proj3[:, :, NSA_QDIM:NSA_QDIM + 4 * NSA_KVDIM].reshape(b, t, 4, NSA_KV_HEADS, NSA_HD)
    new_win = proj3[:, :, NSA_QDIM + 4 * NSA_KVDIM:].reshape(b, t, 2, NSA_KV_HEADS, NSA_HD)
    return y, new_rows, new_win


def _pad_cols(w, n_pad):
    return jnp.pad(w, ((0, 0), (0, n_pad - w.shape[1])))


def _nsa_in_post(q_norm, k_norm):
    ones_kv = jnp.ones((NSA_KVDIM,), F32)
    zeros_kv = jnp.zeros((NSA_KVDIM,), F32)
    pw = jnp.concatenate([
        jnp.tile(q_norm.astype(F32) * (NSA_HD ** -0.5), NSA_HEADS),
        ones_kv, ones_kv, jnp.tile(k_norm[1].astype(F32), NSA_KV_HEADS), ones_kv,
        jnp.tile(k_norm[2].astype(F32), NSA_KV_HEADS), ones_kv])
    pm = jnp.concatenate([
        jnp.ones((NSA_QDIM,), F32),
        zeros_kv, zeros_kv, ones_kv, zeros_kv, ones_kv, zeros_kv])
    return (pw, pm), (jnp.ones((LANE,), F32), jnp.full((LANE,), 2.0, F32))


def kernel(x_prompt, x_sample, state_gdn, state_gdn_conv, cache_nsa_kv, state_nsa_win, state_ffn_conv, page_table,
           norm_mix, norm_ffn, gdn_w_in, gdn_conv_w, gdn_A_log, gdn_dt_bias, gdn_norm, gdn_w_out,
           nsa_w_in, nsa_q_norm, nsa_k_norm, nsa_cmp_pe, nsa_cmp_w1, nsa_cmp_w2, rel_bias, nsa_w_out,
           ffn_w_up, ffn_conv_w, ffn_conv_b, ffn_w_down):
    depth = norm_mix.shape[0]
    bp, tp, d = x_prompt.shape
    bs, ts, _ = x_sample.shape
    win_buf = state_nsa_win.shape[2]
    assert win_buf == WINDOW and tp >= WINDOW and cache_nsa_kv.shape[2] == PAGE_SIZE
    d_ff = ffn_conv_w.shape[1]
    conv_keep = ffn_conv_w.shape[2] - 1
    hp = x_prompt.reshape(bp * tp, d)
    hs = x_sample.reshape(bs * ts, d)
    gdn_p, gdnc_p, kv_p, win_p, ffn_p = [], [], [], [], []
    gdn_s, gdnc_s, kv_s, win_s, ffn_s = [], [], [], [], []
    for i in range(depth):
        j = i // 2
        if i % 2 == 0:
            n_main = GDN_CONV_DIM + GDN_VDIM
            w_in = (gdn_w_in[j], _pad_cols(gdn_w_in[j][:, n_main:], LANE).astype(BF16))
            gw = (norm_mix[i], w_in, gdn_conv_w[j], gdn_A_log[j], gdn_dt_bias[j], gdn_norm[j],
                  gdn_w_out[j])
            hp, st_p, cv_p = gdn_mixer(hp, bp, tp, jnp.zeros((bp, GDN_V_HEADS, GDN_DK, GDN_DV), F32),
                                       jnp.zeros((bp, gdn_conv_w.shape[2] - 1, GDN_CONV_DIM), F32), *gw)
            hs, st_s, cv_s = gdn_mixer(hs, bs, ts, state_gdn[j], state_gdn_conv[j], *gw)
            gdn_p.append(st_p)
            gdnc_p.append(cv_p)
            gdn_s.append(st_s)
            gdnc_s.append(cv_s)
        else:
            nw = _nsa_weights(nsa_w_in[j], nsa_q_norm[j], nsa_k_norm[j], nsa_cmp_pe[j], nsa_cmp_w1[j],
                              nsa_cmp_w2[j], rel_bias, nsa_w_out[j])
            hp, rows_p, nwin_p = nsa_mixer(hp, bp, tp, norm_mix[i], nw, None)
            n_pool = cache_nsa_kv.shape[1]
            pool = cache_nsa_kv.reshape(cache_nsa_kv.shape[0] * n_pool, PAGE_SIZE, 4 * NSA_KV_HEADS, NSA_HD)
            win_pool = state_nsa_win[j].reshape(bs * (win_buf // PAGE_SIZE), PAGE_SIZE, 2 * NSA_KV_HEADS, NSA_HD)
            hs, rows_s, nwin_s = nsa_mixer(hs, bs, ts, norm_mix[i], nw, (pool, page_table + j * n_pool, win_pool))
            kv_p.append(rows_p)
            win_p.append(nwin_p[:, tp - WINDOW:])
            kv_s.append(rows_s)
            win_s.append(jnp.concatenate([state_nsa_win[j][:, ts:], nwin_s], axis=1))
        n_up = -(-ffn_w_up.shape[2] // 512) * 512
        fw = (norm_ffn[i], _pad_cols(ffn_w_up[i], n_up).astype(BF16), ffn_conv_w[i], ffn_conv_b[i], ffn_w_down[i].astype(BF16))
        hp, cp = conv_ffn(hp, bp, tp, jnp.zeros((bp, conv_keep, d_ff), F32), *fw)
        hs, cs = conv_ffn(hs, bs, ts, state_ffn_conv[i], *fw)
        ffn_p.append(cp)
        ffn_s.append(cs)
    return (hp.reshape(bp, tp, d), hs.reshape(bs, ts, d),
            jnp.stack(gdn_p), jnp.stack(gdnc_p), jnp.stack(kv_p), jnp.stack(win_p), jnp.stack(ffn_p),
            jnp.stack(gdn_s), jnp.stack(gdnc_s), jnp.stack(kv_s), jnp.stack(win_s), jnp.stack(ffn_s))
```
